```python
import jax, jax.numpy as jnp
from jax import lax
import numpy as np

D_MODEL = 1024
BATCH = 2
SEQ = 8192
DEPTH = 2

CTX_LEN = 256
GRID_W = 64
N_MOD = 6
NORM_EPS = 1e-6

ML_HEADS = 4
ML_DQK = 64
ML_DV = 128
ML_QK = ML_HEADS * ML_DQK
ML_V = ML_HEADS * ML_DV
ML_CHUNK = 64
GATE_CAP = 15.0
ML_COLS = 2 * ML_QK + 2 * ML_V + 4 * ML_HEADS

RW_HEADS = 8
RW_N = 64
RW_W = RW_HEADS * RW_N
RW_DECAY_LORA = 64
RW_A_LORA = 64
RW_GATE_LORA = 128
RW_GN_EPS = 6.4e-4
RW_COLS = 3 * RW_W + 2 * RW_DECAY_LORA + 2 * RW_A_LORA + RW_GATE_LORA

IN_COLS = ML_COLS + RW_COLS + 2 * D_MODEL

N_EXPERTS = 16
N_GROUPS = 4
EXPERTS_PER_GROUP = N_EXPERTS // N_GROUPS
GROUP_SCORE_TOPK = 2
TOP_K = 2
D_EXPERT = 512

kernel_name = "bidir_mlstm_rwkv7_gated_moe_dit"


def _split(u, sizes):
    idx = [int(i) for i in np.cumsum(sizes)[:-1]]
    return jnp.split(u, idx, axis=-1)


def _rmsnorm(u, g):
    uf = u.astype(jnp.float32)
    y = uf * lax.rsqrt(jnp.mean(uf * uf, axis=-1, keepdims=True) + NORM_EPS)
    return (y * g.astype(jnp.float32)).astype(u.dtype)


def _prev(u):
    return jnp.pad(u[:, :-1], ((0, 0), (1, 0), (0, 0)))


def _next(u):
    return jnp.pad(u[:, 1:], ((0, 0), (0, 1), (0, 0)))


def _grid_conv(u, taps, bias):
    bsz, t, ch = u.shape
    rows = t // GRID_W
    img = u.reshape(bsz, rows, GRID_W, ch)
    out = lax.conv_general_dilated(img, taps[:, :, None, :].astype(u.dtype), (1, 1), "SAME",
                                   dimension_numbers=("NHWC", "HWIO", "NHWC"), feature_group_count=ch)
    return out.reshape(bsz, t, ch) + bias


def _line_conv(u, taps, bias):
    row = taps[1]
    return row[0] * _prev(u) + row[1] * u + row[2] * _next(u) + bias


def _mlstm_zero_state(bsz):
    f32 = jnp.float32
    return (jnp.zeros((bsz, ML_HEADS, ML_DQK, ML_DV), f32), jnp.zeros((bsz, ML_HEADS, ML_DQK), f32),
            jnp.zeros((bsz, ML_HEADS), f32))


def _mlstm_state_update(state, k, v, ig, b):
    C, n, m = state
    g_end = b[..., -1:] - b + ig
    m_new = jnp.maximum(b[..., -1] + m, jnp.max(g_end, axis=-1))
    wk = jnp.exp(g_end - m_new[..., None])
    carry = jnp.exp(b[..., -1] + m - m_new)
    C_new = carry[..., None, None] * C + jnp.einsum("bhsd,bhse->bhde", k * wk[..., None], v)
    n_new = carry[..., None] * n + jnp.einsum("bhs,bhsd->bhd", wk, k)
    return (C_new, n_new, m_new)


def _mlstm_segment(state, q, k, v, ig, lf):
    C, n, m = state
    L = q.shape[2]
    b = jnp.cumsum(lf, axis=-1)
    logw = b[..., :, None] - b[..., None, :] + ig[..., None, :]
    logw = jnp.where(jnp.tril(jnp.ones((L, L), dtype=bool)), logw, -jnp.inf)
    inter = b + m[..., None]
    m_t = jnp.maximum(inter, jnp.max(logw, axis=-1))
    s = jnp.einsum("bhtd,bhsd->bhts", q, k) * jnp.exp(logw - m_t[..., None])
    w_inter = jnp.exp(inter - m_t)
    num = w_inter[..., None] * jnp.einsum("bhtd,bhde->bhte", q, C) + jnp.einsum("bhts,bhse->bhte", s, v)
    den = w_inter * jnp.einsum("bhtd,bhd->bht", q, n) + jnp.sum(s, axis=-1)
    h = num / jnp.maximum(jnp.abs(den), jnp.exp(-m_t))[..., None]
    return _mlstm_state_update(state, k, v, ig, b), h


def _mlstm_run(q, k, v, ig, lf, state, reverse):
    if reverse:
        q, k, v, ig, lf = (jnp.flip(a, axis=2) for a in (q, k, v, ig, lf))
    bsz, nh, t, _ = q.shape
    nc = t // ML_CHUNK

    def chunks(a):
        return jnp.moveaxis(a.reshape(bsz, nh, nc, ML_CHUNK, *a.shape[3:]), 2, 0)

    state, hs = lax.scan(lambda st, xs: _mlstm_segment(st, *xs), state,
                         tuple(chunks(a) for a in (q, k, v, ig, lf)))
    h = jnp.moveaxis(hs, 0, 2).reshape(bsz, nh, t, ML_DV)
    if reverse:
        h = jnp.flip(h, axis=2)
    return state, h


def _mlstm_final_state(k, v, ig, lf, state, reverse):
    if reverse:
        k, v, ig, lf = (jnp.flip(a, axis=2) for a in (k, v, ig, lf))
    return _mlstm_state_update(state, k, v, ig, jnp.cumsum(lf, axis=-1))


def _mlstm_inputs(p, conv, gate_b):
    f32 = jnp.float32
    qk, v, o, gates = _split(p, [2 * ML_QK, ML_V, ML_V, 4 * ML_HEADS])
    q, k = _split(conv(qk), [ML_QK, ML_QK])
    bsz, t, _ = p.shape

    def heads(a, d):
        return jnp.transpose(a.reshape(bsz, t, ML_HEADS, d), (0, 2, 1, 3)).astype(f32)

    q = heads(q, ML_DQK)
    k = heads(k, ML_DQK) * (ML_DQK ** -0.5)
    v = heads(v, ML_DV)
    pre = gates.reshape(bsz, t, 2, 2, ML_HEADS).astype(f32) + gate_b.astype(f32)
    pre = GATE_CAP * jnp.tanh(pre / GATE_CAP)
    pre = jnp.transpose(pre, (2, 3, 0, 4, 1))
    return q, k, v, o, pre[0], jax.nn.log_sigmoid(pre[1])


def _mlstm_readout(h, o, norm_g):
    h = h * lax.rsqrt(jnp.mean(h * h, axis=-1, keepdims=True) + NORM_EPS)
    bsz, nh, t, dv = h.shape
    h = jnp.transpose(h, (0, 2, 1, 3)).reshape(bsz, t, nh * dv)
    return (h * norm_g * jax.nn.sigmoid(o.astype(jnp.float32))).astype(o.dtype)


def _rwkv_inputs(p, mu, w_up, w0, a_up, a0, k_k, k_a):
    f32 = jnp.float32
    p = p + mu * (0.5 * (_prev(p) + _next(p)) - p)
    r, k, v, wd, ad, gd = _split(p, [RW_W, RW_W, RW_W, 2 * RW_DECAY_LORA, 2 * RW_A_LORA, RW_GATE_LORA])
    bsz, t, _ = p.shape
    wd = wd.reshape(bsz, t, 2, RW_DECAY_LORA)
    ad = ad.reshape(bsz, t, 2, RW_A_LORA)
    lw = (w0 + jnp.einsum("btdr,drc->btdc", jnp.tanh(wd), w_up)).astype(f32)
    decay = jnp.exp(-jnp.exp(-jax.nn.softplus(-lw) - 0.5))
    a = jax.nn.sigmoid((a0 + jnp.einsum("btdr,drc->btdc", ad, a_up)).astype(f32))

    def heads(u):
        return u.reshape(*u.shape[:-1], RW_HEADS, RW_N)

    kf = k.astype(f32)
    kk = heads(kf * k_k)
    kk = kk * lax.rsqrt(jnp.sum(kk * kk, axis=-1, keepdims=True) + 1e-12)
    kd = kf[:, :, None, :] * (1.0 + (a - 1.0) * k_a)
    return heads(r.astype(f32)), heads(kd), heads(v.astype(f32)), heads(decay), kk, heads(a), gd


def _rwkv7_update(S, k, v, w, kk, a):
    sa = jnp.einsum("bhij,bhj->bhi", S, kk)
    return S * w[:, :, None, :] - sa[..., None] * (kk * a)[:, :, None, :] + v[..., None] * k[:, :, None, :]


def _rwkv7_run(r, k, v, w, kk, a, state, reverse):
    xs = tuple(jnp.moveaxis(u, 1, 0) for u in (r, k, v, w, kk, a))

    def step(S, inp):
        r_t, k_t, v_t, w_t, kk_t, a_t = inp
        S = _rwkv7_update(S, k_t, v_t, w_t, kk_t, a_t)
        return S, jnp.einsum("bhij,bhj->bhi", S, r_t)

    state, ys = lax.scan(step, state, xs, reverse=reverse)
    return state, jnp.moveaxis(ys, 0, 1)


def _rwkv7_final_state(k, v, w, kk, a, state, reverse):
    xs = tuple(jnp.moveaxis(u, 1, 0) for u in (k, v, w, kk, a))
    state, _ = lax.scan(lambda S, inp: (_rwkv7_update(S, *inp), None), state, xs, reverse=reverse)
    return state


def _rwkv_readout(y, r, kd, v, gd, r_k, ln_w, ln_b, g_up):
    bsz, t = y.shape[:2]
    mu = jnp.mean(y, axis=-1, keepdims=True)
    var = jnp.mean(jnp.square(y - mu), axis=-1, keepdims=True)
    yn = ((y - mu) * lax.rsqrt(var + RW_GN_EPS)).reshape(bsz, t, RW_W) * ln_w + ln_b
    bonus = jnp.sum(r[:, :, None] * kd * r_k, axis=(2, 4))[..., None] * v
    g = jax.nn.sigmoid(gd) @ g_up
    return ((yn + bonus.reshape(bsz, t, RW_W)) * g).astype(g.dtype)


def _merge(ha, hb, gates, p_a, p_b, w_out):
    ga, gb = _split(gates, [D_MODEL, D_MODEL])
    return (jax.nn.sigmoid(ga) * (ha @ p_a) + jax.nn.sigmoid(gb) * (hb @ p_b)) @ w_out


def _hybrid_mixer(h_lat, h_ctx, w_in, conv_k, conv_b, gate_b, ml_norm_g, rw_mu, rw_w_up, rw_w0, rw_a_up,
                  rw_a0, rw_g_up, rw_k_k, rw_k_a, rw_r_k, rw_ln_w, rw_ln_b, p_a, p_b, w_out, need_ctx):
    bsz = h_lat.shape[0]
    ml_l, rw_l, gt_l = _split(h_lat @ w_in, [ML_COLS, RW_COLS, 2 * D_MODEL])
    ml_c, rw_c, gt_c = _split(h_ctx @ w_in, [ML_COLS, RW_COLS, 2 * D_MODEL])

    ql, kl, vl, ol, igl, lfl = _mlstm_inputs(ml_l, lambda u: _grid_conv(u, conv_k, conv_b), gate_b)
    qc, kc, vc, oc, igc, lfc = _mlstm_inputs(ml_c, lambda u: _line_conv(u, conv_k, conv_b), gate_b)
    zero = _mlstm_zero_state(bsz)
    if need_ctx:
        mst_f, mhc_f = _mlstm_run(qc, kc, vc, igc[0], lfc[0], zero, False)
        mst_b, mhc_b = _mlstm_run(qc, kc, vc, igc[1], lfc[1], zero, True)
    else:
        mst_f = _mlstm_final_state(kc, vc, igc[0], lfc[0], zero, False)
        mst_b = _mlstm_final_state(kc, vc, igc[1], lfc[1], zero, True)
    _, mhl_f = _mlstm_run(ql, kl, vl, igl[0], lfl[0], mst_f, False)
    _, mhl_b = _mlstm_run(ql, kl, vl, igl[1], lfl[1], mst_b, True)
    a_lat = _mlstm_readout(mhl_f + mhl_b, ol, ml_norm_g)

    rw_par = (rw_mu, rw_w_up, rw_w0, rw_a_up, rw_a0, rw_k_k, rw_k_a)
    rl, kdl, vvl, wl, kkl, al, gdl = _rwkv_inputs(rw_l, *rw_par)
    rc, kdc, vvc, wc, kkc, ac, gdc = _rwkv_inputs(rw_c, *rw_par)
    s0 = jnp.zeros((bsz, RW_HEADS, RW_N, RW_N), jnp.float32)
    if need_ctx:
        rst_f, ryc_f = _rwkv7_run(rc, kdc[:, :, 0], vvc, wc[:, :, 0], kkc, ac[:, :, 0], s0, False)
        rst_b, ryc_b = _rwkv7_run(rc, kdc[:, :, 1], vvc, wc[:, :, 1], kkc, ac[:, :, 1], s0, True)
    else:
        rst_f = _rwkv7_final_state(kdc[:, :, 0], vvc, wc[:, :, 0], kkc, ac[:, :, 0], s0, False)
        rst_b = _rwkv7_final_state(kdc[:, :, 1], vvc, wc[:, :, 1], kkc, ac[:, :, 1], s0, True)
    _, ryl_f = _rwkv7_run(rl, kdl[:, :, 0], vvl, wl[:, :, 0], kkl, al[:, :, 0], rst_f, False)
    _, ryl_b = _rwkv7_run(rl, kdl[:, :, 1], vvl, wl[:, :, 1], kkl, al[:, :, 1], rst_b, True)
    b_lat = _rwkv_readout(ryl_f + ryl_b, rl, kdl, vvl, gdl, rw_r_k, rw_ln_w, rw_ln_b, rw_g_up)

    out_lat = _merge(a_lat, b_lat, gt_l, p_a, p_b, w_out)
    out_ctx = None
    if need_ctx:
        a_ctx = _mlstm_readout(mhc_f + mhc_b, oc, ml_norm_g)
        b_ctx = _rwkv_readout(ryc_f + ryc_b, rc, kdc, vvc, gdc, rw_r_k, rw_ln_w, rw_ln_b, rw_g_up)
        out_ctx = _merge(a_ctx, b_ctx, gt_c, p_a, p_b, w_out)
    return out_lat, out_ctx


def _moe(h, router_w, router_b, w_gate, w_up, w_down):
    bsz, t, d = h.shape
    hf = h.reshape(bsz * t, d)
    scores = jax.nn.sigmoid((hf @ router_w).astype(jnp.float32))
    sel = scores + router_b.astype(jnp.float32)
    grp = sel.reshape(-1, N_GROUPS, EXPERTS_PER_GROUP)
    group_score = jnp.sum(lax.top_k(grp, GROUP_SCORE_TOPK)[0], axis=-1)
    best = jnp.argmax(group_score, axis=-1)
    in_group = (jnp.arange(N_EXPERTS) // EXPERTS_PER_GROUP)[None, :] == best[:, None]
    _, idx = lax.top_k(jnp.where(in_group, sel, -jnp.inf), TOP_K)
    wsel = jnp.take_along_axis(scores, idx, axis=-1)
    wsel = wsel / jnp.sum(wsel, axis=-1, keepdims=True)
    comb = jnp.sum(jax.nn.one_hot(idx, N_EXPERTS, dtype=jnp.float32) * wsel[..., None], axis=1).astype(h.dtype)
    y = jnp.zeros_like(hf)
    for e in range(N_EXPERTS):
        hid = jax.nn.silu(hf @ w_gate[e]) * (hf @ w_up[e])
        y = y + (comb[:, e:e + 1] * hid) @ w_down[e]
    return y.reshape(bsz, t, d)


def setup_inputs(seed: int = 0) -> dict:
    key = jax.random.key(seed)
    ks = list(jax.random.split(key, 40))
    f32 = jnp.float32
    L = DEPTH

    def nrm(i, shape, scale):
        return scale * jax.random.normal(ks[i], shape, f32)

    def gain(i, shape):
        return 1.0 + nrm(i, shape, 0.02)

    i_bias = nrm(12, (L, 1, 2, ML_HEADS), 0.1)
    f_bias = jax.random.uniform(ks[13], (L, 1, 2, ML_HEADS), f32, 3.0, 6.0)
    centre = jnp.zeros((3, 3, 1), f32).at[1, 1, 0].set(1.0)
    return {
        "x": nrm(0, (BATCH, SEQ, D_MODEL), 1.0),
        "c": nrm(1, (BATCH, D_MODEL), 1.0),
        "ctx": nrm(2, (BATCH, CTX_LEN, D_MODEL), 1.0),
        "c_ctx": nrm(3, (D_MODEL,), 1.0),
        "w_ada": nrm(4, (L, D_MODEL, N_MOD * D_MODEL), 0.5 * D_MODEL ** -0.5),
        "b_ada": nrm(5, (L, N_MOD * D_MODEL), 0.02),
        "norm1_g": gain(6, (L, D_MODEL)),
        "norm2_g": gain(7, (L, D_MODEL)),
        "w_in": nrm(8, (L, D_MODEL, IN_COLS), D_MODEL ** -0.5),
        "ml_conv_k": centre + nrm(9, (L, 3, 3, 2 * ML_QK), 0.1),
        "ml_conv_b": nrm(10, (L, 2 * ML_QK), 0.02),
        "ml_gate_b": jnp.concatenate([i_bias, f_bias], axis=1),
        "ml_norm_g": gain(11, (L, ML_V)),
        "rw_mu": jax.random.uniform(ks[14], (L, RW_COLS), f32, 0.0, 1.0),
        "rw_w_up": nrm(15, (L, 2, RW_DECAY_LORA, RW_W), 0.5 * RW_DECAY_LORA ** -0.5),
        "rw_w0": jax.random.uniform(ks[16], (L, 2, RW_W), f32, -6.0, -1.0),
        "rw_a_up": nrm(17, (L, 2, RW_A_LORA, RW_W), 0.5 * RW_A_LORA ** -0.5),
        "rw_a0": nrm(18, (L, 2, RW_W), 0.1),
        "rw_g_up": nrm(19, (L, RW_GATE_LORA, RW_W), RW_GATE_LORA ** -0.5),
        "rw_k_k": 0.85 + nrm(20, (L, RW_W), 0.02),
        "rw_k_a": gain(21, (L, RW_W)),
        "rw_r_k": nrm(22, (L, RW_HEADS, RW_N), 0.1),
        "rw_ln_w": gain(23, (L, RW_W)),
        "rw_ln_b": nrm(24, (L, RW_W), 0.02),
        "merge_pa": nrm(25, (L, ML_V, D_MODEL), ML_V ** -0.5),
        "merge_pb": nrm(26, (L, RW_W, D_MODEL), RW_W ** -0.5),
        "w_out": nrm(27, (L, D_MODEL, D_MODEL), D_MODEL ** -0.5),
        "router_w": nrm(28, (D_MODEL, N_EXPERTS), D_MODEL ** -0.5),
        "router_b": nrm(29, (N_EXPERTS,), 0.01),
        "exp_w_gate": nrm(30, (L, N_EXPERTS, D_MODEL, D_EXPERT), D_MODEL ** -0.5),
        "exp_w_up": nrm(31, (L, N_EXPERTS, D_MODEL, D_EXPERT), D_MODEL ** -0.5),
        "exp_w_down": nrm(32, (L, N_EXPERTS, D_EXPERT, D_MODEL), D_EXPERT ** -0.5),
        "final_g": gain(33, (D_MODEL,)),
    }


def reference(x, c, ctx, c_ctx, w_ada, b_ada, norm1_g, norm2_g, w_in, ml_conv_k, ml_conv_b, ml_gate_b,
              ml_norm_g, rw_mu, rw_w_up, rw_w0, rw_a_up, rw_a0, rw_g_up, rw_k_k, rw_k_a, rw_r_k, rw_ln_w,
              rw_ln_b, merge_pa, merge_pb, w_out, router_w, router_b, exp_w_gate, exp_w_up, exp_w_down,
              final_g):
    f32 = jnp.float32
    s_lat = jax.nn.silu(c.astype(f32))
    s_ctx = jax.nn.silu(c_ctx.astype(f32))
    x_lat, x_ctx = x, ctx
    for l in range(DEPTH):
        last = l == DEPTH - 1
        sh1, sc1, g1, sh2, sc2, g2 = _split((s_lat @ w_ada[l] + b_ada[l]).astype(x.dtype)[:, None, :],
                                            [D_MODEL] * N_MOD)
        csh1, csc1, cg1, csh2, csc2, cg2 = _split((s_ctx @ w_ada[l] + b_ada[l]).astype(x.dtype),
                                                  [D_MODEL] * N_MOD)
        h_lat = _rmsnorm(x_lat, norm1_g[l]) * (1 + sc1) + sh1
        h_ctx = _rmsnorm(x_ctx, norm1_g[l]) * (1 + csc1) + csh1
        mix_lat, mix_ctx = _hybrid_mixer(
            h_lat, h_ctx, w_in[l], ml_conv_k[l], ml_conv_b[l], ml_gate_b[l], ml_norm_g[l], rw_mu[l],
            rw_w_up[l], rw_w0[l], rw_a_up[l], rw_a0[l], rw_g_up[l], rw_k_k[l], rw_k_a[l], rw_r_k[l],
            rw_ln_w[l], rw_ln_b[l], merge_pa[l], merge_pb[l], w_out[l], not last)
        moe_w = (router_w, router_b, exp_w_gate[l], exp_w_up[l], exp_w_down[l])
        x_lat = x_lat + g1 * mix_lat
        x_lat = x_lat + g2 * _moe(_rmsnorm(x_lat, norm2_g[l]) * (1 + sc2) + sh2, *moe_w)
        if not last:
            x_ctx = x_ctx + cg1 * mix_ctx
            x_ctx = x_ctx + cg2 * _moe(_rmsnorm(x_ctx, norm2_g[l]) * (1 + csc2) + csh2, *moe_w)
    return _rmsnorm(x_lat, final_g)
```

```python
import functools

import jax
import jax.numpy as jnp
import numpy as np
from jax import lax
from jax.experimental import pallas as pl
from jax.experimental.pallas import tpu as pltpu

f32 = jnp.float32
bf16 = jnp.bfloat16

D_MODEL = 1024
DEPTH = 2
GRID_W = 64
N_MOD = 6
NORM_EPS = 1e-6

ML_HEADS = 4
ML_DQK = 64
ML_DV = 128
ML_QK = ML_HEADS * ML_DQK
ML_V = ML_HEADS * ML_DV
GATE_CAP = 15.0
ML_COLS = 2 * ML_QK + 2 * ML_V + 4 * ML_HEADS
ML_CHUNK = 256

RW_HEADS = 8
RW_N = 64
RW_W = RW_HEADS * RW_N
RW_DECAY_LORA = 64
RW_A_LORA = 64
RW_GATE_LORA = 128
RW_GN_EPS = 6.4e-4
RW_COLS = 3 * RW_W + 2 * RW_DECAY_LORA + 2 * RW_A_LORA + RW_GATE_LORA
RW_CHUNK = 64
RW_BLOCK = 256

N_EXPERTS = 16
N_GROUPS = 4
EXPERTS_PER_GROUP = N_EXPERTS // N_GROUPS
D_EXPERT = 512

P_GA = 0
P_GB = D_MODEL
P_RW = 2 * D_MODEL
P_RW_PAD = 2048
P_QK = P_RW + P_RW_PAD
P_V = P_QK + 2 * ML_QK
P_O = P_V + ML_V
P_MLG = P_O + ML_V
P_MLG_PAD = 256
P_COLS = P_MLG + P_MLG_PAD
PROJ_TN = 256

VMEM_LIMIT = 48 * 1024 * 1024
EXP_NEG_HALF = float(np.exp(-0.5))
NEG_BIG = -1e30


_NN = ((1,), (0,))
_NT = ((1,), (1,))
_TN = ((0,), (0,))


def _dot(a, b, dims=_NN):
    return lax.dot_general(a, b, (dims, ((), ())), preferred_element_type=f32)


def _mm(a, b, dims=_NN):
    return _dot(a.astype(bf16), b.astype(bf16), dims)


def _hi_lo(x):
    hi = x.astype(bf16)
    lo = (x - hi.astype(f32)).astype(bf16)
    return hi, lo


def _mm3(a, b, dims=_NN):
    ah, al = _hi_lo(a)
    bh, bl = _hi_lo(b)
    return _dot(ah, bh, dims) + (_dot(ah, bl, dims) + _dot(al, bh, dims))


def _mm_exact_lhs(a_bf, b, dims=_NN):
    hi = b.astype(bf16)
    r1 = b - hi.astype(f32)
    mid = r1.astype(bf16)
    lo = (r1 - mid.astype(f32)).astype(bf16)
    return _dot(a_bf, hi, dims) + (_dot(a_bf, mid, dims) + _dot(a_bf, lo, dims))


def _mm_exact_rhs(a, b_bf, dims=_NN):
    hi = a.astype(bf16)
    r1 = a - hi.astype(f32)
    mid = r1.astype(bf16)
    lo = (r1 - mid.astype(f32)).astype(bf16)
    return _dot(hi, b_bf, dims) + (_dot(mid, b_bf, dims) + _dot(lo, b_bf, dims))


def _sigmoid(x):
    return 1.0 / (1.0 + jnp.exp(-x))


def _norm_mod(x, g, sc, sh):
    y = x * lax.rsqrt(jnp.mean(x * x, axis=-1, keepdims=True) + NORM_EPS)
    return (y * g) * (1.0 + sc) + sh


def _ada_kernel(s_ref, w_ref, b_ref, o_ref):
    s = s_ref[...]
    s = s * _sigmoid(s)
    o_ref[...] = _mm3(s, w_ref[...]) + b_ref[...]


def _ada(s_rows, w_ada, b_ada):
    tn = 1536
    n = N_MOD * D_MODEL
    return pl.pallas_call(
        _ada_kernel,
        grid=(DEPTH, n // tn),
        in_specs=[
            pl.BlockSpec((8, D_MODEL), lambda l, j: (0, 0)),
            pl.BlockSpec((None, D_MODEL, tn), lambda l, j: (l, 0, j)),
            pl.BlockSpec((None, 1, tn), lambda l, j: (l, 0, j)),
        ],
        out_specs=pl.BlockSpec((None, 8, tn), lambda l, j: (l, 0, j)),
        out_shape=jax.ShapeDtypeStruct((DEPTH, 8, n), f32),
        compiler_params=pltpu.CompilerParams(
            dimension_semantics=("arbitrary", "arbitrary"), vmem_limit_bytes=VMEM_LIMIT),
        name="ada",
    )(s_rows, w_ada, b_ada.reshape(DEPTH, 1, n))


def _proj_kernel(x_ref, g_ref, sc_ref, sh_ref, w_ref, o_ref, h_scr):
    @pl.when(pl.program_id(2) == 0)
    def _():
        h_scr[...] = _norm_mod(x_ref[...], g_ref[...], sc_ref[...], sh_ref[...]).astype(bf16)

    o_ref[...] = _dot(h_scr[...], w_ref[...])


def _proj(x, g, sc, sh, w_bf):
    bsz, t, _ = x.shape
    tm = min(t, 1024)
    return pl.pallas_call(
        _proj_kernel,
        grid=(bsz, t // tm, P_COLS // PROJ_TN),
        in_specs=[
            pl.BlockSpec((None, tm, D_MODEL), lambda b, i, j: (b, i, 0)),
            pl.BlockSpec((1, D_MODEL), lambda b, i, j: (0, 0)),
            pl.BlockSpec((None, 1, D_MODEL), lambda b, i, j: (b, 0, 0)),
            pl.BlockSpec((None, 1, D_MODEL), lambda b, i, j: (b, 0, 0)),
            pl.BlockSpec((D_MODEL, PROJ_TN), lambda b, i, j: (0, j)),
        ],
        out_specs=pl.BlockSpec((None, tm, PROJ_TN), lambda b, i, j: (b, i, j)),
        out_shape=jax.ShapeDtypeStruct((bsz, t, P_COLS), f32),
        scratch_shapes=[pltpu.VMEM((tm, D_MODEL), bf16)],
        compiler_params=pltpu.CompilerParams(
            dimension_semantics=("parallel", "parallel", "arbitrary"), vmem_limit_bytes=VMEM_LIMIT),
        name="proj",
    )(x, g, sc, sh, w_bf)


def _mlstm_kernel(q_ref, k_ref, v_ref, gc_ref, gr_ref, c0_ref, h_ref, cfin_ref, c_scr, *, rev):
    L = ML_CHUNK

    @pl.when(pl.program_id(1) == 0)
    def _():
        c_scr[...] = c0_ref[...]

    ii = lax.broadcasted_iota(jnp.int32, (L, L), 0)
    jj = lax.broadcasted_iota(jnp.int32, (L, L), 1)
    incl = (jj >= ii) if rev else (jj <= ii)
    tri = incl.astype(bf16)
    gc = gc_ref[...]
    gr = gr_ref[...]
    bc = _mm_exact_lhs(tri, gc)
    br = _mm_exact_rhs(gr, tri, _NT)
    one_col = (lax.broadcasted_iota(jnp.int32, (L, ML_DV), 1) == 0).astype(f32)
    q = q_ref[...]
    k = k_ref[...]
    v = v_ref[...]
    for h in range(ML_HEADS):
        qh = q[:, h * ML_DQK:(h + 1) * ML_DQK]
        kh = k[:, h * ML_DQK:(h + 1) * ML_DQK]
        vh = v[:, h * ML_DV:(h + 1) * ML_DV]
        ig_col = gc[:, h:h + 1]
        b_col = bc[:, ML_HEADS + h:ML_HEADS + h + 1]
        ig_row = gr[h:h + 1, :]
        b_row = br[ML_HEADS + h:ML_HEADS + h + 1, :]
        btot = b_col[0:1, :] if rev else b_col[L - 1:L, :]
        decay = jnp.exp(jnp.where(incl, (b_col - b_row) + ig_row, NEG_BIG))
        s = _mm(qh, kh, _NT) * decay
        vp = jnp.concatenate([vh, one_col], axis=1)
        cp = c_scr[h]
        nd = _mm(s, vp) + jnp.exp(b_col) * _mm(qh, cp)
        num = nd[:, :ML_DV]
        den = nd[:, ML_DV:ML_DV + 1]
        h_ref[:, h * ML_DV:(h + 1) * ML_DV] = num / jnp.maximum(jnp.abs(den), 1.0)
        kw = kh * jnp.exp((btot - b_col) + ig_col)
        c_scr[h] = jnp.exp(btot) * cp + _mm(kw, vp, _TN)
    cfin_ref[...] = c_scr[...]


def _mlstm(q, k, pfull, gcol, grow, c0, rev):
    bsz, t, _ = q.shape
    L = ML_CHUNK
    nb = t // L
    tb = (lambda j: nb - 1 - j) if rev else (lambda j: j)
    vblk = P_V // ML_V
    return pl.pallas_call(
        functools.partial(_mlstm_kernel, rev=rev),
        grid=(bsz, nb),
        in_specs=[
            pl.BlockSpec((None, L, ML_QK), lambda b, j: (b, tb(j), 0)),
            pl.BlockSpec((None, L, ML_QK), lambda b, j: (b, tb(j), 0)),
            pl.BlockSpec((None, L, ML_V), lambda b, j: (b, tb(j), vblk)),
            pl.BlockSpec((None, L, 2 * ML_HEADS), lambda b, j: (b, tb(j), 0)),
            pl.BlockSpec((None, 2 * ML_HEADS, L), lambda b, j: (b, 0, tb(j))),
            pl.BlockSpec((None, ML_HEADS, ML_DQK, 2 * ML_DV), lambda b, j: (b, 0, 0, 0)),
        ],
        out_specs=[
            pl.BlockSpec((None, L, ML_V), lambda b, j: (b, tb(j), 0)),
            pl.BlockSpec((None, ML_HEADS, ML_DQK, 2 * ML_DV), lambda b, j: (b, 0, 0, 0)),
        ],
        out_shape=[
            jax.ShapeDtypeStruct((bsz, t, ML_V), f32),
            jax.ShapeDtypeStruct((bsz, ML_HEADS, ML_DQK, 2 * ML_DV), f32),
        ],
        scratch_shapes=[pltpu.VMEM((ML_HEADS, ML_DQK, 2 * ML_DV), f32)],
        compiler_params=pltpu.CompilerParams(
            dimension_semantics=("parallel", "arbitrary"), vmem_limit_bytes=VMEM_LIMIT),
        name="mlstm_bwd" if rev else "mlstm_fwd",
    )(q, k, pfull, gcol, grow, c0)


def _rwkv_kernel(p_ref, s0_ref, wup_ref, w0_ref, aup_ref, a0_ref, kk_ref, ka_ref, rk_ref, ones_ref,
                 y_ref, bv_ref, sfin_ref,
                 st_scr, r_scr, v_scr, kk_scr, kd_scr, be_scr, ld_scr, *, rev, d):
    L = RW_CHUNK
    nch = RW_BLOCK // L

    @pl.when(pl.program_id(1) == 0)
    def _():
        st_scr[...] = s0_ref[...]

    p = p_ref[...]
    r = p[:, 0:RW_W]
    k = p[:, RW_W:2 * RW_W]
    v = p[:, 2 * RW_W:3 * RW_W]
    o_wd = 3 * RW_W + d * RW_DECAY_LORA
    o_ad = 3 * RW_W + 2 * RW_DECAY_LORA + d * RW_A_LORA
    wd = p[:, o_wd:o_wd + RW_DECAY_LORA]
    ad = p[:, o_ad:o_ad + RW_A_LORA]
    lw = w0_ref[...] + _mm(jnp.tanh(wd), wup_ref[...])
    ld_scr[...] = -EXP_NEG_HALF * _sigmoid(lw)
    a = _sigmoid(a0_ref[...] + _mm(ad, aup_ref[...]))
    ones_bd = ones_ref[...]
    kkr = k * kk_ref[...]
    kk = kkr * lax.rsqrt(_mm_exact_rhs(kkr * kkr, ones_bd) + 1e-12)
    kd = k * (1.0 + (a - 1.0) * ka_ref[...])
    bonus = _mm_exact_rhs(r * kd * rk_ref[...], ones_bd)
    bv_ref[...] = bonus * v
    r_scr[...] = r
    v_scr[...] = v
    kk_scr[...] = kk
    kd_scr[...] = kd
    be_scr[...] = kk * a

    ii = lax.broadcasted_iota(jnp.int32, (L, L), 0)
    jj = lax.broadcasted_iota(jnp.int32, (L, L), 1)
    incl = (jj >= ii) if rev else (jj <= ii)
    strict = (jj > ii) if rev else (jj < ii)
    tri = incl.astype(bf16)
    eye = (ii == jj).astype(f32)

    def chunk(i, carry):
        c = (nch - 1 - i) if rev else i
        rows = pl.ds(pl.multiple_of(c * L, L), L)
        ld = ld_scr[rows, :]
        b = _mm_exact_lhs(tri, ld)
        bx = b - ld
        btot = b[0:1, :] if rev else b[L - 1:L, :]
        e_b = jnp.exp(b)
        e_nb = jnp.exp(-b)
        e_end = jnp.exp(btot - b)
        kk_c = kk_scr[rows, :]
        kd_c = kd_scr[rows, :]
        be_c = be_scr[rows, :]
        v_c = v_scr[rows, :]
        aw = -kk_c * jnp.exp(bx)
        rw = r_scr[rows, :] * e_b
        bi = be_c * e_nb
        ki = kd_c * e_nb
        bw = be_c * e_end
        kw = kd_c * e_end
        wend = jnp.exp(btot)
        ys = []
        for h in range(RW_HEADS):
            cs = slice(h * RW_N, (h + 1) * RW_N)
            aw_h = aw[:, cs]
            rw_h = rw[:, cs]
            v_h = v_c[:, cs]
            m = _mm(jnp.concatenate([aw_h, rw_h], axis=0),
                    jnp.concatenate([bi[:, cs], ki[:, cs]], axis=0), _NT)
            n_mat = jnp.where(strict, m[:L, :L], 0.0)
            mak = jnp.where(strict, m[:L, L:], 0.0)
            mrb = jnp.where(incl, m[L:, :L], 0.0)
            mrk = jnp.where(incl, m[L:, L:], 0.0)
            x = jnp.concatenate([aw_h, _mm(mak, v_h)], axis=1)
            pw = n_mat
            for it in range(6):
                x = x + _mm(pw, x)
                if it < 5:
                    pw = _mm(pw, pw)
            ray = _mm(mrb, x)
            ra = rw_h + ray[:, :RW_N]
            y0 = ray[:, RW_N:] + _mm(mrk, v_h)
            gh = _mm(bw[:, cs], x, _TN)
            g_mat = eye * wend[:, cs] + gh[:, :RW_N]
            h_mat = gh[:, RW_N:] + _mm(kw[:, cs], v_h, _TN)
            st = st_scr[h]
            yg = _mm3(jnp.concatenate([ra, g_mat], axis=0), st)
            ys.append(yg[:L] + y0)
            st_scr[h] = yg[L:] + h_mat
        y_ref[rows, :] = jnp.concatenate(ys, axis=1)
        return carry

    lax.fori_loop(0, nch, chunk, 0)
    sfin_ref[...] = st_scr[...]


def _rwkv(p_sh, s0, wup, w0, aup, a0, k_k, k_a, r_k, ones_bd, rev, d):
    bsz, t, _ = p_sh.shape
    tb_ = RW_BLOCK
    nb = t // tb_
    tb = (lambda j: nb - 1 - j) if rev else (lambda j: j)
    vec = lambda: pl.BlockSpec((1, RW_W), lambda b, j: (0, 0))
    st_spec = pl.BlockSpec((None, RW_HEADS, RW_N, RW_N), lambda b, j: (b, 0, 0, 0))
    return pl.pallas_call(
        functools.partial(_rwkv_kernel, rev=rev, d=d),
        grid=(bsz, nb),
        in_specs=[
            pl.BlockSpec((None, tb_, RW_COLS), lambda b, j: (b, tb(j), 0)),
            st_spec,
            pl.BlockSpec((RW_DECAY_LORA, RW_W), lambda b, j: (0, 0)),
            vec(),
            pl.BlockSpec((RW_A_LORA, RW_W), lambda b, j: (0, 0)),
            vec(), vec(), vec(), vec(),
            pl.BlockSpec((RW_W, RW_W), lambda b, j: (0, 0)),
        ],
        out_specs=[
            pl.BlockSpec((None, tb_, RW_W), lambda b, j: (b, tb(j), 0)),
            pl.BlockSpec((None, tb_, RW_W), lambda b, j: (b, tb(j), 0)),
            st_spec,
        ],
        out_shape=[
            jax.ShapeDtypeStruct((bsz, t, RW_W), f32),
            jax.ShapeDtypeStruct((bsz, t, RW_W), f32),
            jax.ShapeDtypeStruct((bsz, RW_HEADS, RW_N, RW_N), f32),
        ],
        scratch_shapes=[pltpu.VMEM((RW_HEADS, RW_N, RW_N), f32)]
        + [pltpu.VMEM((tb_, RW_W), f32) for _ in range(6)],
        compiler_params=pltpu.CompilerParams(
            dimension_semantics=("parallel", "arbitrary"), vmem_limit_bytes=VMEM_LIMIT),
        name="rwkv_bwd" if rev else "rwkv_fwd",
    )(p_sh, s0, wup, w0, aup, a0, k_k, k_a, r_k, ones_bd)


def _merge_kernel(x_ref, a_ref, bp_ref, gd_ref, ga_ref, gb_ref, g1_ref, gup_ref, pa_ref, pb_ref, wo_ref,
                  o_ref):
    g = _mm(_sigmoid(gd_ref[...]), gup_ref[...])
    bl = bp_ref[...] * g
    m = _sigmoid(ga_ref[...]) * _mm(a_ref[...], pa_ref[...]) + _sigmoid(gb_ref[...]) * _mm(bl, pb_ref[...])
    o_ref[...] = x_ref[...] + g1_ref[...] * _mm(m, wo_ref[...])


def _merge(x, a_lat, b_pre, p_sh, pfull, g1, gup_bf, pa_bf, pb_bf, wo_bf):
    bsz, t, _ = x.shape
    tm = min(t, 512)
    gd_blk = (3 * RW_W + 2 * RW_DECAY_LORA + 2 * RW_A_LORA) // RW_GATE_LORA
    full = lambda shape: pl.BlockSpec(shape, lambda b, i: (0,) * len(shape))
    return pl.pallas_call(
        _merge_kernel,
        grid=(bsz, t // tm),
        in_specs=[
            pl.BlockSpec((None, tm, D_MODEL), lambda b, i: (b, i, 0)),
            pl.BlockSpec((None, tm, ML_V), lambda b, i: (b, i, 0)),
            pl.BlockSpec((None, tm, RW_W), lambda b, i: (b, i, 0)),
            pl.BlockSpec((None, tm, RW_GATE_LORA), lambda b, i: (b, i, gd_blk)),
            pl.BlockSpec((None, tm, D_MODEL), lambda b, i: (b, i, P_GA // D_MODEL)),
            pl.BlockSpec((None, tm, D_MODEL), lambda b, i: (b, i, P_GB // D_MODEL)),
            pl.BlockSpec((None, 1, D_MODEL), lambda b, i: (b, 0, 0)),
            full((RW_GATE_LORA, RW_W)),
            full((ML_V, D_MODEL)),
            full((RW_W, D_MODEL)),
            full((D_MODEL, D_MODEL)),
        ],
        out_specs=pl.BlockSpec((None, tm, D_MODEL), lambda b, i: (b, i, 0)),
        out_shape=jax.ShapeDtypeStruct((bsz, t, D_MODEL), f32),
        compiler_params=pltpu.CompilerParams(
            dimension_semantics=("parallel", "parallel"), vmem_limit_bytes=VMEM_LIMIT),
        name="merge",
    )(x, a_lat, b_pre, p_sh, pfull, pfull, g1, gup_bf, pa_bf, pb_bf, wo_bf)


def _lane_max(x):
    return jnp.max(x, axis=-1, keepdims=True)


def _first_at(x, val, lane_f):
    return jnp.min(jnp.where(x == val, lane_f, float(N_EXPERTS)), axis=-1, keepdims=True)


def _route_kernel(x_ref, g_ref, sc_ref, sh_ref, rw_ref, rb_ref, comb_ref):
    h = _norm_mod(x_ref[...], g_ref[...], sc_ref[...], sh_ref[...])
    scores = _sigmoid(_mm3(h, rw_ref[...]))
    sel = scores + rb_ref[...]
    lane = lax.broadcasted_iota(jnp.int32, sel.shape, 1)
    lane_f = lane.astype(f32)
    grp = lane // EXPERTS_PER_GROUP
    neg = -jnp.inf
    best_g = jnp.zeros(sel.shape[:1] + (1,), jnp.int32)
    best_v = None
    for gi in range(N_GROUPS):
        mg = jnp.where(grp == gi, sel, neg)
        m1 = _lane_max(mg)
        i1 = _first_at(mg, m1, lane_f)
        m2 = _lane_max(jnp.where(lane_f == i1, neg, mg))
        gs = m1 + m2
        if gi == 0:
            best_v = gs
        else:
            upd = gs > best_v
            best_g = jnp.where(upd, gi, best_g)
            best_v = jnp.where(upd, gs, best_v)
    cand = jnp.where(grp == best_g, sel, neg)
    v1 = _lane_max(cand)
    i1 = _first_at(cand, v1, lane_f)
    cand2 = jnp.where(lane_f == i1, neg, cand)
    v2 = _lane_max(cand2)
    i2 = _first_at(cand2, v2, lane_f)
    picked = (lane_f == i1) | (lane_f == i2)
    w = jnp.where(picked, scores, 0.0)
    comb_ref[...] = w / jnp.sum(w, axis=-1, keepdims=True)


def _route(x, g, sc, sh, router_w, router_b):
    bsz, t, _ = x.shape
    tm = min(t, 512)
    return pl.pallas_call(
        _route_kernel,
        grid=(bsz, t // tm),
        in_specs=[
            pl.BlockSpec((None, tm, D_MODEL), lambda b, i: (b, i, 0)),
            pl.BlockSpec((1, D_MODEL), lambda b, i: (0, 0)),
            pl.BlockSpec((None, 1, D_MODEL), lambda b, i: (b, 0, 0)),
            pl.BlockSpec((None, 1, D_MODEL), lambda b, i: (b, 0, 0)),
            pl.BlockSpec((D_MODEL, N_EXPERTS), lambda b, i: (0, 0)),
            pl.BlockSpec((1, N_EXPERTS), lambda b, i: (0, 0)),
        ],
        out_specs=pl.BlockSpec((None, tm, N_EXPERTS), lambda b, i: (b, i, 0)),
        out_shape=jax.ShapeDtypeStruct((bsz, t, N_EXPERTS), f32),
        compiler_params=pltpu.CompilerParams(
            dimension_semantics=("parallel", "parallel"), vmem_limit_bytes=VMEM_LIMIT),
        name="route",
    )(x, g, sc, sh, router_w, router_b.reshape(1, N_EXPERTS))


def _moe_kernel(x_ref, g_ref, sc_ref, sh_ref, g2_ref, comb_ref, wg_ref, wu_ref, wd_ref, o_ref,
                h_scr, acc_scr):
    e = pl.program_id(2)

    @pl.when(e == 0)
    def _():
        h_scr[...] = _norm_mod(x_ref[...], g_ref[...], sc_ref[...], sh_ref[...]).astype(bf16)
        acc_scr[...] = jnp.zeros_like(acc_scr)

    h = h_scr[...]
    gate = _dot(h, wg_ref[...])
    hid = (gate * _sigmoid(gate)) * _dot(h, wu_ref[...])
    comb = comb_ref[...]
    lane = lax.broadcasted_iota(jnp.int32, comb.shape, 1)
    ce = jnp.sum(jnp.where(lane == e, comb, 0.0), axis=-1, keepdims=True)
    acc_scr[...] += _dot((ce * hid).astype(bf16), wd_ref[...])

    @pl.when(e == N_EXPERTS - 1)
    def _():
        o_ref[...] = x_ref[...] + g2_ref[...] * acc_scr[...]


def _moe(x, g, sc, sh, g2, comb, wg_bf, wu_bf, wd_bf):
    bsz, t, _ = x.shape
    tm = min(t, 1024)
    return pl.pallas_call(
        _moe_kernel,
        grid=(bsz, t // tm, N_EXPERTS),
        in_specs=[
            pl.BlockSpec((None, tm, D_MODEL), lambda b, i, e: (b, i, 0)),
            pl.BlockSpec((1, D_MODEL), lambda b, i, e: (0, 0)),
            pl.BlockSpec((None, 1, D_MODEL), lambda b, i, e: (b, 0, 0)),
            pl.BlockSpec((None, 1, D_MODEL), lambda b, i, e: (b, 0, 0)),
            pl.BlockSpec((None, 1, D_MODEL), lambda b, i, e: (b, 0, 0)),
            pl.BlockSpec((None, tm, N_EXPERTS), lambda b, i, e: (b, i, 0)),
            pl.BlockSpec((None, D_MODEL, D_EXPERT), lambda b, i, e: (e, 0, 0)),
            pl.BlockSpec((None, D_MODEL, D_EXPERT), lambda b, i, e: (e, 0, 0)),
            pl.BlockSpec((None, D_EXPERT, D_MODEL), lambda b, i, e: (e, 0, 0)),
        ],
        out_specs=pl.BlockSpec((None, tm, D_MODEL), lambda b, i, e: (b, i, 0)),
        out_shape=jax.ShapeDtypeStruct((bsz, t, D_MODEL), f32),
        scratch_shapes=[pltpu.VMEM((tm, D_MODEL), bf16), pltpu.VMEM((tm, D_MODEL), f32)],
        compiler_params=pltpu.CompilerParams(
            dimension_semantics=("parallel", "parallel", "arbitrary"), vmem_limit_bytes=VMEM_LIMIT),
        name="moe",
    )(x, g, sc, sh, g2, comb, wg_bf, wu_bf, wd_bf)


def _final_kernel(x_ref, g_ref, o_ref):
    x = x_ref[...]
    o_ref[...] = x * lax.rsqrt(jnp.mean(x * x, axis=-1, keepdims=True) + NORM_EPS) * g_ref[...]


def _final_norm(x, g):
    bsz, t, _ = x.shape
    tm = min(t, 1024)
    return pl.pallas_call(
        _final_kernel,
        grid=(bsz, t // tm),
        in_specs=[
            pl.BlockSpec((None, tm, D_MODEL), lambda b, i: (b, i, 0)),
            pl.BlockSpec((1, D_MODEL), lambda b, i: (0, 0)),
        ],
        out_specs=pl.BlockSpec((None, tm, D_MODEL), lambda b, i: (b, i, 0)),
        out_shape=jax.ShapeDtypeStruct((bsz, t, D_MODEL), f32),
        compiler_params=pltpu.CompilerParams(
            dimension_semantics=("parallel", "parallel"), vmem_limit_bytes=VMEM_LIMIT),
        name="final_norm",
    )(x, g.reshape(1, D_MODEL))


def _pack_w_in(w_in):
    ml, rw, gt = w_in[:, :ML_COLS], w_in[:, ML_COLS:ML_COLS + RW_COLS], w_in[:, ML_COLS + RW_COLS:]
    qkvo, mlg = ml[:, :2 * ML_QK + 2 * ML_V], ml[:, 2 * ML_QK + 2 * ML_V:]
    z = lambda n: jnp.zeros((D_MODEL, n), w_in.dtype)
    return jnp.concatenate(
        [gt, rw, z(P_RW_PAD - RW_COLS), qkvo, mlg, z(P_MLG_PAD - 4 * ML_HEADS)], axis=1).astype(bf16)


def _shift_prev(u):
    return jnp.pad(u[:, :-1], ((0, 0), (1, 0), (0, 0)))


def _shift_next(u):
    return jnp.pad(u[:, 1:], ((0, 0), (0, 1), (0, 0)))


def _grid_conv(u, taps, bias):
    bsz, t, ch = u.shape
    rows = t // GRID_W
    img = u.reshape(bsz, rows, GRID_W, ch)
    pad = jnp.pad(img, ((0, 0), (1, 1), (1, 1), (0, 0)))
    out = bias
    for dr in range(3):
        for dc in range(3):
            out = out + taps[dr, dc] * pad[:, dr:dr + rows, dc:dc + GRID_W]
    return out.reshape(bsz, t, ch)


def _line_conv(u, taps, bias):
    row = taps[1]
    return row[0] * _shift_prev(u) + row[1] * u + row[2] * _shift_next(u) + bias


def _ml_gates(mlg, gate_b):
    bsz, t, _ = mlg.shape
    pre = mlg.reshape(bsz, t, 2, 2, ML_HEADS) + gate_b
    pre = GATE_CAP * jnp.tanh(pre / GATE_CAP)
    out = []
    for d in range(2):
        gc = jnp.concatenate([pre[:, :, 0, d], jax.nn.log_sigmoid(pre[:, :, 1, d])], axis=-1)
        out.append((gc, jnp.transpose(gc, (0, 2, 1))))
    return out


def _mixer(x, pfull, is_ctx, lp, ml_state, rw_state):
    bsz, t, _ = x.shape
    qk = pfull[..., P_QK:P_QK + 2 * ML_QK]
    conv = _line_conv if is_ctx else _grid_conv
    qk = conv(qk, lp["conv_k"], lp["conv_b"])
    q, k = qk[..., :ML_QK], qk[..., ML_QK:] * (ML_DQK ** -0.5)
    gates = _ml_gates(pfull[..., P_MLG:P_MLG + 4 * ML_HEADS], lp["gate_b"])
    hs, ml_fin = [], []
    for d in range(2):
        h_d, c_fin = _mlstm(q, k, pfull, gates[d][0], gates[d][1], ml_state[d], rev=bool(d))
        hs.append(h_d)
        ml_fin.append(c_fin)
    hsum = (hs[0] + hs[1]).reshape(bsz, t, ML_HEADS, ML_DV)
    hn = hsum * lax.rsqrt(jnp.mean(hsum * hsum, axis=-1, keepdims=True) + NORM_EPS)
    o = pfull[..., P_O:P_O + ML_V]
    a_lat = hn.reshape(bsz, t, ML_V) * lp["ml_norm_g"] * jax.nn.sigmoid(o)

    prw = pfull[..., P_RW:P_RW + RW_COLS]
    p_sh = prw + lp["rw_mu"] * (0.5 * (_shift_prev(prw) + _shift_next(prw)) - prw)
    ys, bvs, rw_fin = [], [], []
    for d in range(2):
        y_d, bv_d, s_fin = _rwkv(p_sh, rw_state[d], lp["rw_w_up"][d], lp["rw_w0"][d:d + 1], lp["rw_a_up"][d],
                                 lp["rw_a0"][d:d + 1], lp["rw_k_k"], lp["rw_k_a"], lp["rw_r_k"], lp["ones_bd"],
                                 rev=bool(d), d=d)
        ys.append(y_d)
        bvs.append(bv_d)
        rw_fin.append(s_fin)
    y = (ys[0] + ys[1]).reshape(bsz, t, RW_HEADS, RW_N)
    mu = jnp.mean(y, axis=-1, keepdims=True)
    var = jnp.mean(jnp.square(y - mu), axis=-1, keepdims=True)
    yn = ((y - mu) * lax.rsqrt(var + RW_GN_EPS)).reshape(bsz, t, RW_W) * lp["rw_ln_w"] + lp["rw_ln_b"]
    b_pre = yn + bvs[0] + bvs[1]
    return a_lat, b_pre, p_sh, ml_fin, rw_fin


def kernel(x, c, ctx, c_ctx, w_ada, b_ada, norm1_g, norm2_g, w_in, ml_conv_k, ml_conv_b, ml_gate_b, ml_norm_g, rw_mu, rw_w_up, rw_w0, rw_a_up, rw_a0, rw_g_up, rw_k_k, rw_k_a, rw_r_k, rw_ln_w, rw_ln_b, merge_pa, merge_pb, w_out, router_w, router_b, exp_w_gate, exp_w_up, exp_w_down, final_g):
    bsz = x.shape[0]
    s_rows = jnp.zeros((8, D_MODEL), f32).at[:bsz].set(c).at[bsz].set(c_ctx)
    mod = _ada(s_rows, w_ada, b_ada)
    head_id = jnp.arange(RW_W) // RW_N
    ones_bd = (head_id[:, None] == head_id[None, :]).astype(bf16)

    x_lat, x_ctx = x, ctx
    for l in range(DEPTH):
        last = l == DEPTH - 1
        mod_lat = mod[l, :bsz].reshape(bsz, 1, N_MOD, D_MODEL)
        mod_ctx = jnp.broadcast_to(mod[l, bsz].reshape(1, 1, N_MOD, D_MODEL), (bsz, 1, N_MOD, D_MODEL))
        lp = dict(conv_k=ml_conv_k[l], conv_b=ml_conv_b[l], gate_b=ml_gate_b[l], ml_norm_g=ml_norm_g[l],
                  rw_mu=rw_mu[l], rw_w_up=rw_w_up[l].astype(bf16), rw_w0=rw_w0[l],
                  rw_a_up=rw_a_up[l].astype(bf16), rw_a0=rw_a0[l], rw_k_k=rw_k_k[l].reshape(1, RW_W),
                  rw_k_a=rw_k_a[l].reshape(1, RW_W), rw_r_k=rw_r_k[l].reshape(1, RW_W),
                  rw_ln_w=rw_ln_w[l], rw_ln_b=rw_ln_b[l], ones_bd=ones_bd)
        w_in_bf = _pack_w_in(w_in[l])
        gup_bf, pa_bf, pb_bf, wo_bf = (rw_g_up[l].astype(bf16), merge_pa[l].astype(bf16),
                                       merge_pb[l].astype(bf16), w_out[l].astype(bf16))
        wg_bf, wu_bf, wd_bf = exp_w_gate[l].astype(bf16), exp_w_up[l].astype(bf16), exp_w_down[l].astype(bf16)
        g1n = norm1_g[l].reshape(1, D_MODEL)
        g2n = norm2_g[l].reshape(1, D_MODEL)

        def m(modv, i):
            return modv[:, :, i]

        p_ctx = _proj(x_ctx, g1n, m(mod_ctx, 1), m(mod_ctx, 0), w_in_bf)
        ml0 = [jnp.zeros((bsz, ML_HEADS, ML_DQK, 2 * ML_DV), f32)] * 2
        rw0 = [jnp.zeros((bsz, RW_HEADS, RW_N, RW_N), f32)] * 2
        a_c, b_c, psh_c, ml_st, rw_st = _mixer(x_ctx, p_ctx, True, lp, ml0, rw0)

        p_lat = _proj(x_lat, g1n, m(mod_lat, 1), m(mod_lat, 0), w_in_bf)
        a_l, b_l, psh_l, _, _ = _mixer(x_lat, p_lat, False, lp, ml_st, rw_st)
        x_lat = _merge(x_lat, a_l, b_l, psh_l, p_lat, m(mod_lat, 2), gup_bf, pa_bf, pb_bf, wo_bf)
        comb = _route(x_lat, g2n, m(mod_lat, 4), m(mod_lat, 3), router_w, router_b)
        x_lat = _moe(x_lat, g2n, m(mod_lat, 4), m(mod_lat, 3), m(mod_lat, 5), comb, wg_bf, wu_bf, wd_bf)
        if not last:
            x_ctx = _merge(x_ctx, a_c, b_c, psh_c, p_ctx, m(mod_ctx, 2), gup_bf, pa_bf, pb_bf, wo_bf)
            comb_c = _route(x_ctx, g2n, m(mod_ctx, 4), m(mod_ctx, 3), router_w, router_b)
            x_ctx = _moe(x_ctx, g2n, m(mod_ctx, 4), m(mod_ctx, 3), m(mod_ctx, 5), comb_c, wg_bf, wu_bf, wd_bf)
    return _final_norm(x_lat, final_g)
```

```python
import functools

import jax
import jax.numpy as jnp
import numpy as np
from jax import lax
from jax.experimental import pallas as pl
from jax.experimental.pallas import tpu as pltpu

f32 = jnp.float32
bf16 = jnp.bfloat16

D_MODEL = 1024
DEPTH = 2
GRID_W = 64
N_MOD = 6
NORM_EPS = 1e-6

ML_HEADS = 4
ML_DQK = 64
ML_DV = 128
ML_QK = ML_HEADS * ML_DQK
ML_V = ML_HEADS * ML_DV
GATE_CAP = 15.0
ML_COLS = 2 * ML_QK + 2 * ML_V + 4 * ML_HEADS
ML_CHUNK = 256

RW_HEADS = 8
RW_N = 64
RW_W = RW_HEADS * RW_N
RW_DECAY_LORA = 64
RW_A_LORA = 64
RW_GATE_LORA = 128
RW_GN_EPS = 6.4e-4
RW_COLS = 3 * RW_W + 2 * RW_DECAY_LORA + 2 * RW_A_LORA + RW_GATE_LORA
RW_CHUNK = 64
RW_BLOCK = 256

N_EXPERTS = 16
N_GROUPS = 4
EXPERTS_PER_GROUP = N_EXPERTS // N_GROUPS
D_EXPERT = 512

P_GA = 0
P_GB = D_MODEL
P_RW = 2 * D_MODEL
P_RW_PAD = 2048
P_QK = P_RW + P_RW_PAD
P_V = P_QK + 2 * ML_QK
P_O = P_V + ML_V
P_MLG = P_O + ML_V
P_MLG_PAD = 256
P_COLS = P_MLG + P_MLG_PAD
PROJ_TN = 256

VMEM_LIMIT = 48 * 1024 * 1024
EXP_NEG_HALF = float(np.exp(-0.5))
NEG_BIG = -1e30


_NN = ((1,), (0,))
_NT = ((1,), (1,))
_TN = ((0,), (0,))


def _dot(a, b, dims=_NN):
    return lax.dot_general(a, b, (dims, ((), ())), preferred_element_type=f32)


def _mm(a, b, dims=_NN):
    return _dot(a.astype(bf16), b.astype(bf16), dims)


def _hi_lo(x):
    hi = x.astype(bf16)
    lo = (x - hi.astype(f32)).astype(bf16)
    return hi, lo


def _mm3(a, b, dims=_NN):
    ah, al = _hi_lo(a)
    bh, bl = _hi_lo(b)
    return _dot(ah, bh, dims) + (_dot(ah, bl, dims) + _dot(al, bh, dims))


def _mm_exact_lhs(a_bf, b, dims=_NN):
    hi = b.astype(bf16)
    r1 = b - hi.astype(f32)
    mid = r1.astype(bf16)
    lo = (r1 - mid.astype(f32)).astype(bf16)
    return _dot(a_bf, hi, dims) + (_dot(a_bf, mid, dims) + _dot(a_bf, lo, dims))


def _mm_exact_rhs(a, b_bf, dims=_NN):
    hi = a.astype(bf16)
    r1 = a - hi.astype(f32)
    mid = r1.astype(bf16)
    lo = (r1 - mid.astype(f32)).astype(bf16)
    return _dot(hi, b_bf, dims) + (_dot(mid, b_bf, dims) + _dot(lo, b_bf, dims))


def _sigmoid(x):
    return 1.0 / (1.0 + jnp.exp(-x))


def _norm_mod(x, g, sc, sh):
    y = x * lax.rsqrt(jnp.mean(x * x, axis=-1, keepdims=True) + NORM_EPS)
    return (y * g) * (1.0 + sc) + sh


def _ada_kernel(s_ref, w_ref, b_ref, o_ref):
    s = s_ref[...]
    s = s * _sigmoid(s)
    o_ref[...] = _mm3(s, w_ref[...]) + b_ref[...]


def _ada(s_rows, w_ada, b_ada):
    tn = 1536
    n = N_MOD * D_MODEL
    return pl.pallas_call(
        _ada_kernel,
        grid=(DEPTH, n // tn),
        in_specs=[
            pl.BlockSpec((8, D_MODEL), lambda l, j: (0, 0)),
            pl.BlockSpec((None, D_MODEL, tn), lambda l, j: (l, 0, j)),
            pl.BlockSpec((None, 1, tn), lambda l, j: (l, 0, j)),
        ],
        out_specs=pl.BlockSpec((None, 8, tn), lambda l, j: (l, 0, j)),
        out_shape=jax.ShapeDtypeStruct((DEPTH, 8, n), f32),
        compiler_params=pltpu.CompilerParams(
            dimension_semantics=("arbitrary", "arbitrary"), vmem_limit_bytes=VMEM_LIMIT),
        name="ada",
    )(s_rows, w_ada, b_ada.reshape(DEPTH, 1, n))


def _proj_kernel(x_ref, g_ref, sc_ref, sh_ref, w_ref, o_ref, h_scr):
    @pl.when(pl.program_id(2) == 0)
    def _():
        h_scr[...] = _norm_mod(x_ref[...], g_ref[...], sc_ref[...], sh_ref[...]).astype(bf16)

    o_ref[...] = _dot(h_scr[...], w_ref[...])


def _proj(x, g, sc, sh, w_bf):
    bsz, t, _ = x.shape
    tm = min(t, 1024)
    return pl.pallas_call(
        _proj_kernel,
        grid=(bsz, t // tm, P_COLS // PROJ_TN),
        in_specs=[
            pl.BlockSpec((None, tm, D_MODEL), lambda b, i, j: (b, i, 0)),
            pl.BlockSpec((1, D_MODEL), lambda b, i, j: (0, 0)),
            pl.BlockSpec((None, 1, D_MODEL), lambda b, i, j: (b, 0, 0)),
            pl.BlockSpec((None, 1, D_MODEL), lambda b, i, j: (b, 0, 0)),
            pl.BlockSpec((D_MODEL, PROJ_TN), lambda b, i, j: (0, j)),
        ],
        out_specs=pl.BlockSpec((None, tm, PROJ_TN), lambda b, i, j: (b, i, j)),
        out_shape=jax.ShapeDtypeStruct((bsz, t, P_COLS), f32),
        scratch_shapes=[pltpu.VMEM((tm, D_MODEL), bf16)],
        compiler_params=pltpu.CompilerParams(
            dimension_semantics=("parallel", "parallel", "arbitrary"), vmem_limit_bytes=VMEM_LIMIT),
        name="proj",
    )(x, g, sc, sh, w_bf)


def _shift_rows(u, up_row, dn_row):
    n = u.shape[0]
    rid = lax.broadcasted_iota(jnp.int32, (n, 1), 0)
    up = jnp.where(rid == 0, up_row, pltpu.roll(u, 1, axis=0))
    dn = jnp.where(rid == n - 1, dn_row, pltpu.roll(u, n - 1, axis=0))
    return up, dn


def _log_sigmoid(x):
    return jnp.minimum(x, 0.0) - jnp.log1p(jnp.exp(-jnp.abs(x)))


def _mlstm_kernel(qk_ref, qkp_ref, qkn_ref, v_ref, mlg_ref, taps_ref, cb_ref, gb_ref, c0_ref,
                  h_ref, cfin_ref, c_scr, *, rev, d, grid_conv, nb):
    L = ML_CHUNK
    j = pl.program_id(1)
    jblk = (nb - 1 - j) if rev else j

    @pl.when(j == 0)
    def _():
        c_scr[...] = c0_ref[...]

    qk = qk_ref[...]
    taps = taps_ref[...]
    zero_row = jnp.zeros((1, 2 * ML_QK), f32)
    rid = lax.broadcasted_iota(jnp.int32, (L, 1), 0)
    if grid_conv:
        first_col = (rid % GRID_W) == 0
        last_col = (rid % GRID_W) == GRID_W - 1
        above = jnp.where(jblk > 0, qkp_ref[...], 0.0)
        below = jnp.where(jblk < nb - 1, qkn_ref[...], 0.0)
        ext = jnp.concatenate([above, qk, below], axis=0)
        bases = [(dr, ext[dr * GRID_W:dr * GRID_W + L]) for dr in range(3)]
    else:
        first_col = rid == 0
        last_col = rid == L - 1
        bases = [(1, qk)]
    conv = cb_ref[...]
    for dr, base in bases:
        up, dn = _shift_rows(base, zero_row, zero_row)
        conv = conv + (taps[3 * dr:3 * dr + 1] * jnp.where(first_col, 0.0, up)
                       + taps[3 * dr + 1:3 * dr + 2] * base
                       + taps[3 * dr + 2:3 * dr + 3] * jnp.where(last_col, 0.0, dn))
    q = conv[:, :ML_QK]
    k = conv[:, ML_QK:] * (ML_DQK ** -0.5)

    pre = mlg_ref[...][:, :4 * ML_HEADS] + gb_ref[...]
    pre = GATE_CAP * jnp.tanh(pre * (1.0 / GATE_CAP))
    ig = pre[:, d * ML_HEADS:(d + 1) * ML_HEADS]
    lf = _log_sigmoid(pre[:, (2 + d) * ML_HEADS:(3 + d) * ML_HEADS])
    ii = lax.broadcasted_iota(jnp.int32, (L, L), 0)
    jj = lax.broadcasted_iota(jnp.int32, (L, L), 1)
    incl = (jj >= ii) if rev else (jj <= ii)
    tri = incl.astype(bf16)
    bc = _mm_exact_lhs(tri, lf)
    cols = jnp.concatenate([ig, bc], axis=1)
    e_i = lax.broadcasted_iota(jnp.int32, (2 * ML_HEADS, 2 * ML_HEADS), 0)
    e_j = lax.broadcasted_iota(jnp.int32, (2 * ML_HEADS, 2 * ML_HEADS), 1)
    rows_ = _mm_exact_lhs((e_i == e_j).astype(bf16), cols, _NT)
    one_col = (lax.broadcasted_iota(jnp.int32, (L, ML_DV), 1) == 0).astype(f32)
    v = v_ref[...]

    hs = range(ML_HEADS)
    qh = [q[:, h * ML_DQK:(h + 1) * ML_DQK] for h in hs]
    kh = [k[:, h * ML_DQK:(h + 1) * ML_DQK] for h in hs]
    vp = [jnp.concatenate([v[:, h * ML_DV:(h + 1) * ML_DV], one_col], axis=1) for h in hs]
    b_col = [bc[:, h:h + 1] for h in hs]
    btot = [(b_col[h][0:1, :] if rev else b_col[h][L - 1:L, :]) for h in hs]
    decay = [jnp.exp(jnp.where(incl, (b_col[h] - rows_[ML_HEADS + h:ML_HEADS + h + 1, :]) + rows_[h:h + 1, :],
                               NEG_BIG)) for h in hs]
    qk_s = [_mm(qh[h], kh[h], _NT) for h in hs]
    qc = [_mm(qh[h], c_scr[h]) for h in hs]
    kw = [kh[h] * jnp.exp((btot[h] - b_col[h]) + ig[:, h:h + 1]) for h in hs]
    kv = [_mm(kw[h], vp[h], _TN) for h in hs]
    sv = [_mm(qk_s[h] * decay[h], vp[h]) for h in hs]
    for h in hs:
        nd = sv[h] + jnp.exp(b_col[h]) * qc[h]
        den = nd[:, ML_DV:ML_DV + 1]
        h_ref[:, h * ML_DV:(h + 1) * ML_DV] = nd[:, :ML_DV] / jnp.maximum(jnp.abs(den), 1.0)
        c_scr[h] = jnp.exp(btot[h]) * c_scr[h] + kv[h]
    cfin_ref[...] = c_scr[...]


def _mlstm(pfull, taps, conv_b, gate_b, c0, rev, d, grid_conv):
    bsz, t, _ = pfull.shape
    L = ML_CHUNK
    nb = t // L
    tb = (lambda j: nb - 1 - j) if rev else (lambda j: j)
    vblk = P_V // ML_V
    qkblk = P_QK // (2 * ML_QK)
    rpb = L // GRID_W
    n_rows = t // GRID_W
    return pl.pallas_call(
        functools.partial(_mlstm_kernel, rev=rev, d=d, grid_conv=grid_conv, nb=nb),
        grid=(bsz, nb),
        in_specs=[
            pl.BlockSpec((None, L, 2 * ML_QK), lambda b, j: (b, tb(j), qkblk)),
            pl.BlockSpec((None, GRID_W, 2 * ML_QK),
                         lambda b, j: (b, jnp.maximum(tb(j) * rpb - 1, 0), qkblk)),
            pl.BlockSpec((None, GRID_W, 2 * ML_QK),
                         lambda b, j: (b, jnp.minimum((tb(j) + 1) * rpb, n_rows - 1), qkblk)),
            pl.BlockSpec((None, L, ML_V), lambda b, j: (b, tb(j), vblk)),
            pl.BlockSpec((None, L, P_MLG_PAD), lambda b, j: (b, tb(j), P_MLG // P_MLG_PAD)),
            pl.BlockSpec((9, 2 * ML_QK), lambda b, j: (0, 0)),
            pl.BlockSpec((1, 2 * ML_QK), lambda b, j: (0, 0)),
            pl.BlockSpec((1, 4 * ML_HEADS), lambda b, j: (0, 0)),
            pl.BlockSpec((None, ML_HEADS, ML_DQK, 2 * ML_DV), lambda b, j: (b, 0, 0, 0)),
        ],
        out_specs=[
            pl.BlockSpec((None, L, ML_V), lambda b, j: (b, tb(j), 0)),
            pl.BlockSpec((None, ML_HEADS, ML_DQK, 2 * ML_DV), lambda b, j: (b, 0, 0, 0)),
        ],
        out_shape=[
            jax.ShapeDtypeStruct((bsz, t, ML_V), f32),
            jax.ShapeDtypeStruct((bsz, ML_HEADS, ML_DQK, 2 * ML_DV), f32),
        ],
        scratch_shapes=[pltpu.VMEM((ML_HEADS, ML_DQK, 2 * ML_DV), f32)],
        compiler_params=pltpu.CompilerParams(
            dimension_semantics=("parallel", "arbitrary"), vmem_limit_bytes=VMEM_LIMIT),
        name="mlstm_bwd" if rev else "mlstm_fwd",
    )(pfull, pfull, pfull, pfull, pfull, taps, conv_b, gate_b, c0)


def _rw_prepare(p_ref, pp_ref, pn_ref, jblk, nb, d, par, scr, bv_ref):
    mu_ref, wup_ref, w0_ref, aup_ref, a0_ref, kk_ref, ka_ref, rk_ref, ones_ref = par
    r_scr, v_scr, kk_scr, kd_scr, be_scr, ld_scr = scr
    p = p_ref[...]
    up, dn = _shift_rows(p, jnp.where(jblk > 0, pp_ref[7:8, :], 0.0),
                         jnp.where(jblk < nb - 1, pn_ref[0:1, :], 0.0))
    p = p + mu_ref[...] * (0.5 * (up + dn) - p)
    r = p[:, 0:RW_W]
    k = p[:, RW_W:2 * RW_W]
    v = p[:, 2 * RW_W:3 * RW_W]
    o_wd = 3 * RW_W + d * RW_DECAY_LORA
    o_ad = 3 * RW_W + 2 * RW_DECAY_LORA + d * RW_A_LORA
    lw = w0_ref[d:d + 1, :] + _mm(jnp.tanh(p[:, o_wd:o_wd + RW_DECAY_LORA]), wup_ref[d])
    ld_scr[...] = -EXP_NEG_HALF * _sigmoid(lw)
    a = _sigmoid(a0_ref[d:d + 1, :] + _mm(p[:, o_ad:o_ad + RW_A_LORA], aup_ref[d]))
    ones_bd = ones_ref[...]
    kkr = k * kk_ref[...]
    kk = kkr * lax.rsqrt(_mm_exact_rhs(kkr * kkr, ones_bd) + 1e-12)
    kd = k * (1.0 + (a - 1.0) * ka_ref[...])
    bv_ref[...] = _mm_exact_rhs(r * kd * rk_ref[...], ones_bd) * v
    r_scr[...] = r
    v_scr[...] = v
    kk_scr[...] = kk
    kd_scr[...] = kd
    be_scr[...] = kk * a
    return p


def _rwkv_kernel(pf_ref, pfp_ref, pfn_ref, pb_ref, pbp_ref, pbn_ref, s0f_ref, s0b_ref,
                 mu_ref, wup_ref, w0_ref, aup_ref, a0_ref, kk_ref, ka_ref, rk_ref, ones_ref, gup_ref,
                 yf_ref, bvf_ref, g_ref, yb_ref, bvb_ref, sff_ref, sfb_ref,
                 stf_scr, stb_scr, *scr, nb):
    L = RW_CHUNK
    nch = RW_BLOCK // L
    j = pl.program_id(1)

    @pl.when(j == 0)
    def _():
        stf_scr[...] = s0f_ref[...]
        stb_scr[...] = s0b_ref[...]

    par = (mu_ref, wup_ref, w0_ref, aup_ref, a0_ref, kk_ref, ka_ref, rk_ref, ones_ref)
    scr_f, scr_b = scr[:6], scr[6:]
    p_f = _rw_prepare(pf_ref, pfp_ref, pfn_ref, j, nb, 0, par, scr_f, bvf_ref)
    _rw_prepare(pb_ref, pbp_ref, pbn_ref, nb - 1 - j, nb, 1, par, scr_b, bvb_ref)
    o_gd = 3 * RW_W + 2 * RW_DECAY_LORA + 2 * RW_A_LORA
    g_ref[...] = _mm(_sigmoid(p_f[:, o_gd:o_gd + RW_GATE_LORA]), gup_ref[...])

    ii = lax.broadcasted_iota(jnp.int32, (L, L), 0)
    jj = lax.broadcasted_iota(jnp.int32, (L, L), 1)
    eye = (ii == jj).astype(f32)
    masks = []
    for rev in (False, True):
        incl = (jj >= ii) if rev else (jj <= ii)
        strict = (jj > ii) if rev else (jj < ii)
        masks.append((incl, strict, incl.astype(bf16)))
    streams = ((scr_f, stf_scr, yf_ref, False), (scr_b, stb_scr, yb_ref, True))

    def chunk(i, carry):
        units = []
        rows_of = []
        for si, (sc, st_scr, y_ref, rev) in enumerate(streams):
            r_scr, v_scr, kk_scr, kd_scr, be_scr, ld_scr = sc
            incl, strict, tri = masks[si]
            c = (nch - 1 - i) if rev else i
            rows = pl.ds(pl.multiple_of(c * L, L), L)
            rows_of.append(rows)
            ld = ld_scr[rows, :]
            b = _mm_exact_lhs(tri, ld)
            btot = b[0:1, :] if rev else b[L - 1:L, :]
            e_nb = jnp.exp(-b)
            e_end = jnp.exp(btot - b)
            kd_c = kd_scr[rows, :]
            be_c = be_scr[rows, :]
            v_c = v_scr[rows, :]
            aw = -kk_scr[rows, :] * jnp.exp(b - ld)
            rw = r_scr[rows, :] * jnp.exp(b)
            bi = be_c * e_nb
            ki = kd_c * e_nb
            bw = be_c * e_end
            kw = kd_c * e_end
            wend = jnp.exp(btot)
            for h in range(RW_HEADS):
                c_ = slice(h * RW_N, (h + 1) * RW_N)
                units.append(dict(si=si, h=h, incl=incl, strict=strict, st=st_scr, aw=aw[:, c_], rw=rw[:, c_],
                                  bi=bi[:, c_], ki=ki[:, c_], bw=bw[:, c_], kw=kw[:, c_], v=v_c[:, c_],
                                  wend=wend[:, c_]))
        us = range(len(units))
        m = [_mm(jnp.concatenate([u["aw"], u["rw"]], axis=0),
                 jnp.concatenate([u["bi"], u["ki"]], axis=0), _NT) for u in units]
        pw = [jnp.where(units[n]["strict"], m[n][:L, :L], 0.0) for n in us]
        mak = [jnp.where(units[n]["strict"], m[n][:L, L:], 0.0) for n in us]
        mrb = [jnp.where(units[n]["incl"], m[n][L:, :L], 0.0) for n in us]
        mrk = [jnp.where(units[n]["incl"], m[n][L:, L:], 0.0) for n in us]
        mv = [_mm(mak[n], units[n]["v"]) for n in us]
        x = [jnp.concatenate([units[n]["aw"], mv[n]], axis=1) for n in us]
        for it in range(6):
            px = [_mm(pw[n], x[n]) for n in us]
            if it < 5:
                pw = [_mm(pw[n], pw[n]) for n in us]
            x = [x[n] + px[n] for n in us]
        ray = [_mm(mrb[n], x[n]) for n in us]
        mkv = [_mm(mrk[n], units[n]["v"]) for n in us]
        gh = [_mm(units[n]["bw"], x[n], _TN) for n in us]
        kv = [_mm(units[n]["kw"], units[n]["v"], _TN) for n in us]
        rag = [jnp.concatenate([units[n]["rw"] + ray[n][:, :RW_N],
                                eye * units[n]["wend"] + gh[n][:, :RW_N]], axis=0) for n in us]
        yg = [_mm3(rag[n], units[n]["st"][units[n]["h"]]) for n in us]
        for si, (sc, st_scr, y_ref, rev) in enumerate(streams):
            mine = [n for n in us if units[n]["si"] == si]
            y_ref[rows_of[si], :] = jnp.concatenate(
                [yg[n][:L] + (ray[n][:, RW_N:] + mkv[n]) for n in mine], axis=1)
            for n in mine:
                st_scr[units[n]["h"]] = yg[n][L:] + (gh[n][:, RW_N:] + kv[n])
        return carry

    lax.fori_loop(0, nch, chunk, 0)
    sff_ref[...] = stf_scr[...]
    sfb_ref[...] = stb_scr[...]


def _rwkv(pfull, s0f, s0b, mu, wup, w0, aup, a0, k_k, k_a, r_k, ones_bd, gup):
    bsz, t, _ = pfull.shape
    tb_ = RW_BLOCK
    nb = t // tb_
    cblk = P_RW // P_RW_PAD
    hpb = tb_ // 8
    nh = t // 8
    fwd = lambda j: j
    bwd = lambda j: nb - 1 - j

    def p_specs(tb):
        return [
            pl.BlockSpec((None, tb_, P_RW_PAD), lambda b, j: (b, tb(j), cblk)),
            pl.BlockSpec((None, 8, P_RW_PAD), lambda b, j: (b, jnp.maximum(tb(j) * hpb - 1, 0), cblk)),
            pl.BlockSpec((None, 8, P_RW_PAD), lambda b, j: (b, jnp.minimum((tb(j) + 1) * hpb, nh - 1), cblk)),
        ]

    full = lambda shape: pl.BlockSpec(shape, lambda b, j: (0,) * len(shape))
    st_spec = pl.BlockSpec((None, RW_HEADS, RW_N, RW_N), lambda b, j: (b, 0, 0, 0))
    tok = lambda tb: pl.BlockSpec((None, tb_, RW_W), lambda b, j: (b, tb(j), 0))
    tok_shape = jax.ShapeDtypeStruct((bsz, t, RW_W), f32)
    st_shape = jax.ShapeDtypeStruct((bsz, RW_HEADS, RW_N, RW_N), f32)
    return pl.pallas_call(
        functools.partial(_rwkv_kernel, nb=nb),
        grid=(bsz, nb),
        in_specs=p_specs(fwd) + p_specs(bwd) + [
            st_spec, st_spec,
            full((1, P_RW_PAD)),
            full((2, RW_DECAY_LORA, RW_W)), full((2, RW_W)),
            full((2, RW_A_LORA, RW_W)), full((2, RW_W)),
            full((1, RW_W)), full((1, RW_W)), full((1, RW_W)),
            full((RW_W, RW_W)),
            full((RW_GATE_LORA, RW_W)),
        ],
        out_specs=[tok(fwd), tok(fwd), tok(fwd), tok(bwd), tok(bwd), st_spec, st_spec],
        out_shape=[tok_shape] * 5 + [st_shape] * 2,
        scratch_shapes=[pltpu.VMEM((RW_HEADS, RW_N, RW_N), f32) for _ in range(2)]
        + [pltpu.VMEM((tb_, RW_W), f32) for _ in range(12)],
        compiler_params=pltpu.CompilerParams(
            dimension_semantics=("parallel", "arbitrary"), vmem_limit_bytes=VMEM_LIMIT),
        name="rwkv",
    )(pfull, pfull, pfull, pfull, pfull, pfull, s0f, s0b, mu, wup, w0, aup, a0, k_k, k_a, r_k, ones_bd, gup)


def _merge_kernel(x_ref, hf_ref, hb_ref, o_ref_, yf_ref, yb_ref, bvf_ref, bvb_ref, g_ref, ga_ref, gb_ref,
                  g1_ref, mlg_ref, lnw_ref, lnb_ref, ones_ref, pa_ref, pb_ref, wo_ref, out_ref):
    hsum = hf_ref[...] + hb_ref[...]
    parts = []
    for h in range(ML_HEADS):
        hh = hsum[:, h * ML_DV:(h + 1) * ML_DV]
        parts.append(hh * lax.rsqrt(jnp.mean(hh * hh, axis=-1, keepdims=True) + NORM_EPS))
    a_lat = jnp.concatenate(parts, axis=1) * mlg_ref[...] * _sigmoid(o_ref_[...])
    y = yf_ref[...] + yb_ref[...]
    ones_bd = ones_ref[...]
    inv_n = 1.0 / RW_N
    mu = _mm_exact_rhs(y, ones_bd) * inv_n
    dev = y - mu
    var = _mm_exact_rhs(dev * dev, ones_bd) * inv_n
    yn = dev * lax.rsqrt(var + RW_GN_EPS) * lnw_ref[...] + lnb_ref[...]
    b_lat = (yn + (bvf_ref[...] + bvb_ref[...])) * g_ref[...]
    m = _sigmoid(ga_ref[...]) * _mm(a_lat, pa_ref[...]) + _sigmoid(gb_ref[...]) * _mm(b_lat, pb_ref[...])
    out_ref[...] = x_ref[...] + g1_ref[...] * _mm(m, wo_ref[...])


def _merge(x, hf, hb, yf, yb, bvf, bvb, g, pfull, g1, ml_norm_g, ln_w, ln_b, ones_bd, pa_bf, pb_bf, wo_bf):
    bsz, t, _ = x.shape
    tm = min(t, 512)
    full = lambda shape: pl.BlockSpec(shape, lambda b, i: (0,) * len(shape))
    tok = lambda w: pl.BlockSpec((None, tm, w), lambda b, i: (b, i, 0))
    col = lambda w, off: pl.BlockSpec((None, tm, w), lambda b, i: (b, i, off // w))
    return pl.pallas_call(
        _merge_kernel,
        grid=(bsz, t // tm),
        in_specs=[
            tok(D_MODEL), tok(ML_V), tok(ML_V), col(ML_V, P_O),
            tok(RW_W), tok(RW_W), tok(RW_W), tok(RW_W), tok(RW_W),
            col(D_MODEL, P_GA), col(D_MODEL, P_GB),
            pl.BlockSpec((None, 1, D_MODEL), lambda b, i: (b, 0, 0)),
            full((1, ML_V)), full((1, RW_W)), full((1, RW_W)), full((RW_W, RW_W)),
            full((ML_V, D_MODEL)), full((RW_W, D_MODEL)), full((D_MODEL, D_MODEL)),
        ],
        out_specs=tok(D_MODEL),
        out_shape=jax.ShapeDtypeStruct((bsz, t, D_MODEL), f32),
        compiler_params=pltpu.CompilerParams(
            dimension_semantics=("parallel", "parallel"), vmem_limit_bytes=VMEM_LIMIT),
        name="merge",
    )(x, hf, hb, pfull, yf, yb, bvf, bvb, g, pfull, pfull, g1, ml_norm_g, ln_w, ln_b, ones_bd,
      pa_bf, pb_bf, wo_bf)


def _lane_max(x):
    return jnp.max(x, axis=-1, keepdims=True)


def _first_at(x, val, lane_f):
    return jnp.min(jnp.where(x == val, lane_f, float(N_EXPERTS)), axis=-1, keepdims=True)


def _route_kernel(x_ref, g_ref, sc_ref, sh_ref, rw_ref, rb_ref, comb_ref):
    h = _norm_mod(x_ref[...], g_ref[...], sc_ref[...], sh_ref[...])
    scores = _sigmoid(_mm3(h, rw_ref[...]))
    sel = scores + rb_ref[...]
    lane = lax.broadcasted_iota(jnp.int32, sel.shape, 1)
    lane_f = lane.astype(f32)
    grp = lane // EXPERTS_PER_GROUP
    neg = -jnp.inf
    best_g = jnp.zeros(sel.shape[:1] + (1,), jnp.int32)
    best_v = None
    for gi in range(N_GROUPS):
        mg = jnp.where(grp == gi, sel, neg)
        m1 = _lane_max(mg)
        i1 = _first_at(mg, m1, lane_f)
        m2 = _lane_max(jnp.where(lane_f == i1, neg, mg))
        gs = m1 + m2
        if gi == 0:
            best_v = gs
        else:
            upd = gs > best_v
            best_g = jnp.where(upd, gi, best_g)
            best_v = jnp.where(upd, gs, best_v)
    cand = jnp.where(grp == best_g, sel, neg)
    v1 = _lane_max(cand)
    i1 = _first_at(cand, v1, lane_f)
    cand2 = jnp.where(lane_f == i1, neg, cand)
    v2 = _lane_max(cand2)
    i2 = _first_at(cand2, v2, lane_f)
    picked = (lane_f == i1) | (lane_f == i2)
    w = jnp.where(picked, scores, 0.0)
    comb_ref[...] = w / jnp.sum(w, axis=-1, keepdims=True)


def _route(x, g, sc, sh, router_w, router_b):
    bsz, t, _ = x.shape
    tm = min(t, 512)
    return pl.pallas_call(
        _route_kernel,
        grid=(bsz, t // tm),
        in_specs=[
            pl.BlockSpec((None, tm, D_MODEL), lambda b, i: (b, i, 0)),
            pl.BlockSpec((1, D_MODEL), lambda b, i: (0, 0)),
            pl.BlockSpec((None, 1, D_MODEL), lambda b, i: (b, 0, 0)),
            pl.BlockSpec((None, 1, D_MODEL), lambda b, i: (b, 0, 0)),
            pl.BlockSpec((D_MODEL, N_EXPERTS), lambda b, i: (0, 0)),
            pl.BlockSpec((1, N_EXPERTS), lambda b, i: (0, 0)),
        ],
        out_specs=pl.BlockSpec((None, tm, N_EXPERTS), lambda b, i: (b, i, 0)),
        out_shape=jax.ShapeDtypeStruct((bsz, t, N_EXPERTS), f32),
        compiler_params=pltpu.CompilerParams(
            dimension_semantics=("parallel", "parallel"), vmem_limit_bytes=VMEM_LIMIT),
        name="route",
    )(x, g, sc, sh, router_w, router_b.reshape(1, N_EXPERTS))


def _moe_kernel(x_ref, g_ref, sc_ref, sh_ref, g2_ref, comb_ref, wg_ref, wu_ref, wd_ref, o_ref,
                h_scr, acc_scr):
    e = pl.program_id(2)

    @pl.when(e == 0)
    def _():
        h_scr[...] = _norm_mod(x_ref[...], g_ref[...], sc_ref[...], sh_ref[...]).astype(bf16)
        acc_scr[...] = jnp.zeros_like(acc_scr)

    h = h_scr[...]
    gate = _dot(h, wg_ref[...])
    hid = (gate * _sigmoid(gate)) * _dot(h, wu_ref[...])
    comb = comb_ref[...]
    lane = lax.broadcasted_iota(jnp.int32, comb.shape, 1)
    ce = jnp.sum(jnp.where(lane == e, comb, 0.0), axis=-1, keepdims=True)
    acc_scr[...] += _dot((ce * hid).astype(bf16), wd_ref[...])

    @pl.when(e == N_EXPERTS - 1)
    def _():
        o_ref[...] = x_ref[...] + g2_ref[...] * acc_scr[...]


def _moe(x, g, sc, sh, g2, comb, wg_bf, wu_bf, wd_bf):
    bsz, t, _ = x.shape
    tm = min(t, 1024)
    return pl.pallas_call(
        _moe_kernel,
        grid=(bsz, t // tm, N_EXPERTS),
        in_specs=[
            pl.BlockSpec((None, tm, D_MODEL), lambda b, i, e: (b, i, 0)),
            pl.BlockSpec((1, D_MODEL), lambda b, i, e: (0, 0)),
            pl.BlockSpec((None, 1, D_MODEL), lambda b, i, e: (b, 0, 0)),
            pl.BlockSpec((None, 1, D_MODEL), lambda b, i, e: (b, 0, 0)),
            pl.BlockSpec((None, 1, D_MODEL), lambda b, i, e: (b, 0, 0)),
            pl.BlockSpec((None, tm, N_EXPERTS), lambda b, i, e: (b, i, 0)),
            pl.BlockSpec((None, D_MODEL, D_EXPERT), lambda b, i, e: (e, 0, 0)),
            pl.BlockSpec((None, D_MODEL, D_EXPERT), lambda b, i, e: (e, 0, 0)),
            pl.BlockSpec((None, D_EXPERT, D_MODEL), lambda b, i, e: (e, 0, 0)),
        ],
        out_specs=pl.BlockSpec((None, tm, D_MODEL), lambda b, i, e: (b, i, 0)),
        out_shape=jax.ShapeDtypeStruct((bsz, t, D_MODEL), f32),
        scratch_shapes=[pltpu.VMEM((tm, D_MODEL), bf16), pltpu.VMEM((tm, D_MODEL), f32)],
        compiler_params=pltpu.CompilerParams(
            dimension_semantics=("parallel", "parallel", "arbitrary"), vmem_limit_bytes=VMEM_LIMIT),
        name="moe",
    )(x, g, sc, sh, g2, comb, wg_bf, wu_bf, wd_bf)


def _final_kernel(x_ref, g_ref, o_ref):
    x = x_ref[...]
    o_ref[...] = x * lax.rsqrt(jnp.mean(x * x, axis=-1, keepdims=True) + NORM_EPS) * g_ref[...]


def _final_norm(x, g):
    bsz, t, _ = x.shape
    tm = min(t, 1024)
    return pl.pallas_call(
        _final_kernel,
        grid=(bsz, t // tm),
        in_specs=[
            pl.BlockSpec((None, tm, D_MODEL), lambda b, i: (b, i, 0)),
            pl.BlockSpec((1, D_MODEL), lambda b, i: (0, 0)),
        ],
        out_specs=pl.BlockSpec((None, tm, D_MODEL), lambda b, i: (b, i, 0)),
        out_shape=jax.ShapeDtypeStruct((bsz, t, D_MODEL), f32),
        compiler_params=pltpu.CompilerParams(
            dimension_semantics=("parallel", "parallel"), vmem_limit_bytes=VMEM_LIMIT),
        name="final_norm",
    )(x, g.reshape(1, D_MODEL))


def _pack_w_in(w_in):
    ml, rw, gt = w_in[:, :ML_COLS], w_in[:, ML_COLS:ML_COLS + RW_COLS], w_in[:, ML_COLS + RW_COLS:]
    qkvo, mlg = ml[:, :2 * ML_QK + 2 * ML_V], ml[:, 2 * ML_QK + 2 * ML_V:]
    z = lambda n: jnp.zeros((D_MODEL, n), w_in.dtype)
    return jnp.concatenate(
        [gt, rw, z(P_RW_PAD - RW_COLS), qkvo, mlg, z(P_MLG_PAD - 4 * ML_HEADS)], axis=1).astype(bf16)


def _mixer(pfull, is_ctx, lp, ml_state, rw_state):
    hs, ml_fin = [], []
    for d in range(2):
        h_d, c_fin = _mlstm(pfull, lp["taps"], lp["conv_b"], lp["gate_b"], ml_state[d], rev=bool(d), d=d,
                            grid_conv=not is_ctx)
        hs.append(h_d)
        ml_fin.append(c_fin)
    yf, bvf, g, yb, bvb, sff, sfb = _rwkv(pfull, rw_state[0], rw_state[1], lp["rw_mu"], lp["rw_w_up"],
                                          lp["rw_w0"], lp["rw_a_up"], lp["rw_a0"], lp["rw_k_k"], lp["rw_k_a"],
                                          lp["rw_r_k"], lp["ones_bd"], lp["gup"])
    return (hs[0], hs[1], yf, yb, bvf, bvb, g), ml_fin, [sff, sfb]


def kernel(x, c, ctx, c_ctx, w_ada, b_ada, norm1_g, norm2_g, w_in, ml_conv_k, ml_conv_b, ml_gate_b, ml_norm_g, rw_mu, rw_w_up, rw_w0, rw_a_up, rw_a0, rw_g_up, rw_k_k, rw_k_a, rw_r_k, rw_ln_w, rw_ln_b, merge_pa, merge_pb, w_out, router_w, router_b, exp_w_gate, exp_w_up, exp_w_down, final_g):
    bsz = x.shape[0]
    s_rows = jnp.zeros((8, D_MODEL), f32).at[:bsz].set(c).at[bsz].set(c_ctx)
    mod = _ada(s_rows, w_ada, b_ada)
    head_id = jnp.arange(RW_W) // RW_N
    ones_bd = (head_id[:, None] == head_id[None, :]).astype(bf16)
    row = lambda v: v.reshape(1, -1)

    x_lat, x_ctx = x, ctx
    for l in range(DEPTH):
        last = l == DEPTH - 1
        mod_lat = mod[l, :bsz].reshape(bsz, 1, N_MOD, D_MODEL)
        mod_ctx = jnp.broadcast_to(mod[l, bsz].reshape(1, 1, N_MOD, D_MODEL), (bsz, 1, N_MOD, D_MODEL))
        lp = dict(taps=ml_conv_k[l].reshape(9, 2 * ML_QK), conv_b=row(ml_conv_b[l]), gate_b=row(ml_gate_b[l]),
                  rw_mu=jnp.pad(row(rw_mu[l]), ((0, 0), (0, P_RW_PAD - RW_COLS))),
                  rw_w_up=rw_w_up[l].astype(bf16), rw_w0=rw_w0[l], rw_a_up=rw_a_up[l].astype(bf16),
                  rw_a0=rw_a0[l], rw_k_k=row(rw_k_k[l]), rw_k_a=row(rw_k_a[l]), rw_r_k=row(rw_r_k[l]),
                  ones_bd=ones_bd, gup=rw_g_up[l].astype(bf16))
        w_in_bf = _pack_w_in(w_in[l])
        pa_bf, pb_bf, wo_bf = merge_pa[l].astype(bf16), merge_pb[l].astype(bf16), w_out[l].astype(bf16)
        wg_bf, wu_bf, wd_bf = exp_w_gate[l].astype(bf16), exp_w_up[l].astype(bf16), exp_w_down[l].astype(bf16)
        g1n, g2n = row(norm1_g[l]), row(norm2_g[l])
        readout = (row(ml_norm_g[l]), row(rw_ln_w[l]), row(rw_ln_b[l]), ones_bd, pa_bf, pb_bf, wo_bf)

        def m(modv, i):
            return modv[:, :, i]

        p_ctx = _proj(x_ctx, g1n, m(mod_ctx, 1), m(mod_ctx, 0), w_in_bf)
        ml0 = [jnp.zeros((bsz, ML_HEADS, ML_DQK, 2 * ML_DV), f32)] * 2
        rw0 = [jnp.zeros((bsz, RW_HEADS, RW_N, RW_N), f32)] * 2
        mix_c, ml_st, rw_st = _mixer(p_ctx, True, lp, ml0, rw0)

        p_lat = _proj(x_lat, g1n, m(mod_lat, 1), m(mod_lat, 0), w_in_bf)
        mix_l, _, _ = _mixer(p_lat, False, lp, ml_st, rw_st)
        x_lat = _merge(x_lat, *mix_l, p_lat, m(mod_lat, 2), *readout)
        comb = _route(x_lat, g2n, m(mod_lat, 4), m(mod_lat, 3), router_w, router_b)
        x_lat = _moe(x_lat, g2n, m(mod_lat, 4), m(mod_lat, 3), m(mod_lat, 5), comb, wg_bf, wu_bf, wd_bf)
        if not last:
            x_ctx = _merge(x_ctx, *mix_c, p_ctx, m(mod_ctx, 2), *readout)
            comb_c = _route(x_ctx, g2n, m(mod_ctx, 4), m(mod_ctx, 3), router_w, router_b)
            x_ctx = _moe(x_ctx, g2n, m(mod_ctx, 4), m(mod_ctx, 3), m(mod_ctx, 5), comb_c, wg_bf, wu_bf, wd_bf)
    return _final_norm(x_lat, final_g)
```

```python
import functools

import jax
import jax.numpy as jnp
import numpy as np
from jax import lax
from jax.experimental import pallas as pl
from jax.experimental.pallas import tpu as pltpu

f32 = jnp.float32
bf16 = jnp.bfloat16

D_MODEL = 1024
DEPTH = 2
GRID_W = 64
N_MOD = 6
NORM_EPS = 1e-6

ML_HEADS = 4
ML_DQK = 64
ML_DV = 128
ML_QK = ML_HEADS * ML_DQK
ML_V = ML_HEADS * ML_DV
GATE_CAP = 15.0
ML_COLS = 2 * ML_QK + 2 * ML_V + 4 * ML_HEADS
ML_CHUNK = 256

RW_HEADS = 8
RW_N = 64
RW_W = RW_HEADS * RW_N
RW_DECAY_LORA = 64
RW_A_LORA = 64
RW_GATE_LORA = 128
RW_GN_EPS = 6.4e-4
RW_COLS = 3 * RW_W + 2 * RW_DECAY_LORA + 2 * RW_A_LORA + RW_GATE_LORA
RW_CHUNK = 64
RW_BLOCK = 256

N_EXPERTS = 16
N_GROUPS = 4
EXPERTS_PER_GROUP = N_EXPERTS // N_GROUPS
D_EXPERT = 512

P_GA = 0
P_GB = D_MODEL
P_RW = 2 * D_MODEL
P_RW_PAD = 2048
P_QK = P_RW + P_RW_PAD
P_V = P_QK + 2 * ML_QK
P_O = P_V + ML_V
P_MLG = P_O + ML_V
P_MLG_PAD = 512
P_MLG_BLK = 128
P_COLS = P_MLG + P_MLG_PAD
PROJ_TN = 1024

VMEM_LIMIT = 48 * 1024 * 1024
EXP_NEG_HALF = float(np.exp(-0.5))
NEG_BIG = -1e30


_NN = ((1,), (0,))
_NT = ((1,), (1,))
_TN = ((0,), (0,))


def _dot(a, b, dims=_NN):
    return lax.dot_general(a, b, (dims, ((), ())), preferred_element_type=f32)


def _mm(a, b, dims=_NN):
    return _dot(a.astype(bf16), b.astype(bf16), dims)


def _hi_lo(x):
    hi = x.astype(bf16)
    lo = (x - hi.astype(f32)).astype(bf16)
    return hi, lo


def _mm3(a, b, dims=_NN):
    ah, al = _hi_lo(a)
    bh, bl = _hi_lo(b)
    return _dot(ah, bh, dims) + (_dot(ah, bl, dims) + _dot(al, bh, dims))


def _mm_exact_lhs(a_bf, b, dims=_NN):
    hi = b.astype(bf16)
    r1 = b - hi.astype(f32)
    mid = r1.astype(bf16)
    lo = (r1 - mid.astype(f32)).astype(bf16)
    return _dot(a_bf, hi, dims) + (_dot(a_bf, mid, dims) + _dot(a_bf, lo, dims))


def _head_sum(a, ones_bd):
    hi, lo = _hi_lo(a)
    return _dot(hi, ones_bd) + _dot(lo, ones_bd)


def _sigmoid(x):
    return 1.0 / (1.0 + jnp.exp(-x))


def _norm_mod(x, g, sc, sh):
    y = x * lax.rsqrt(jnp.mean(x * x, axis=-1, keepdims=True) + NORM_EPS)
    return (y * g) * (1.0 + sc) + sh


def _ada_kernel(s_ref, w_ref, b_ref, o_ref):
    s = s_ref[...]
    s = s * _sigmoid(s)
    o_ref[...] = _mm3(s, w_ref[...]) + b_ref[...]


def _ada(s_rows, w_ada, b_ada):
    tn = 1536
    n = N_MOD * D_MODEL
    return pl.pallas_call(
        _ada_kernel,
        grid=(DEPTH, n // tn),
        in_specs=[
            pl.BlockSpec((8, D_MODEL), lambda l, j: (0, 0)),
            pl.BlockSpec((None, D_MODEL, tn), lambda l, j: (l, 0, j)),
            pl.BlockSpec((None, 1, tn), lambda l, j: (l, 0, j)),
        ],
        out_specs=pl.BlockSpec((None, 8, tn), lambda l, j: (l, 0, j)),
        out_shape=jax.ShapeDtypeStruct((DEPTH, 8, n), f32),
        compiler_params=pltpu.CompilerParams(
            dimension_semantics=("arbitrary", "arbitrary"), vmem_limit_bytes=VMEM_LIMIT),
        name="ada",
    )(s_rows, w_ada, b_ada.reshape(DEPTH, 1, n))


def _proj_kernel(x_ref, g_ref, sc_ref, sh_ref, w_ref, o_ref, h_scr):
    @pl.when(pl.program_id(2) == 0)
    def _():
        h_scr[...] = _norm_mod(x_ref[...], g_ref[...], sc_ref[...], sh_ref[...]).astype(bf16)

    o_ref[...] = _dot(h_scr[...], w_ref[...])


def _proj(x, g, sc, sh, w_bf):
    bsz, t, _ = x.shape
    tm = min(t, 1024)
    return pl.pallas_call(
        _proj_kernel,
        grid=(bsz, t // tm, P_COLS // PROJ_TN),
        in_specs=[
            pl.BlockSpec((None, tm, D_MODEL), lambda b, i, j: (b, i, 0)),
            pl.BlockSpec((1, D_MODEL), lambda b, i, j: (0, 0)),
            pl.BlockSpec((None, 1, D_MODEL), lambda b, i, j: (b, 0, 0)),
            pl.BlockSpec((None, 1, D_MODEL), lambda b, i, j: (b, 0, 0)),
            pl.BlockSpec((D_MODEL, PROJ_TN), lambda b, i, j: (0, j)),
        ],
        out_specs=pl.BlockSpec((None, tm, PROJ_TN), lambda b, i, j: (b, i, j)),
        out_shape=jax.ShapeDtypeStruct((bsz, t, P_COLS), f32),
        scratch_shapes=[pltpu.VMEM((tm, D_MODEL), bf16)],
        compiler_params=pltpu.CompilerParams(
            dimension_semantics=("parallel", "parallel", "arbitrary"), vmem_limit_bytes=VMEM_LIMIT),
        name="proj",
    )(x, g, sc, sh, w_bf)


def _shift_rows(u, up_row, dn_row):
    n = u.shape[0]
    rid = lax.broadcasted_iota(jnp.int32, (n, 1), 0)
    up = jnp.where(rid == 0, up_row, pltpu.roll(u, 1, axis=0))
    dn = jnp.where(rid == n - 1, dn_row, pltpu.roll(u, n - 1, axis=0))
    return up, dn


def _log_sigmoid(x):
    return jnp.minimum(x, 0.0) - jnp.log1p(jnp.exp(-jnp.abs(x)))


def _mlstm_kernel(qk_ref, qkp_ref, qkn_ref, v_ref, mlg_ref, taps_ref, cb_ref, gb_ref, c0_ref,
                  h_ref, cfin_ref, c_scr, *, rev, d, grid_conv, nb):
    L = ML_CHUNK
    j = pl.program_id(1)
    jblk = (nb - 1 - j) if rev else j

    @pl.when(j == 0)
    def _():
        c_scr[...] = c0_ref[...]

    qk = qk_ref[...]
    taps = taps_ref[...]
    zero_row = jnp.zeros((1, 2 * ML_QK), f32)
    rid = lax.broadcasted_iota(jnp.int32, (L, 1), 0)
    if grid_conv:
        first_col = (rid % GRID_W) == 0
        last_col = (rid % GRID_W) == GRID_W - 1
        above = jnp.where(jblk > 0, qkp_ref[...], 0.0)
        below = jnp.where(jblk < nb - 1, qkn_ref[...], 0.0)
        ext = jnp.concatenate([above, qk, below], axis=0)
        bases = [(dr, ext[dr * GRID_W:dr * GRID_W + L]) for dr in range(3)]
    else:
        first_col = rid == 0
        last_col = rid == L - 1
        bases = [(1, qk)]
    conv = cb_ref[...]
    for dr, base in bases:
        up, dn = _shift_rows(base, zero_row, zero_row)
        conv = conv + (taps[3 * dr:3 * dr + 1] * jnp.where(first_col, 0.0, up)
                       + taps[3 * dr + 1:3 * dr + 2] * base
                       + taps[3 * dr + 2:3 * dr + 3] * jnp.where(last_col, 0.0, dn))
    q = conv[:, :ML_QK]
    k = conv[:, ML_QK:] * (ML_DQK ** -0.5)

    pre = mlg_ref[...][:, :4 * ML_HEADS] + gb_ref[...]
    pre = GATE_CAP * jnp.tanh(pre * (1.0 / GATE_CAP))
    ig = pre[:, d * ML_HEADS:(d + 1) * ML_HEADS]
    lf = _log_sigmoid(pre[:, (2 + d) * ML_HEADS:(3 + d) * ML_HEADS])
    ii = lax.broadcasted_iota(jnp.int32, (L, L), 0)
    jj = lax.broadcasted_iota(jnp.int32, (L, L), 1)
    incl = (jj >= ii) if rev else (jj <= ii)
    tri = incl.astype(bf16)
    bc = _mm_exact_lhs(tri, lf)
    cols = jnp.concatenate([ig, bc], axis=1)
    e_i = lax.broadcasted_iota(jnp.int32, (2 * ML_HEADS, 2 * ML_HEADS), 0)
    e_j = lax.broadcasted_iota(jnp.int32, (2 * ML_HEADS, 2 * ML_HEADS), 1)
    rows_ = _mm_exact_lhs((e_i == e_j).astype(bf16), cols, _NT)
    one_col = (lax.broadcasted_iota(jnp.int32, (L, ML_DV), 1) == 0).astype(f32)
    v = v_ref[...]

    hs = range(ML_HEADS)
    qh = [q[:, h * ML_DQK:(h + 1) * ML_DQK] for h in hs]
    kh = [k[:, h * ML_DQK:(h + 1) * ML_DQK] for h in hs]
    vp = [jnp.concatenate([v[:, h * ML_DV:(h + 1) * ML_DV], one_col], axis=1) for h in hs]
    b_col = [bc[:, h:h + 1] for h in hs]
    btot = [(b_col[h][0:1, :] if rev else b_col[h][L - 1:L, :]) for h in hs]
    decay = [jnp.exp(jnp.where(incl, (b_col[h] - rows_[ML_HEADS + h:ML_HEADS + h + 1, :]) + rows_[h:h + 1, :],
                               NEG_BIG)) for h in hs]
    qk_s = [_mm(qh[h], kh[h], _NT) for h in hs]
    qc = [_mm(qh[h], c_scr[h]) for h in hs]
    kw = [kh[h] * jnp.exp((btot[h] - b_col[h]) + ig[:, h:h + 1]) for h in hs]
    kv = [_mm(kw[h], vp[h], _TN) for h in hs]
    sv = [_mm(qk_s[h] * decay[h], vp[h]) for h in hs]
    for h in hs:
        nd = sv[h] + jnp.exp(b_col[h]) * qc[h]
        den = nd[:, ML_DV:ML_DV + 1]
        h_ref[:, h * ML_DV:(h + 1) * ML_DV] = nd[:, :ML_DV] / jnp.maximum(jnp.abs(den), 1.0)
        c_scr[h] = jnp.exp(btot[h]) * c_scr[h] + kv[h]
    cfin_ref[...] = c_scr[...]


def _mlstm(pfull, taps, conv_b, gate_b, c0, rev, d, grid_conv):
    bsz, t, _ = pfull.shape
    L = ML_CHUNK
    nb = t // L
    tb = (lambda j: nb - 1 - j) if rev else (lambda j: j)
    vblk = P_V // ML_V
    qkblk = P_QK // (2 * ML_QK)
    rpb = L // GRID_W
    n_rows = t // GRID_W
    return pl.pallas_call(
        functools.partial(_mlstm_kernel, rev=rev, d=d, grid_conv=grid_conv, nb=nb),
        grid=(bsz, nb),
        in_specs=[
            pl.BlockSpec((None, L, 2 * ML_QK), lambda b, j: (b, tb(j), qkblk)),
            pl.BlockSpec((None, GRID_W, 2 * ML_QK),
                         lambda b, j: (b, jnp.maximum(tb(j) * rpb - 1, 0), qkblk)),
            pl.BlockSpec((None, GRID_W, 2 * ML_QK),
                         lambda b, j: (b, jnp.minimum((tb(j) + 1) * rpb, n_rows - 1), qkblk)),
            pl.BlockSpec((None, L, ML_V), lambda b, j: (b, tb(j), vblk)),
            pl.BlockSpec((None, L, P_MLG_BLK), lambda b, j: (b, tb(j), P_MLG // P_MLG_BLK)),
            pl.BlockSpec((9, 2 * ML_QK), lambda b, j: (0, 0)),
            pl.BlockSpec((1, 2 * ML_QK), lambda b, j: (0, 0)),
            pl.BlockSpec((1, 4 * ML_HEADS), lambda b, j: (0, 0)),
            pl.BlockSpec((None, ML_HEADS, ML_DQK, 2 * ML_DV), lambda b, j: (b, 0, 0, 0)),
        ],
        out_specs=[
            pl.BlockSpec((None, L, ML_V), lambda b, j: (b, tb(j), 0)),
            pl.BlockSpec((None, ML_HEADS, ML_DQK, 2 * ML_DV), lambda b, j: (b, 0, 0, 0)),
        ],
        out_shape=[
            jax.ShapeDtypeStruct((bsz, t, ML_V), f32),
            jax.ShapeDtypeStruct((bsz, ML_HEADS, ML_DQK, 2 * ML_DV), f32),
        ],
        scratch_shapes=[pltpu.VMEM((ML_HEADS, ML_DQK, 2 * ML_DV), f32)],
        compiler_params=pltpu.CompilerParams(
            dimension_semantics=("parallel", "arbitrary"), vmem_limit_bytes=VMEM_LIMIT),
        name="mlstm_bwd" if rev else "mlstm_fwd",
    )(pfull, pfull, pfull, pfull, pfull, taps, conv_b, gate_b, c0)


def _rw_prepare(p_ref, pp_ref, pn_ref, jblk, nb, d, par, scr, bv_ref):
    mu_ref, wup_ref, w0_ref, aup_ref, a0_ref, kk_ref, ka_ref, rk_ref, ones_ref = par
    r_scr, v_scr, kk_scr, kd_scr, be_scr, ld_scr = scr
    p = p_ref[...]
    up, dn = _shift_rows(p, jnp.where(jblk > 0, pp_ref[7:8, :], 0.0),
                         jnp.where(jblk < nb - 1, pn_ref[0:1, :], 0.0))
    p = p + mu_ref[...] * (0.5 * (up + dn) - p)
    r = p[:, 0:RW_W]
    k = p[:, RW_W:2 * RW_W]
    v = p[:, 2 * RW_W:3 * RW_W]
    o_wd = 3 * RW_W + d * RW_DECAY_LORA
    o_ad = 3 * RW_W + 2 * RW_DECAY_LORA + d * RW_A_LORA
    lw = w0_ref[d:d + 1, :] + _mm(jnp.tanh(p[:, o_wd:o_wd + RW_DECAY_LORA]), wup_ref[d])
    ld_scr[...] = -EXP_NEG_HALF * _sigmoid(lw)
    a = _sigmoid(a0_ref[d:d + 1, :] + _mm(p[:, o_ad:o_ad + RW_A_LORA], aup_ref[d]))
    ones_bd = ones_ref[...]
    kkr = k * kk_ref[...]
    kk = kkr * lax.rsqrt(_head_sum(kkr * kkr, ones_bd) + 1e-12)
    kd = k * (1.0 + (a - 1.0) * ka_ref[...])
    bv_ref[...] = _head_sum(r * kd * rk_ref[...], ones_bd) * v
    r_scr[...] = r
    v_scr[...] = v
    kk_scr[...] = kk
    kd_scr[...] = kd
    be_scr[...] = kk * a
    return p


def _rwkv_kernel(pf_ref, pfp_ref, pfn_ref, pb_ref, pbp_ref, pbn_ref, s0f_ref, s0b_ref,
                 mu_ref, wup_ref, w0_ref, aup_ref, a0_ref, kk_ref, ka_ref, rk_ref, ones_ref, gup_ref,
                 yf_ref, bvf_ref, g_ref, yb_ref, bvb_ref, sff_ref, sfb_ref,
                 stf_scr, stb_scr, *scr, nb):
    L = RW_CHUNK
    nch = RW_BLOCK // L
    j = pl.program_id(1)

    @pl.when(j == 0)
    def _():
        stf_scr[...] = s0f_ref[...]
        stb_scr[...] = s0b_ref[...]

    par = (mu_ref, wup_ref, w0_ref, aup_ref, a0_ref, kk_ref, ka_ref, rk_ref, ones_ref)
    scr_f, scr_b = scr[:6], scr[6:]
    p_f = _rw_prepare(pf_ref, pfp_ref, pfn_ref, j, nb, 0, par, scr_f, bvf_ref)
    _rw_prepare(pb_ref, pbp_ref, pbn_ref, nb - 1 - j, nb, 1, par, scr_b, bvb_ref)
    o_gd = 3 * RW_W + 2 * RW_DECAY_LORA + 2 * RW_A_LORA
    g_ref[...] = _mm(_sigmoid(p_f[:, o_gd:o_gd + RW_GATE_LORA]), gup_ref[...])

    ii = lax.broadcasted_iota(jnp.int32, (L, L), 0)
    jj = lax.broadcasted_iota(jnp.int32, (L, L), 1)
    ii2 = lax.broadcasted_iota(jnp.int32, (L, 2 * L), 0)
    jj2 = lax.broadcasted_iota(jnp.int32, (L, 2 * L), 1) & (L - 1)
    masks = []
    for rev in (False, True):
        incl = (jj >= ii) if rev else (jj <= ii)
        strict = (jj > ii) if rev else (jj < ii)
        incl2 = (jj2 >= ii2) if rev else (jj2 <= ii2)
        masks.append((incl2, strict, incl.astype(bf16)))
    streams = ((scr_f, stf_scr, yf_ref, False), (scr_b, stb_scr, yb_ref, True))

    def chunk(i, carry):
        units = []
        rows_of = []
        for si, (sc, st_scr, y_ref, rev) in enumerate(streams):
            r_scr, v_scr, kk_scr, kd_scr, be_scr, ld_scr = sc
            incl2, strict, tri = masks[si]
            c = (nch - 1 - i) if rev else i
            rows = pl.ds(pl.multiple_of(c * L, L), L)
            rows_of.append(rows)
            ld = ld_scr[rows, :]
            b = _mm_exact_lhs(tri, ld)
            btot = b[0:1, :] if rev else b[L - 1:L, :]
            e_nb = jnp.exp(-b)
            e_end = jnp.exp(btot - b)
            kd_c = kd_scr[rows, :]
            be_c = be_scr[rows, :]
            v_c = v_scr[rows, :]
            aw = -kk_scr[rows, :] * jnp.exp(b - ld)
            rw = r_scr[rows, :] * jnp.exp(b)
            bi = be_c * e_nb
            ki = kd_c * e_nb
            bw = be_c * e_end
            kw = kd_c * e_end
            wend = jnp.exp(btot)
            for h in range(RW_HEADS):
                c_ = slice(h * RW_N, (h + 1) * RW_N)
                units.append(dict(si=si, h=h, incl2=incl2, strict=strict, st=st_scr, aw=aw[:, c_], rw=rw[:, c_],
                                  bi=bi[:, c_], ki=ki[:, c_], bw=bw[:, c_], kw=kw[:, c_], v=v_c[:, c_],
                                  wend=wend[:, c_]))
        us = range(len(units))
        m = [_mm(jnp.concatenate([u["aw"], u["rw"]], axis=0),
                 jnp.concatenate([u["bi"], u["ki"]], axis=0), _NT) for u in units]
        pw = [jnp.where(units[n]["strict"], m[n][:L, :L], 0.0) for n in us]
        mak = [jnp.where(units[n]["strict"], m[n][:L, L:], 0.0) for n in us]
        mr = [jnp.where(units[n]["incl2"], m[n][L:, :], 0.0) for n in us]
        mv = [_mm(mak[n], units[n]["v"]) for n in us]
        x = [jnp.concatenate([units[n]["aw"], mv[n]], axis=1) for n in us]
        for it in range(6):
            if it < 5:
                pr = [_mm(pw[n], jnp.concatenate([x[n], pw[n]], axis=1)) for n in us]
                pw = [pr[n][:, 2 * RW_N:] for n in us]
                x = [x[n] + pr[n][:, :2 * RW_N] for n in us]
            else:
                pr = [_mm(pw[n], x[n]) for n in us]
                x = [x[n] + pr[n] for n in us]
        zero = jnp.zeros((L, RW_N), f32)
        ray = [_mm(mr[n], jnp.concatenate(
            [x[n], jnp.concatenate([zero, units[n]["v"]], axis=1)], axis=0)) for n in us]
        gt = [_mm(x[n], units[n]["bw"], _TN) for n in us]
        vk = [_mm(units[n]["v"], units[n]["kw"], _TN) for n in us]
        s_old = [units[n]["st"][units[n]["h"]] for n in us]
        yy = [_mm(units[n]["rw"] + ray[n][:, :RW_N], s_old[n], _NT) for n in us]
        sg = [_mm(s_old[n], gt[n][:RW_N]) for n in us]
        for si, (sc, st_scr, y_ref, rev) in enumerate(streams):
            mine = [n for n in us if units[n]["si"] == si]
            y_ref[rows_of[si], :] = jnp.concatenate([yy[n] + ray[n][:, RW_N:] for n in mine], axis=1)
            for n in mine:
                st_scr[units[n]["h"]] = (s_old[n] * units[n]["wend"] + sg[n]) + (gt[n][RW_N:] + vk[n])
        return carry

    lax.fori_loop(0, nch, chunk, 0)
    sff_ref[...] = stf_scr[...]
    sfb_ref[...] = stb_scr[...]


def _rwkv(pfull, s0f, s0b, mu, wup, w0, aup, a0, k_k, k_a, r_k, ones_bd, gup):
    bsz, t, _ = pfull.shape
    tb_ = RW_BLOCK
    nb = t // tb_
    cblk = P_RW // P_RW_PAD
    hpb = tb_ // 8
    nh = t // 8
    fwd = lambda j: j
    bwd = lambda j: nb - 1 - j

    def p_specs(tb):
        return [
            pl.BlockSpec((None, tb_, P_RW_PAD), lambda b, j: (b, tb(j), cblk)),
            pl.BlockSpec((None, 8, P_RW_PAD), lambda b, j: (b, jnp.maximum(tb(j) * hpb - 1, 0), cblk)),
            pl.BlockSpec((None, 8, P_RW_PAD), lambda b, j: (b, jnp.minimum((tb(j) + 1) * hpb, nh - 1), cblk)),
        ]

    full = lambda shape: pl.BlockSpec(shape, lambda b, j: (0,) * len(shape))
    st_spec = pl.BlockSpec((None, RW_HEADS, RW_N, RW_N), lambda b, j: (b, 0, 0, 0))
    tok = lambda tb: pl.BlockSpec((None, tb_, RW_W), lambda b, j: (b, tb(j), 0))
    tok_shape = jax.ShapeDtypeStruct((bsz, t, RW_W), f32)
    st_shape = jax.ShapeDtypeStruct((bsz, RW_HEADS, RW_N, RW_N), f32)
    return pl.pallas_call(
        functools.partial(_rwkv_kernel, nb=nb),
        grid=(bsz, nb),
        in_specs=p_specs(fwd) + p_specs(bwd) + [
            st_spec, st_spec,
            full((1, P_RW_PAD)),
            full((2, RW_DECAY_LORA, RW_W)), full((2, RW_W)),
            full((2, RW_A_LORA, RW_W)), full((2, RW_W)),
            full((1, RW_W)), full((1, RW_W)), full((1, RW_W)),
            full((RW_W, RW_W)),
            full((RW_GATE_LORA, RW_W)),
        ],
        out_specs=[tok(fwd), tok(fwd), tok(fwd), tok(bwd), tok(bwd), st_spec, st_spec],
        out_shape=[tok_shape] * 5 + [st_shape] * 2,
        scratch_shapes=[pltpu.VMEM((RW_HEADS, RW_N, RW_N), f32) for _ in range(2)]
        + [pltpu.VMEM((tb_, RW_W), f32) for _ in range(12)],
        compiler_params=pltpu.CompilerParams(
            dimension_semantics=("parallel", "arbitrary"), vmem_limit_bytes=VMEM_LIMIT),
        name="rwkv",
    )(pfull, pfull, pfull, pfull, pfull, pfull, s0f, s0b, mu, wup, w0, aup, a0, k_k, k_a, r_k, ones_bd, gup)


def _merge_kernel(x_ref, hf_ref, hb_ref, o_ref_, yf_ref, yb_ref, bvf_ref, bvb_ref, g_ref, ga_ref, gb_ref,
                  g1_ref, mlg_ref, lnw_ref, lnb_ref, ones_ref, pa_ref, pb_ref, wo_ref, out_ref):
    hsum = hf_ref[...] + hb_ref[...]
    parts = []
    for h in range(ML_HEADS):
        hh = hsum[:, h * ML_DV:(h + 1) * ML_DV]
        parts.append(hh * lax.rsqrt(jnp.mean(hh * hh, axis=-1, keepdims=True) + NORM_EPS))
    a_lat = jnp.concatenate(parts, axis=1) * mlg_ref[...] * _sigmoid(o_ref_[...])
    y = yf_ref[...] + yb_ref[...]
    ones_bd = ones_ref[...]
    inv_n = 1.0 / RW_N
    mu = _head_sum(y, ones_bd) * inv_n
    dev = y - mu
    var = _head_sum(dev * dev, ones_bd) * inv_n
    yn = dev * lax.rsqrt(var + RW_GN_EPS) * lnw_ref[...] + lnb_ref[...]
    b_lat = (yn + (bvf_ref[...] + bvb_ref[...])) * g_ref[...]
    m = _sigmoid(ga_ref[...]) * _mm(a_lat, pa_ref[...]) + _sigmoid(gb_ref[...]) * _mm(b_lat, pb_ref[...])
    out_ref[...] = x_ref[...] + g1_ref[...] * _mm(m, wo_ref[...])


def _merge(x, hf, hb, yf, yb, bvf, bvb, g, pfull, g1, ml_norm_g, ln_w, ln_b, ones_bd, pa_bf, pb_bf, wo_bf):
    bsz, t, _ = x.shape
    tm = min(t, 512)
    full = lambda shape: pl.BlockSpec(shape, lambda b, i: (0,) * len(shape))
    tok = lambda w: pl.BlockSpec((None, tm, w), lambda b, i: (b, i, 0))
    col = lambda w, off: pl.BlockSpec((None, tm, w), lambda b, i: (b, i, off // w))
    return pl.pallas_call(
        _merge_kernel,
        grid=(bsz, t // tm),
        in_specs=[
            tok(D_MODEL), tok(ML_V), tok(ML_V), col(ML_V, P_O),
            tok(RW_W), tok(RW_W), tok(RW_W), tok(RW_W), tok(RW_W),
            col(D_MODEL, P_GA), col(D_MODEL, P_GB),
            pl.BlockSpec((None, 1, D_MODEL), lambda b, i: (b, 0, 0)),
            full((1, ML_V)), full((1, RW_W)), full((1, RW_W)), full((RW_W, RW_W)),
            full((ML_V, D_MODEL)), full((RW_W, D_MODEL)), full((D_MODEL, D_MODEL)),
        ],
        out_specs=tok(D_MODEL),
        out_shape=jax.ShapeDtypeStruct((bsz, t, D_MODEL), f32),
        compiler_params=pltpu.CompilerParams(
            dimension_semantics=("parallel", "parallel"), vmem_limit_bytes=VMEM_LIMIT),
        name="merge",
    )(x, hf, hb, pfull, yf, yb, bvf, bvb, g, pfull, pfull, g1, ml_norm_g, ln_w, ln_b, ones_bd,
      pa_bf, pb_bf, wo_bf)


def _lane_max(x):
    return jnp.max(x, axis=-1, keepdims=True)


def _first_at(x, val, lane_f):
    return jnp.min(jnp.where(x == val, lane_f, float(N_EXPERTS)), axis=-1, keepdims=True)


def _route_kernel(x_ref, g_ref, sc_ref, sh_ref, rw_ref, rb_ref, comb_ref):
    h = _norm_mod(x_ref[...], g_ref[...], sc_ref[...], sh_ref[...])
    scores = _sigmoid(_mm3(h, rw_ref[...]))
    sel = scores + rb_ref[...]
    lane = lax.broadcasted_iota(jnp.int32, sel.shape, 1)
    lane_f = lane.astype(f32)
    grp = lane // EXPERTS_PER_GROUP
    neg = -jnp.inf
    best_g = jnp.zeros(sel.shape[:1] + (1,), jnp.int32)
    best_v = None
    for gi in range(N_GROUPS):
        mg = jnp.where(grp == gi, sel, neg)
        m1 = _lane_max(mg)
        i1 = _first_at(mg, m1, lane_f)
        m2 = _lane_max(jnp.where(lane_f == i1, neg, mg))
        gs = m1 + m2
        if gi == 0:
            best_v = gs
        else:
            upd = gs > best_v
            best_g = jnp.where(upd, gi, best_g)
            best_v = jnp.where(upd, gs, best_v)
    cand = jnp.where(grp == best_g, sel, neg)
    v1 = _lane_max(cand)
    i1 = _first_at(cand, v1, lane_f)
    cand2 = jnp.where(lane_f == i1, neg, cand)
    v2 = _lane_max(cand2)
    i2 = _first_at(cand2, v2, lane_f)
    picked = (lane_f == i1) | (lane_f == i2)
    w = jnp.where(picked, scores, 0.0)
    comb_ref[...] = w / jnp.sum(w, axis=-1, keepdims=True)


def _route(x, g, sc, sh, router_w, router_b):
    bsz, t, _ = x.shape
    tm = min(t, 512)
    return pl.pallas_call(
        _route_kernel,
        grid=(bsz, t // tm),
        in_specs=[
            pl.BlockSpec((None, tm, D_MODEL), lambda b, i: (b, i, 0)),
            pl.BlockSpec((1, D_MODEL), lambda b, i: (0, 0)),
            pl.BlockSpec((None, 1, D_MODEL), lambda b, i: (b, 0, 0)),
            pl.BlockSpec((None, 1, D_MODEL), lambda b, i: (b, 0, 0)),
            pl.BlockSpec((D_MODEL, N_EXPERTS), lambda b, i: (0, 0)),
            pl.BlockSpec((1, N_EXPERTS), lambda b, i: (0, 0)),
        ],
        out_specs=pl.BlockSpec((None, tm, N_EXPERTS), lambda b, i: (b, i, 0)),
        out_shape=jax.ShapeDtypeStruct((bsz, t, N_EXPERTS), f32),
        compiler_params=pltpu.CompilerParams(
            dimension_semantics=("parallel", "parallel"), vmem_limit_bytes=VMEM_LIMIT),
        name="route",
    )(x, g, sc, sh, router_w, router_b.reshape(1, N_EXPERTS))


def _moe_kernel(x_ref, g_ref, sc_ref, sh_ref, g2_ref, comb_ref, fg_ref, wg_ref, wu_ref, wd_ref, o_ref,
                h_scr, acc_scr, *, final_norm):
    e = pl.program_id(2)

    @pl.when(e == 0)
    def _():
        h_scr[...] = _norm_mod(x_ref[...], g_ref[...], sc_ref[...], sh_ref[...]).astype(bf16)
        acc_scr[...] = jnp.zeros_like(acc_scr)

    h = h_scr[...]
    gate = _dot(h, wg_ref[...].astype(bf16))
    hid = (gate * _sigmoid(gate)) * _dot(h, wu_ref[...].astype(bf16))
    comb = comb_ref[...]
    lane = lax.broadcasted_iota(jnp.int32, comb.shape, 1)
    ce = jnp.sum(jnp.where(lane == e, comb, 0.0), axis=-1, keepdims=True)
    acc_scr[...] += _dot((ce * hid).astype(bf16), wd_ref[...].astype(bf16))

    @pl.when(e == N_EXPERTS - 1)
    def _():
        y = x_ref[...] + g2_ref[...] * acc_scr[...]
        if final_norm:
            y = y * lax.rsqrt(jnp.mean(y * y, axis=-1, keepdims=True) + NORM_EPS) * fg_ref[...]
        o_ref[...] = y


def _moe(x, g, sc, sh, g2, comb, final_g, final_norm, w_gate, w_up, w_down, l):
    bsz, t, _ = x.shape
    tm = min(t, 1024)
    return pl.pallas_call(
        functools.partial(_moe_kernel, final_norm=final_norm),
        grid=(bsz, t // tm, N_EXPERTS),
        in_specs=[
            pl.BlockSpec((None, tm, D_MODEL), lambda b, i, e: (b, i, 0)),
            pl.BlockSpec((1, D_MODEL), lambda b, i, e: (0, 0)),
            pl.BlockSpec((None, 1, D_MODEL), lambda b, i, e: (b, 0, 0)),
            pl.BlockSpec((None, 1, D_MODEL), lambda b, i, e: (b, 0, 0)),
            pl.BlockSpec((None, 1, D_MODEL), lambda b, i, e: (b, 0, 0)),
            pl.BlockSpec((None, tm, N_EXPERTS), lambda b, i, e: (b, i, 0)),
            pl.BlockSpec((1, D_MODEL), lambda b, i, e: (0, 0)),
            pl.BlockSpec((None, None, D_MODEL, D_EXPERT), lambda b, i, e: (l, e, 0, 0)),
            pl.BlockSpec((None, None, D_MODEL, D_EXPERT), lambda b, i, e: (l, e, 0, 0)),
            pl.BlockSpec((None, None, D_EXPERT, D_MODEL), lambda b, i, e: (l, e, 0, 0)),
        ],
        out_specs=pl.BlockSpec((None, tm, D_MODEL), lambda b, i, e: (b, i, 0)),
        out_shape=jax.ShapeDtypeStruct((bsz, t, D_MODEL), f32),
        scratch_shapes=[pltpu.VMEM((tm, D_MODEL), bf16), pltpu.VMEM((tm, D_MODEL), f32)],
        compiler_params=pltpu.CompilerParams(
            dimension_semantics=("parallel", "parallel", "arbitrary"), vmem_limit_bytes=VMEM_LIMIT),
        name="moe",
    )(x, g, sc, sh, g2, comb, final_g, w_gate, w_up, w_down)


def _pack_w_in(w_in):
    ml, rw, gt = w_in[:, :ML_COLS], w_in[:, ML_COLS:ML_COLS + RW_COLS], w_in[:, ML_COLS + RW_COLS:]
    qkvo, mlg = ml[:, :2 * ML_QK + 2 * ML_V], ml[:, 2 * ML_QK + 2 * ML_V:]
    z = lambda n: jnp.zeros((D_MODEL, n), w_in.dtype)
    return jnp.concatenate(
        [gt, rw, z(P_RW_PAD - RW_COLS), qkvo, mlg, z(P_MLG_PAD - 4 * ML_HEADS)], axis=1).astype(bf16)


def _mixer(pfull, is_ctx, lp, ml_state, rw_state):
    hs, ml_fin = [], []
    for d in range(2):
        h_d, c_fin = _mlstm(pfull, lp["taps"], lp["conv_b"], lp["gate_b"], ml_state[d], rev=bool(d), d=d,
                            grid_conv=not is_ctx)
        hs.append(h_d)
        ml_fin.append(c_fin)
    yf, bvf, g, yb, bvb, sff, sfb = _rwkv(pfull, rw_state[0], rw_state[1], lp["rw_mu"], lp["rw_w_up"],
                                          lp["rw_w0"], lp["rw_a_up"], lp["rw_a0"], lp["rw_k_k"], lp["rw_k_a"],
                                          lp["rw_r_k"], lp["ones_bd"], lp["gup"])
    return (hs[0], hs[1], yf, yb, bvf, bvb, g), ml_fin, [sff, sfb]


def kernel(x, c, ctx, c_ctx, w_ada, b_ada, norm1_g, norm2_g, w_in, ml_conv_k, ml_conv_b, ml_gate_b, ml_norm_g, rw_mu, rw_w_up, rw_w0, rw_a_up, rw_a0, rw_g_up, rw_k_k, rw_k_a, rw_r_k, rw_ln_w, rw_ln_b, merge_pa, merge_pb, w_out, router_w, router_b, exp_w_gate, exp_w_up, exp_w_down, final_g):
    bsz = x.shape[0]
    s_rows = jnp.zeros((8, D_MODEL), f32).at[:bsz].set(c).at[bsz].set(c_ctx)
    mod = _ada(s_rows, w_ada, b_ada)
    head_id = jnp.arange(RW_W) // RW_N
    ones_bd = (head_id[:, None] == head_id[None, :]).astype(bf16)
    row = lambda v: v.reshape(1, -1)

    x_lat, x_ctx = x, ctx
    for l in range(DEPTH):
        last = l == DEPTH - 1
        mod_lat = mod[l, :bsz].reshape(bsz, 1, N_MOD, D_MODEL)
        mod_ctx = jnp.broadcast_to(mod[l, bsz].reshape(1, 1, N_MOD, D_MODEL), (bsz, 1, N_MOD, D_MODEL))
        lp = dict(taps=ml_conv_k[l].reshape(9, 2 * ML_QK), conv_b=row(ml_conv_b[l]), gate_b=row(ml_gate_b[l]),
                  rw_mu=jnp.pad(row(rw_mu[l]), ((0, 0), (0, P_RW_PAD - RW_COLS))),
                  rw_w_up=rw_w_up[l].astype(bf16), rw_w0=rw_w0[l], rw_a_up=rw_a_up[l].astype(bf16),
                  rw_a0=rw_a0[l], rw_k_k=row(rw_k_k[l]), rw_k_a=row(rw_k_a[l]), rw_r_k=row(rw_r_k[l]),
                  ones_bd=ones_bd, gup=rw_g_up[l].astype(bf16))
        w_in_bf = _pack_w_in(w_in[l])
        pa_bf, pb_bf, wo_bf = merge_pa[l].astype(bf16), merge_pb[l].astype(bf16), w_out[l].astype(bf16)
        experts = (exp_w_gate, exp_w_up, exp_w_down, l)
        g1n, g2n = row(norm1_g[l]), row(norm2_g[l])
        readout = (row(ml_norm_g[l]), row(rw_ln_w[l]), row(rw_ln_b[l]), ones_bd, pa_bf, pb_bf, wo_bf)

        def m(modv, i):
            return modv[:, :, i]

        p_ctx = _proj(x_ctx, g1n, m(mod_ctx, 1), m(mod_ctx, 0), w_in_bf)
        ml0 = [jnp.zeros((bsz, ML_HEADS, ML_DQK, 2 * ML_DV), f32)] * 2
        rw0 = [jnp.zeros((bsz, RW_HEADS, RW_N, RW_N), f32)] * 2
        mix_c, ml_st, rw_st = _mixer(p_ctx, True, lp, ml0, rw0)

        p_lat = _proj(x_lat, g1n, m(mod_lat, 1), m(mod_lat, 0), w_in_bf)
        mix_l, _, _ = _mixer(p_lat, False, lp, ml_st, rw_st)
        x_lat = _merge(x_lat, *mix_l, p_lat, m(mod_lat, 2), *readout)
        comb = _route(x_lat, g2n, m(mod_lat, 4), m(mod_lat, 3), router_w, router_b)
        x_lat = _moe(x_lat, g2n, m(mod_lat, 4), m(mod_lat, 3), m(mod_lat, 5), comb, row(final_g), last, *experts)
        if not last:
            x_ctx = _merge(x_ctx, *mix_c, p_ctx, m(mod_ctx, 2), *readout)
            comb_c = _route(x_ctx, g2n, m(mod_ctx, 4), m(mod_ctx, 3), router_w, router_b)
            x_ctx = _moe(x_ctx, g2n, m(mod_ctx, 4), m(mod_ctx, 3), m(mod_ctx, 5), comb_c, row(final_g), False,
                         *experts)
    return x_lat
```

```python
import functools

import jax
import jax.numpy as jnp
import numpy as np
from jax import lax
from jax.experimental import pallas as pl
from jax.experimental.pallas import tpu as pltpu

f32 = jnp.float32
bf16 = jnp.bfloat16

D_MODEL = 1024
DEPTH = 2
GRID_W = 64
N_MOD = 6
NORM_EPS = 1e-6

ML_HEADS = 4
ML_DQK = 64
ML_DV = 128
ML_QK = ML_HEADS * ML_DQK
ML_V = ML_HEADS * ML_DV
GATE_CAP = 15.0
ML_COLS = 2 * ML_QK + 2 * ML_V + 4 * ML_HEADS
ML_CHUNK = 256

RW_HEADS = 8
RW_N = 64
RW_W = RW_HEADS * RW_N
RW_DECAY_LORA = 64
RW_A_LORA = 64
RW_GATE_LORA = 128
RW_GN_EPS = 6.4e-4
RW_COLS = 3 * RW_W + 2 * RW_DECAY_LORA + 2 * RW_A_LORA + RW_GATE_LORA
RW_CHUNK = 64
RW_BLOCK = 256

N_EXPERTS = 16
N_GROUPS = 4
EXPERTS_PER_GROUP = N_EXPERTS // N_GROUPS
D_EXPERT = 512

P_GA = 0
P_GB = D_MODEL
P_RW = 2 * D_MODEL
P_RW_PAD = 2048
P_QK = P_RW + P_RW_PAD
P_V = P_QK + 2 * ML_QK
P_O = P_V + ML_V
P_MLG = P_O + ML_V
P_MLG_PAD = 512
P_MLG_BLK = 128
P_COLS = P_MLG + P_MLG_PAD
PROJ_TN = 1024

VMEM_LIMIT = 48 * 1024 * 1024
MOE_VMEM_LIMIT = 58 * 1024 * 1024
ROUTE_W = 32
ROUTE_GID = N_EXPERTS
MOE_SB = 256
EXP_NEG_HALF = float(np.exp(-0.5))
NEG_BIG = -1e30


_NN = ((1,), (0,))
_NT = ((1,), (1,))
_TN = ((0,), (0,))


def _dot(a, b, dims=_NN):
    return lax.dot_general(a, b, (dims, ((), ())), preferred_element_type=f32)


def _mm(a, b, dims=_NN):
    return _dot(a.astype(bf16), b.astype(bf16), dims)


def _hi_lo(x):
    hi = x.astype(bf16)
    lo = (x - hi.astype(f32)).astype(bf16)
    return hi, lo


def _mm3(a, b, dims=_NN):
    ah, al = _hi_lo(a)
    bh, bl = _hi_lo(b)
    return _dot(ah, bh, dims) + (_dot(ah, bl, dims) + _dot(al, bh, dims))


def _mm_exact_lhs(a_bf, b, dims=_NN):
    hi = b.astype(bf16)
    r1 = b - hi.astype(f32)
    mid = r1.astype(bf16)
    lo = (r1 - mid.astype(f32)).astype(bf16)
    return _dot(a_bf, hi, dims) + (_dot(a_bf, mid, dims) + _dot(a_bf, lo, dims))


def _head_sum(a, ones_bd):
    hi, lo = _hi_lo(a)
    return _dot(hi, ones_bd) + _dot(lo, ones_bd)


def _sigmoid(x):
    return 1.0 / (1.0 + jnp.exp(-x))


def _norm_mod(x, g, sc, sh):
    y = x * lax.rsqrt(jnp.mean(x * x, axis=-1, keepdims=True) + NORM_EPS)
    return (y * g) * (1.0 + sc) + sh


def _ada_kernel(s_ref, w_ref, b_ref, o_ref):
    s = s_ref[...]
    s = s * _sigmoid(s)
    o_ref[...] = _mm3(s, w_ref[...]) + b_ref[...]


def _ada(s_rows, w_ada, b_ada):
    tn = 1536
    n = N_MOD * D_MODEL
    return pl.pallas_call(
        _ada_kernel,
        grid=(DEPTH, n // tn),
        in_specs=[
            pl.BlockSpec((8, D_MODEL), lambda l, j: (0, 0)),
            pl.BlockSpec((None, D_MODEL, tn), lambda l, j: (l, 0, j)),
            pl.BlockSpec((None, 1, tn), lambda l, j: (l, 0, j)),
        ],
        out_specs=pl.BlockSpec((None, 8, tn), lambda l, j: (l, 0, j)),
        out_shape=jax.ShapeDtypeStruct((DEPTH, 8, n), f32),
        compiler_params=pltpu.CompilerParams(
            dimension_semantics=("arbitrary", "arbitrary"), vmem_limit_bytes=VMEM_LIMIT),
        name="ada",
    )(s_rows, w_ada, b_ada.reshape(DEPTH, 1, n))


def _proj_kernel(x_ref, g_ref, sc_ref, sh_ref, w_ref, o_ref, h_scr):
    @pl.when(pl.program_id(2) == 0)
    def _():
        h_scr[...] = _norm_mod(x_ref[...], g_ref[...], sc_ref[...], sh_ref[...]).astype(bf16)

    o_ref[...] = _dot(h_scr[...], w_ref[...])


def _proj(x, g, sc, sh, w_bf):
    bsz, t, _ = x.shape
    tm = min(t, 1024)
    return pl.pallas_call(
        _proj_kernel,
        grid=(bsz, t // tm, P_COLS // PROJ_TN),
        in_specs=[
            pl.BlockSpec((None, tm, D_MODEL), lambda b, i, j: (b, i, 0)),
            pl.BlockSpec((1, D_MODEL), lambda b, i, j: (0, 0)),
            pl.BlockSpec((None, 1, D_MODEL), lambda b, i, j: (b, 0, 0)),
            pl.BlockSpec((None, 1, D_MODEL), lambda b, i, j: (b, 0, 0)),
            pl.BlockSpec((D_MODEL, PROJ_TN), lambda b, i, j: (0, j)),
        ],
        out_specs=pl.BlockSpec((None, tm, PROJ_TN), lambda b, i, j: (b, i, j)),
        out_shape=jax.ShapeDtypeStruct((bsz, t, P_COLS), f32),
        scratch_shapes=[pltpu.VMEM((tm, D_MODEL), bf16)],
        compiler_params=pltpu.CompilerParams(
            dimension_semantics=("parallel", "parallel", "arbitrary"), vmem_limit_bytes=VMEM_LIMIT),
        name="proj",
    )(x, g, sc, sh, w_bf)


def _shift_rows(u, up_row, dn_row):
    n = u.shape[0]
    rid = lax.broadcasted_iota(jnp.int32, (n, 1), 0)
    up = jnp.where(rid == 0, up_row, pltpu.roll(u, 1, axis=0))
    dn = jnp.where(rid == n - 1, dn_row, pltpu.roll(u, n - 1, axis=0))
    return up, dn


def _log_sigmoid(x):
    return jnp.minimum(x, 0.0) - jnp.log1p(jnp.exp(-jnp.abs(x)))


def _mlstm_kernel(qk_ref, qkp_ref, qkn_ref, v_ref, mlg_ref, taps_ref, cb_ref, gb_ref, c0_ref,
                  h_ref, cfin_ref, c_scr, *, rev, d, grid_conv, nb):
    L = ML_CHUNK
    j = pl.program_id(1)
    jblk = (nb - 1 - j) if rev else j

    @pl.when(j == 0)
    def _():
        c_scr[...] = c0_ref[...]

    qk = qk_ref[...]
    taps = taps_ref[...]
    zero_row = jnp.zeros((1, 2 * ML_QK), f32)
    rid = lax.broadcasted_iota(jnp.int32, (L, 1), 0)
    if grid_conv:
        first_col = (rid % GRID_W) == 0
        last_col = (rid % GRID_W) == GRID_W - 1
        above = jnp.where(jblk > 0, qkp_ref[...], 0.0)
        below = jnp.where(jblk < nb - 1, qkn_ref[...], 0.0)
        ext = jnp.concatenate([above, qk, below], axis=0)
        bases = [(dr, ext[dr * GRID_W:dr * GRID_W + L]) for dr in range(3)]
    else:
        first_col = rid == 0
        last_col = rid == L - 1
        bases = [(1, qk)]
    conv = cb_ref[...]
    for dr, base in bases:
        up, dn = _shift_rows(base, zero_row, zero_row)
        conv = conv + (taps[3 * dr:3 * dr + 1] * jnp.where(first_col, 0.0, up)
                       + taps[3 * dr + 1:3 * dr + 2] * base
                       + taps[3 * dr + 2:3 * dr + 3] * jnp.where(last_col, 0.0, dn))
    q = conv[:, :ML_QK]
    k = conv[:, ML_QK:] * (ML_DQK ** -0.5)

    pre = mlg_ref[...][:, :4 * ML_HEADS] + gb_ref[...]
    pre = GATE_CAP * jnp.tanh(pre * (1.0 / GATE_CAP))
    ig = pre[:, d * ML_HEADS:(d + 1) * ML_HEADS]
    lf = _log_sigmoid(pre[:, (2 + d) * ML_HEADS:(3 + d) * ML_HEADS])
    ii = lax.broadcasted_iota(jnp.int32, (L, L), 0)
    jj = lax.broadcasted_iota(jnp.int32, (L, L), 1)
    incl = (jj >= ii) if rev else (jj <= ii)
    tri = incl.astype(bf16)
    bc = _mm_exact_lhs(tri, lf)
    cols = jnp.concatenate([ig, bc], axis=1)
    e_i = lax.broadcasted_iota(jnp.int32, (2 * ML_HEADS, 2 * ML_HEADS), 0)
    e_j = lax.broadcasted_iota(jnp.int32, (2 * ML_HEADS, 2 * ML_HEADS), 1)
    rows_ = _mm_exact_lhs((e_i == e_j).astype(bf16), cols, _NT)
    one_col = (lax.broadcasted_iota(jnp.int32, (L, ML_DV), 1) == 0).astype(f32)
    v = v_ref[...]

    hs = range(ML_HEADS)
    qh = [q[:, h * ML_DQK:(h + 1) * ML_DQK] for h in hs]
    kh = [k[:, h * ML_DQK:(h + 1) * ML_DQK] for h in hs]
    vp = [jnp.concatenate([v[:, h * ML_DV:(h + 1) * ML_DV], one_col], axis=1) for h in hs]
    b_col = [bc[:, h:h + 1] for h in hs]
    btot = [(b_col[h][0:1, :] if rev else b_col[h][L - 1:L, :]) for h in hs]
    decay = [jnp.exp(jnp.where(incl, (b_col[h] - rows_[ML_HEADS + h:ML_HEADS + h + 1, :]) + rows_[h:h + 1, :],
                               NEG_BIG)) for h in hs]
    qk_s = [_mm(qh[h], kh[h], _NT) for h in hs]
    qc = [_mm(qh[h], c_scr[h]) for h in hs]
    kw = [kh[h] * jnp.exp((btot[h] - b_col[h]) + ig[:, h:h + 1]) for h in hs]
    kv = [_mm(kw[h], vp[h], _TN) for h in hs]
    sv = [_mm(qk_s[h] * decay[h], vp[h]) for h in hs]
    for h in hs:
        nd = sv[h] + jnp.exp(b_col[h]) * qc[h]
        den = nd[:, ML_DV:ML_DV + 1]
        h_ref[:, h * ML_DV:(h + 1) * ML_DV] = nd[:, :ML_DV] / jnp.maximum(jnp.abs(den), 1.0)
        c_scr[h] = jnp.exp(btot[h]) * c_scr[h] + kv[h]
    cfin_ref[...] = c_scr[...]


def _mlstm(pfull, taps, conv_b, gate_b, c0, rev, d, grid_conv):
    bsz, t, _ = pfull.shape
    L = ML_CHUNK
    nb = t // L
    tb = (lambda j: nb - 1 - j) if rev else (lambda j: j)
    vblk = P_V // ML_V
    qkblk = P_QK // (2 * ML_QK)
    rpb = L // GRID_W
    n_rows = t // GRID_W
    return pl.pallas_call(
        functools.partial(_mlstm_kernel, rev=rev, d=d, grid_conv=grid_conv, nb=nb),
        grid=(bsz, nb),
        in_specs=[
            pl.BlockSpec((None, L, 2 * ML_QK), lambda b, j: (b, tb(j), qkblk)),
            pl.BlockSpec((None, GRID_W, 2 * ML_QK),
                         lambda b, j: (b, jnp.maximum(tb(j) * rpb - 1, 0), qkblk)),
            pl.BlockSpec((None, GRID_W, 2 * ML_QK),
                         lambda b, j: (b, jnp.minimum((tb(j) + 1) * rpb, n_rows - 1), qkblk)),
            pl.BlockSpec((None, L, ML_V), lambda b, j: (b, tb(j), vblk)),
            pl.BlockSpec((None, L, P_MLG_BLK), lambda b, j: (b, tb(j), P_MLG // P_MLG_BLK)),
            pl.BlockSpec((9, 2 * ML_QK), lambda b, j: (0, 0)),
            pl.BlockSpec((1, 2 * ML_QK), lambda b, j: (0, 0)),
            pl.BlockSpec((1, 4 * ML_HEADS), lambda b, j: (0, 0)),
            pl.BlockSpec((None, ML_HEADS, ML_DQK, 2 * ML_DV), lambda b, j: (b, 0, 0, 0)),
        ],
        out_specs=[
            pl.BlockSpec((None, L, ML_V), lambda b, j: (b, tb(j), 0)),
            pl.BlockSpec((None, ML_HEADS, ML_DQK, 2 * ML_DV), lambda b, j: (b, 0, 0, 0)),
        ],
        out_shape=[
            jax.ShapeDtypeStruct((bsz, t, ML_V), f32),
            jax.ShapeDtypeStruct((bsz, ML_HEADS, ML_DQK, 2 * ML_DV), f32),
        ],
        scratch_shapes=[pltpu.VMEM((ML_HEADS, ML_DQK, 2 * ML_DV), f32)],
        compiler_params=pltpu.CompilerParams(
            dimension_semantics=("parallel", "arbitrary"), vmem_limit_bytes=VMEM_LIMIT),
        name="mlstm_bwd" if rev else "mlstm_fwd",
    )(pfull, pfull, pfull, pfull, pfull, taps, conv_b, gate_b, c0)


def _rw_prepare(p_ref, pp_ref, pn_ref, jblk, nb, d, par, scr, bv_ref):
    mu_ref, wup_ref, w0_ref, aup_ref, a0_ref, kk_ref, ka_ref, rk_ref, ones_ref = par
    r_scr, v_scr, kk_scr, kd_scr, be_scr, ld_scr = scr
    p = p_ref[...]
    up, dn = _shift_rows(p, jnp.where(jblk > 0, pp_ref[7:8, :], 0.0),
                         jnp.where(jblk < nb - 1, pn_ref[0:1, :], 0.0))
    p = p + mu_ref[...] * (0.5 * (up + dn) - p)
    r = p[:, 0:RW_W]
    k = p[:, RW_W:2 * RW_W]
    v = p[:, 2 * RW_W:3 * RW_W]
    o_wd = 3 * RW_W + d * RW_DECAY_LORA
    o_ad = 3 * RW_W + 2 * RW_DECAY_LORA + d * RW_A_LORA
    lw = w0_ref[d:d + 1, :] + _mm(jnp.tanh(p[:, o_wd:o_wd + RW_DECAY_LORA]), wup_ref[d])
    ld_scr[...] = -EXP_NEG_HALF * _sigmoid(lw)
    a = _sigmoid(a0_ref[d:d + 1, :] + _mm(p[:, o_ad:o_ad + RW_A_LORA], aup_ref[d]))
    ones_bd = ones_ref[...]
    kkr = k * kk_ref[...]
    kk = kkr * lax.rsqrt(_head_sum(kkr * kkr, ones_bd) + 1e-12)
    kd = k * (1.0 + (a - 1.0) * ka_ref[...])
    bv_ref[...] = _head_sum(r * kd * rk_ref[...], ones_bd) * v
    r_scr[...] = r
    v_scr[...] = v
    kk_scr[...] = kk
    kd_scr[...] = kd
    be_scr[...] = kk * a
    return p


def _rwkv_kernel(pf_ref, pfp_ref, pfn_ref, pb_ref, pbp_ref, pbn_ref, s0f_ref, s0b_ref,
                 mu_ref, wup_ref, w0_ref, aup_ref, a0_ref, kk_ref, ka_ref, rk_ref, ones_ref, gup_ref,
                 yf_ref, bvf_ref, g_ref, yb_ref, bvb_ref, sff_ref, sfb_ref,
                 stf_scr, stb_scr, *scr, nb):
    L = RW_CHUNK
    nch = RW_BLOCK // L
    j = pl.program_id(1)

    @pl.when(j == 0)
    def _():
        stf_scr[...] = s0f_ref[...]
        stb_scr[...] = s0b_ref[...]

    par = (mu_ref, wup_ref, w0_ref, aup_ref, a0_ref, kk_ref, ka_ref, rk_ref, ones_ref)
    scr_f, scr_b = scr[:6], scr[6:]
    p_f = _rw_prepare(pf_ref, pfp_ref, pfn_ref, j, nb, 0, par, scr_f, bvf_ref)
    _rw_prepare(pb_ref, pbp_ref, pbn_ref, nb - 1 - j, nb, 1, par, scr_b, bvb_ref)
    o_gd = 3 * RW_W + 2 * RW_DECAY_LORA + 2 * RW_A_LORA
    g_ref[...] = _mm(_sigmoid(p_f[:, o_gd:o_gd + RW_GATE_LORA]), gup_ref[...])

    ii = lax.broadcasted_iota(jnp.int32, (L, L), 0)
    jj = lax.broadcasted_iota(jnp.int32, (L, L), 1)
    ii2 = lax.broadcasted_iota(jnp.int32, (L, 2 * L), 0)
    jj2 = lax.broadcasted_iota(jnp.int32, (L, 2 * L), 1) & (L - 1)
    masks = []
    for rev in (False, True):
        incl = (jj >= ii) if rev else (jj <= ii)
        strict = (jj > ii) if rev else (jj < ii)
        incl2 = (jj2 >= ii2) if rev else (jj2 <= ii2)
        masks.append((incl2, strict, incl.astype(bf16)))
    streams = ((scr_f, stf_scr, yf_ref, False), (scr_b, stb_scr, yb_ref, True))

    def chunk(i, carry):
        units = []
        rows_of = []
        for si, (sc, st_scr, y_ref, rev) in enumerate(streams):
            r_scr, v_scr, kk_scr, kd_scr, be_scr, ld_scr = sc
            incl2, strict, tri = masks[si]
            c = (nch - 1 - i) if rev else i
            rows = pl.ds(pl.multiple_of(c * L, L), L)
            rows_of.append(rows)
            ld = ld_scr[rows, :]
            b = _mm_exact_lhs(tri, ld)
            btot = b[0:1, :] if rev else b[L - 1:L, :]
            e_nb = jnp.exp(-b)
            e_end = jnp.exp(btot - b)
            kd_c = kd_scr[rows, :]
            be_c = be_scr[rows, :]
            v_c = v_scr[rows, :]
            aw = -kk_scr[rows, :] * jnp.exp(b - ld)
            rw = r_scr[rows, :] * jnp.exp(b)
            bi = be_c * e_nb
            ki = kd_c * e_nb
            bw = be_c * e_end
            kw = kd_c * e_end
            wend = jnp.exp(btot)
            for h in range(RW_HEADS):
                c_ = slice(h * RW_N, (h + 1) * RW_N)
                units.append(dict(si=si, h=h, incl2=incl2, strict=strict, st=st_scr, aw=aw[:, c_], rw=rw[:, c_],
                                  bi=bi[:, c_], ki=ki[:, c_], bw=bw[:, c_], kw=kw[:, c_], v=v_c[:, c_],
                                  wend=wend[:, c_]))
        us = range(len(units))
        m = [_mm(jnp.concatenate([u["aw"], u["rw"]], axis=0),
                 jnp.concatenate([u["bi"], u["ki"]], axis=0), _NT) for u in units]
        pw = [jnp.where(units[n]["strict"], m[n][:L, :L], 0.0) for n in us]
        mak = [jnp.where(units[n]["strict"], m[n][:L, L:], 0.0) for n in us]
        mr = [jnp.where(units[n]["incl2"], m[n][L:, :], 0.0) for n in us]
        mv = [_mm(mak[n], units[n]["v"]) for n in us]
        x = [jnp.concatenate([units[n]["aw"], mv[n]], axis=1) for n in us]
        for it in range(6):
            if it < 5:
                pr = [_mm(pw[n], jnp.concatenate([x[n], pw[n]], axis=1)) for n in us]
                pw = [pr[n][:, 2 * RW_N:] for n in us]
                x = [x[n] + pr[n][:, :2 * RW_N] for n in us]
            else:
                pr = [_mm(pw[n], x[n]) for n in us]
                x = [x[n] + pr[n] for n in us]
        zero = jnp.zeros((L, RW_N), f32)
        ray = [_mm(mr[n], jnp.concatenate(
            [x[n], jnp.concatenate([zero, units[n]["v"]], axis=1)], axis=0)) for n in us]
        gt = [_mm(x[n], units[n]["bw"], _TN) for n in us]
        vk = [_mm(units[n]["v"], units[n]["kw"], _TN) for n in us]
        s_old = [units[n]["st"][units[n]["h"]] for n in us]
        yy = [_mm(units[n]["rw"] + ray[n][:, :RW_N], s_old[n], _NT) for n in us]
        sg = [_mm(s_old[n], gt[n][:RW_N]) for n in us]
        for si, (sc, st_scr, y_ref, rev) in enumerate(streams):
            mine = [n for n in us if units[n]["si"] == si]
            y_ref[rows_of[si], :] = jnp.concatenate([yy[n] + ray[n][:, RW_N:] for n in mine], axis=1)
            for n in mine:
                st_scr[units[n]["h"]] = (s_old[n] * units[n]["wend"] + sg[n]) + (gt[n][RW_N:] + vk[n])
        return carry

    lax.fori_loop(0, nch, chunk, 0)
    sff_ref[...] = stf_scr[...]
    sfb_ref[...] = stb_scr[...]


def _rwkv(pfull, s0f, s0b, mu, wup, w0, aup, a0, k_k, k_a, r_k, ones_bd, gup):
    bsz, t, _ = pfull.shape
    tb_ = RW_BLOCK
    nb = t // tb_
    cblk = P_RW // P_RW_PAD
    hpb = tb_ // 8
    nh = t // 8
    fwd = lambda j: j
    bwd = lambda j: nb - 1 - j

    def p_specs(tb):
        return [
            pl.BlockSpec((None, tb_, P_RW_PAD), lambda b, j: (b, tb(j), cblk)),
            pl.BlockSpec((None, 8, P_RW_PAD), lambda b, j: (b, jnp.maximum(tb(j) * hpb - 1, 0), cblk)),
            pl.BlockSpec((None, 8, P_RW_PAD), lambda b, j: (b, jnp.minimum((tb(j) + 1) * hpb, nh - 1), cblk)),
        ]

    full = lambda shape: pl.BlockSpec(shape, lambda b, j: (0,) * len(shape))
    st_spec = pl.BlockSpec((None, RW_HEADS, RW_N, RW_N), lambda b, j: (b, 0, 0, 0))
    tok = lambda tb: pl.BlockSpec((None, tb_, RW_W), lambda b, j: (b, tb(j), 0))
    tok_shape = jax.ShapeDtypeStruct((bsz, t, RW_W), f32)
    st_shape = jax.ShapeDtypeStruct((bsz, RW_HEADS, RW_N, RW_N), f32)
    return pl.pallas_call(
        functools.partial(_rwkv_kernel, nb=nb),
        grid=(bsz, nb),
        in_specs=p_specs(fwd) + p_specs(bwd) + [
            st_spec, st_spec,
            full((1, P_RW_PAD)),
            full((2, RW_DECAY_LORA, RW_W)), full((2, RW_W)),
            full((2, RW_A_LORA, RW_W)), full((2, RW_W)),
            full((1, RW_W)), full((1, RW_W)), full((1, RW_W)),
            full((RW_W, RW_W)),
            full((RW_GATE_LORA, RW_W)),
        ],
        out_specs=[tok(fwd), tok(fwd), tok(fwd), tok(bwd), tok(bwd), st_spec, st_spec],
        out_shape=[tok_shape] * 5 + [st_shape] * 2,
        scratch_shapes=[pltpu.VMEM((RW_HEADS, RW_N, RW_N), f32) for _ in range(2)]
        + [pltpu.VMEM((tb_, RW_W), f32) for _ in range(12)],
        compiler_params=pltpu.CompilerParams(
            dimension_semantics=("parallel", "arbitrary"), vmem_limit_bytes=VMEM_LIMIT),
        name="rwkv",
    )(pfull, pfull, pfull, pfull, pfull, pfull, s0f, s0b, mu, wup, w0, aup, a0, k_k, k_a, r_k, ones_bd, gup)


def _merge_kernel(x_ref, hf_ref, hb_ref, o_ref_, yf_ref, yb_ref, bvf_ref, bvb_ref, g_ref, ga_ref, gb_ref,
                  g1_ref, mlg_ref, lnw_ref, lnb_ref, ones_ref, pa_ref, pb_ref, wo_ref, out_ref):
    hsum = hf_ref[...] + hb_ref[...]
    parts = []
    for h in range(ML_HEADS):
        hh = hsum[:, h * ML_DV:(h + 1) * ML_DV]
        parts.append(hh * lax.rsqrt(jnp.mean(hh * hh, axis=-1, keepdims=True) + NORM_EPS))
    a_lat = jnp.concatenate(parts, axis=1) * mlg_ref[...] * _sigmoid(o_ref_[...])
    y = yf_ref[...] + yb_ref[...]
    ones_bd = ones_ref[...]
    inv_n = 1.0 / RW_N
    mu = _head_sum(y, ones_bd) * inv_n
    dev = y - mu
    var = _head_sum(dev * dev, ones_bd) * inv_n
    yn = dev * lax.rsqrt(var + RW_GN_EPS) * lnw_ref[...] + lnb_ref[...]
    b_lat = (yn + (bvf_ref[...] + bvb_ref[...])) * g_ref[...]
    m = _sigmoid(ga_ref[...]) * _mm(a_lat, pa_ref[...]) + _sigmoid(gb_ref[...]) * _mm(b_lat, pb_ref[...])
    out_ref[...] = x_ref[...] + g1_ref[...] * _mm(m, wo_ref[...])


def _merge(x, hf, hb, yf, yb, bvf, bvb, g, pfull, g1, ml_norm_g, ln_w, ln_b, ones_bd, pa_bf, pb_bf, wo_bf):
    bsz, t, _ = x.shape
    tm = min(t, 512)
    full = lambda shape: pl.BlockSpec(shape, lambda b, i: (0,) * len(shape))
    tok = lambda w: pl.BlockSpec((None, tm, w), lambda b, i: (b, i, 0))
    col = lambda w, off: pl.BlockSpec((None, tm, w), lambda b, i: (b, i, off // w))
    return pl.pallas_call(
        _merge_kernel,
        grid=(bsz, t // tm),
        in_specs=[
            tok(D_MODEL), tok(ML_V), tok(ML_V), col(ML_V, P_O),
            tok(RW_W), tok(RW_W), tok(RW_W), tok(RW_W), tok(RW_W),
            col(D_MODEL, P_GA), col(D_MODEL, P_GB),
            pl.BlockSpec((None, 1, D_MODEL), lambda b, i: (b, 0, 0)),
            full((1, ML_V)), full((1, RW_W)), full((1, RW_W)), full((RW_W, RW_W)),
            full((ML_V, D_MODEL)), full((RW_W, D_MODEL)), full((D_MODEL, D_MODEL)),
        ],
        out_specs=tok(D_MODEL),
        out_shape=jax.ShapeDtypeStruct((bsz, t, D_MODEL), f32),
        compiler_params=pltpu.CompilerParams(
            dimension_semantics=("parallel", "parallel"), vmem_limit_bytes=VMEM_LIMIT),
        name="merge",
    )(x, hf, hb, pfull, yf, yb, bvf, bvb, g, pfull, pfull, g1, ml_norm_g, ln_w, ln_b, ones_bd,
      pa_bf, pb_bf, wo_bf)


def _lane_max(x):
    return jnp.max(x, axis=-1, keepdims=True)


def _first_at(x, val, lane_f):
    return jnp.min(jnp.where(x == val, lane_f, float(ROUTE_W)), axis=-1, keepdims=True)


def _route_kernel(x_ref, g_ref, sc_ref, sh_ref, rw_ref, rb_ref, rt_ref):
    h = _norm_mod(x_ref[...], g_ref[...], sc_ref[...], sh_ref[...])
    scores = _sigmoid(_mm3(h, rw_ref[...]))
    sel = scores + rb_ref[...]
    lane = lax.broadcasted_iota(jnp.int32, sel.shape, 1)
    lane_f = lane.astype(f32)
    grp = lane // EXPERTS_PER_GROUP
    neg = -jnp.inf
    best_g = jnp.zeros(sel.shape[:1] + (1,), jnp.int32)
    best_v = None
    for gi in range(N_GROUPS):
        mg = jnp.where(grp == gi, sel, neg)
        m1 = _lane_max(mg)
        i1 = _first_at(mg, m1, lane_f)
        m2 = _lane_max(jnp.where(lane_f == i1, neg, mg))
        gs = m1 + m2
        if gi == 0:
            best_v = gs
        else:
            upd = gs > best_v
            best_g = jnp.where(upd, gi, best_g)
            best_v = jnp.where(upd, gs, best_v)
    cand = jnp.where(grp == best_g, sel, neg)
    v1 = _lane_max(cand)
    i1 = _first_at(cand, v1, lane_f)
    cand2 = jnp.where(lane_f == i1, neg, cand)
    v2 = _lane_max(cand2)
    i2 = _first_at(cand2, v2, lane_f)
    picked = (lane_f == i1) | (lane_f == i2)
    w = jnp.where(picked, scores, 0.0)
    comb = w / jnp.sum(w, axis=-1, keepdims=True)
    rt_ref[...] = jnp.where(lane == ROUTE_GID, best_g.astype(f32), comb)


def _route(x, g, sc, sh, router_w, router_b):
    bsz, t, _ = x.shape
    tm = min(t, 512)
    pad = ROUTE_W - N_EXPERTS
    rw = jnp.pad(router_w, ((0, 0), (0, pad)))
    rb = jnp.pad(router_b.reshape(1, N_EXPERTS), ((0, 0), (0, pad)))
    return pl.pallas_call(
        _route_kernel,
        grid=(bsz, t // tm),
        in_specs=[
            pl.BlockSpec((None, tm, D_MODEL), lambda b, i: (b, i, 0)),
            pl.BlockSpec((1, D_MODEL), lambda b, i: (0, 0)),
            pl.BlockSpec((None, 1, D_MODEL), lambda b, i: (b, 0, 0)),
            pl.BlockSpec((None, 1, D_MODEL), lambda b, i: (b, 0, 0)),
            pl.BlockSpec((D_MODEL, ROUTE_W), lambda b, i: (0, 0)),
            pl.BlockSpec((1, ROUTE_W), lambda b, i: (0, 0)),
        ],
        out_specs=pl.BlockSpec((None, tm, ROUTE_W), lambda b, i: (b, i, 0)),
        out_shape=jax.ShapeDtypeStruct((bsz, t, ROUTE_W), f32),
        compiler_params=pltpu.CompilerParams(
            dimension_semantics=("parallel", "parallel"), vmem_limit_bytes=VMEM_LIMIT),
        name="route",
    )(x, g, sc, sh, rw, rb)


def _moe_kernel(meta_ref, x_ref, g_ref, sc_ref, sh_ref, g2_ref, rt_ref, fg_ref, wg_ref, wu_ref, wd_ref, o_ref,
                xs_scr, acc_scr, cs_scr, pos_scr, *, final_norm, tm, nt):
    sb = MOE_SB
    n_rows = tm + N_GROUPS * sb
    e = pl.program_id(2)
    base = (pl.program_id(0) * nt + pl.program_id(1)) * (2 * N_GROUPS)

    @pl.when(e == 0)
    def _():
        h = _norm_mod(x_ref[...], g_ref[...], sc_ref[...], sh_ref[...]).astype(bf16)
        rt = rt_ref[...]
        lane = lax.broadcasted_iota(jnp.int32, rt.shape, 1)
        gid = jnp.sum(jnp.where(lane == ROUTE_GID, rt, 0.0), axis=-1, keepdims=True)
        g8 = lax.broadcasted_iota(jnp.int32, (tm, 8), 1)
        onehot = gid == g8.astype(f32)
        ii = lax.broadcasted_iota(jnp.int32, (tm, tm), 0)
        jj = lax.broadcasted_iota(jnp.int32, (tm, tm), 1)
        rank = _dot((jj <= ii).astype(bf16), onehot.astype(bf16))
        start = jnp.zeros((1, 8), f32)
        g8r = lax.broadcasted_iota(jnp.int32, (1, 8), 1)
        for gi in range(N_GROUPS):
            start = jnp.where(g8r == gi, meta_ref[base + gi].astype(f32), start)
        pos = jnp.sum(jnp.where(onehot, (start + rank) - 1.0, 0.0), axis=-1, keepdims=True)
        posmat = jnp.broadcast_to(pos, (tm, 8))
        pos_scr[...] = posmat
        e_i = lax.broadcasted_iota(jnp.int32, (8, 8), 0)
        e_j = lax.broadcasted_iota(jnp.int32, (8, 8), 1)
        pos_row = _mm_exact_lhs((e_i == e_j).astype(bf16), posmat, _NT)[0:1, :]
        hi = rt.astype(bf16)
        r1 = rt - hi.astype(f32)
        mid = r1.astype(bf16)
        lo = (r1 - mid.astype(f32)).astype(bf16)
        rt3 = jnp.concatenate([hi, mid, lo], axis=1)
        for c in range(n_rows // sb):
            rid = (lax.broadcasted_iota(jnp.int32, (sb, 1), 0) + c * sb).astype(f32)
            perm = (rid == pos_row).astype(bf16)
            xs_scr[c * sb:(c + 1) * sb, :] = _dot(perm, h).astype(bf16)
            cc = _dot(perm, rt3)
            cs_scr[c * sb:(c + 1) * sb, :] = cc[:, :ROUTE_W] + (cc[:, ROUTE_W:2 * ROUTE_W] + cc[:, 2 * ROUTE_W:])
        acc_scr[...] = jnp.zeros_like(acc_scr)

    grp = e // EXPERTS_PER_GROUP
    seg_start = meta_ref[base + grp]
    seg_blocks = meta_ref[base + N_GROUPS + grp]
    wg = wg_ref[...].astype(bf16)
    wu = wu_ref[...].astype(bf16)
    wd = wd_ref[...].astype(bf16)
    lane_sb = lax.broadcasted_iota(jnp.int32, (sb, ROUTE_W), 1)

    def block(kb, carry):
        rows = pl.ds(pl.multiple_of(seg_start + kb * sb, sb), sb)
        xs = xs_scr[rows, :]
        gate = _dot(xs, wg)
        hid = (gate * _sigmoid(gate)) * _dot(xs, wu)
        ce = jnp.sum(jnp.where(lane_sb == e, cs_scr[rows, :], 0.0), axis=-1, keepdims=True)
        acc_scr[rows, :] += _dot((ce * hid).astype(bf16), wd)
        return carry

    lax.fori_loop(0, seg_blocks, block, 0)

    @pl.when(e == N_EXPERTS - 1)
    def _():
        xs_scr[...] = acc_scr[...].astype(bf16)
        cid = lax.broadcasted_iota(jnp.int32, (1, n_rows), 1).astype(f32)
        for c in range(tm // sb):
            rows = slice(c * sb, (c + 1) * sb)
            unperm = (pos_scr[rows, 0:1] == cid).astype(bf16)
            y = x_ref[rows, :] + g2_ref[...] * _dot(unperm, xs_scr[...])
            if final_norm:
                y = y * lax.rsqrt(jnp.mean(y * y, axis=-1, keepdims=True) + NORM_EPS) * fg_ref[...]
            o_ref[rows, :] = y


def _moe(x, g, sc, sh, g2, rt, final_g, final_norm, w_gate, w_up, w_down, l):
    bsz, t, _ = x.shape
    tm = min(t, 1024)
    nt = t // tm
    sb = MOE_SB
    n_rows = tm + N_GROUPS * sb
    gid = rt[..., ROUTE_GID].astype(jnp.int32).reshape(bsz, nt, tm)
    cnt = jnp.sum(gid[..., None] == jnp.arange(N_GROUPS), axis=2).astype(jnp.int32)
    nblk = (cnt + (sb - 1)) // sb
    start = (jnp.cumsum(nblk, axis=-1) - nblk) * sb
    meta = jnp.concatenate([start, nblk], axis=-1).reshape(-1).astype(jnp.int32)
    grid_spec = pltpu.PrefetchScalarGridSpec(
        num_scalar_prefetch=1,
        grid=(bsz, nt, N_EXPERTS),
        in_specs=[
            pl.BlockSpec((None, tm, D_MODEL), lambda b, i, e, m: (b, i, 0)),
            pl.BlockSpec((1, D_MODEL), lambda b, i, e, m: (0, 0)),
            pl.BlockSpec((None, 1, D_MODEL), lambda b, i, e, m: (b, 0, 0)),
            pl.BlockSpec((None, 1, D_MODEL), lambda b, i, e, m: (b, 0, 0)),
            pl.BlockSpec((None, 1, D_MODEL), lambda b, i, e, m: (b, 0, 0)),
            pl.BlockSpec((None, tm, ROUTE_W), lambda b, i, e, m: (b, i, 0)),
            pl.BlockSpec((1, D_MODEL), lambda b, i, e, m: (0, 0)),
            pl.BlockSpec((None, None, D_MODEL, D_EXPERT), lambda b, i, e, m: (l, e, 0, 0)),
            pl.BlockSpec((None, None, D_MODEL, D_EXPERT), lambda b, i, e, m: (l, e, 0, 0)),
            pl.BlockSpec((None, None, D_EXPERT, D_MODEL), lambda b, i, e, m: (l, e, 0, 0)),
        ],
        out_specs=pl.BlockSpec((None, tm, D_MODEL), lambda b, i, e, m: (b, i, 0)),
        scratch_shapes=[pltpu.VMEM((n_rows, D_MODEL), bf16), pltpu.VMEM((n_rows, D_MODEL), f32),
                        pltpu.VMEM((n_rows, ROUTE_W), f32), pltpu.VMEM((tm, 8), f32)],
    )
    return pl.pallas_call(
        functools.partial(_moe_kernel, final_norm=final_norm, tm=tm, nt=nt),
        grid_spec=grid_spec,
        out_shape=jax.ShapeDtypeStruct((bsz, t, D_MODEL), f32),
        compiler_params=pltpu.CompilerParams(
            dimension_semantics=("parallel", "parallel", "arbitrary"), vmem_limit_bytes=MOE_VMEM_LIMIT),
        name="moe",
    )(meta, x, g, sc, sh, g2, rt, final_g, w_gate, w_up, w_down)


def _pack_w_in(w_in):
    ml, rw, gt = w_in[:, :ML_COLS], w_in[:, ML_COLS:ML_COLS + RW_COLS], w_in[:, ML_COLS + RW_COLS:]
    qkvo, mlg = ml[:, :2 * ML_QK + 2 * ML_V], ml[:, 2 * ML_QK + 2 * ML_V:]
    z = lambda n: jnp.zeros((D_MODEL, n), w_in.dtype)
    return jnp.concatenate(
        [gt, rw, z(P_RW_PAD - RW_COLS), qkvo, mlg, z(P_MLG_PAD - 4 * ML_HEADS)], axis=1).astype(bf16)


def _mixer(pfull, is_ctx, lp, ml_state, rw_state):
    hs, ml_fin = [], []
    for d in range(2):
        h_d, c_fin = _mlstm(pfull, lp["taps"], lp["conv_b"], lp["gate_b"], ml_state[d], rev=bool(d), d=d,
                            grid_conv=not is_ctx)
        hs.append(h_d)
        ml_fin.append(c_fin)
    yf, bvf, g, yb, bvb, sff, sfb = _rwkv(pfull, rw_state[0], rw_state[1], lp["rw_mu"], lp["rw_w_up"],
                                          lp["rw_w0"], lp["rw_a_up"], lp["rw_a0"], lp["rw_k_k"], lp["rw_k_a"],
                                          lp["rw_r_k"], lp["ones_bd"], lp["gup"])
    return (hs[0], hs[1], yf, yb, bvf, bvb, g), ml_fin, [sff, sfb]


def kernel(x, c, ctx, c_ctx, w_ada, b_ada, norm1_g, norm2_g, w_in, ml_conv_k, ml_conv_b, ml_gate_b, ml_norm_g, rw_mu, rw_w_up, rw_w0, rw_a_up, rw_a0, rw_g_up, rw_k_k, rw_k_a, rw_r_k, rw_ln_w, rw_ln_b, merge_pa, merge_pb, w_out, router_w, router_b, exp_w_gate, exp_w_up, exp_w_down, final_g):
    bsz = x.shape[0]
    s_rows = jnp.zeros((8, D_MODEL), f32).at[:bsz].set(c).at[bsz].set(c_ctx)
    mod = _ada(s_rows, w_ada, b_ada)
    head_id = jnp.arange(RW_W) // RW_N
    ones_bd = (head_id[:, None] == head_id[None, :]).astype(bf16)
    row = lambda v: v.reshape(1, -1)

    x_lat, x_ctx = x, ctx
    for l in range(DEPTH):
        last = l == DEPTH - 1
        mod_lat = mod[l, :bsz].reshape(bsz, 1, N_MOD, D_MODEL)
        mod_ctx = jnp.broadcast_to(mod[l, bsz].reshape(1, 1, N_MOD, D_MODEL), (bsz, 1, N_MOD, D_MODEL))
        lp = dict(taps=ml_conv_k[l].reshape(9, 2 * ML_QK), conv_b=row(ml_conv_b[l]), gate_b=row(ml_gate_b[l]),
                  rw_mu=jnp.pad(row(rw_mu[l]), ((0, 0), (0, P_RW_PAD - RW_COLS))),
                  rw_w_up=rw_w_up[l].astype(bf16), rw_w0=rw_w0[l], rw_a_up=rw_a_up[l].astype(bf16),
                  rw_a0=rw_a0[l], rw_k_k=row(rw_k_k[l]), rw_k_a=row(rw_k_a[l]), rw_r_k=row(rw_r_k[l]),
                  ones_bd=ones_bd, gup=rw_g_up[l].astype(bf16))
        w_in_bf = _pack_w_in(w_in[l])
        pa_bf, pb_bf, wo_bf = merge_pa[l].astype(bf16), merge_pb[l].astype(bf16), w_out[l].astype(bf16)
        experts = (exp_w_gate, exp_w_up, exp_w_down, l)
        g1n, g2n = row(norm1_g[l]), row(norm2_g[l])
        readout = (row(ml_norm_g[l]), row(rw_ln_w[l]), row(rw_ln_b[l]), ones_bd, pa_bf, pb_bf, wo_bf)

        def m(modv, i):
            return modv[:, :, i]

        p_ctx = _proj(x_ctx, g1n, m(mod_ctx, 1), m(mod_ctx, 0), w_in_bf)
        ml0 = [jnp.zeros((bsz, ML_HEADS, ML_DQK, 2 * ML_DV), f32)] * 2
        rw0 = [jnp.zeros((bsz, RW_HEADS, RW_N, RW_N), f32)] * 2
        mix_c, ml_st, rw_st = _mixer(p_ctx, True, lp, ml0, rw0)

        p_lat = _proj(x_lat, g1n, m(mod_lat, 1), m(mod_lat, 0), w_in_bf)
        mix_l, _, _ = _mixer(p_lat, False, lp, ml_st, rw_st)
        x_lat = _merge(x_lat, *mix_l, p_lat, m(mod_lat, 2), *readout)
        comb = _route(x_lat, g2n, m(mod_lat, 4), m(mod_lat, 3), router_w, router_b)
        x_lat = _moe(x_lat, g2n, m(mod_lat, 4), m(mod_lat, 3), m(mod_lat, 5), comb, row(final_g), last, *experts)
        if not last:
            x_ctx = _merge(x_ctx, *mix_c, p_ctx, m(mod_ctx, 2), *readout)
            comb_c = _route(x_ctx, g2n, m(mod_ctx, 4), m(mod_ctx, 3), router_w, router_b)
            x_ctx = _moe(x_ctx, g2n, m(mod_ctx, 4), m(mod_ctx, 3), m(mod_ctx, 5), comb_c, row(final_g), False,
                         *experts)
    return x_lat
```

```python
import functools

import jax
import jax.numpy as jnp
import numpy as np
from jax import lax
from jax.experimental import pallas as pl
from jax.experimental.pallas import tpu as pltpu

f32 = jnp.float32
bf16 = jnp.bfloat16

D_MODEL = 1024
DEPTH = 2
GRID_W = 64
N_MOD = 6
NORM_EPS = 1e-6

ML_HEADS = 4
ML_DQK = 64
ML_DV = 128
ML_QK = ML_HEADS * ML_DQK
ML_V = ML_HEADS * ML_DV
GATE_CAP = 15.0
ML_COLS = 2 * ML_QK + 2 * ML_V + 4 * ML_HEADS
ML_CHUNK = 256

RW_HEADS = 8
RW_N = 64
RW_W = RW_HEADS * RW_N
RW_DECAY_LORA = 64
RW_A_LORA = 64
RW_GATE_LORA = 128
RW_GN_EPS = 6.4e-4
RW_COLS = 3 * RW_W + 2 * RW_DECAY_LORA + 2 * RW_A_LORA + RW_GATE_LORA
RW_CHUNK = 64
RW_BLOCK = 256
RW_HALO = 16

N_EXPERTS = 16
N_GROUPS = 4
EXPERTS_PER_GROUP = N_EXPERTS // N_GROUPS
D_EXPERT = 512

P_GA = 0
P_GB = D_MODEL
P_RW = 2 * D_MODEL
P_RW_PAD = 2048
P_QK = P_RW + P_RW_PAD
P_V = P_QK + 2 * ML_QK
P_O = P_V + ML_V
P_MLG = P_O + ML_V
P_MLG_PAD = 512
P_MLG_BLK = 128
P_COLS = P_MLG + P_MLG_PAD
PROJ_TN = 1024

VMEM_LIMIT = 48 * 1024 * 1024
MOE_VMEM_LIMIT = 58 * 1024 * 1024
ROUTE_W = 32
ROUTE_GID = N_EXPERTS
MOE_SB = 256
EXP_NEG_HALF = float(np.exp(-0.5))
NEG_BIG = -1e30


_NN = ((1,), (0,))
_NT = ((1,), (1,))
_TN = ((0,), (0,))


def _dot(a, b, dims=_NN):
    return lax.dot_general(a, b, (dims, ((), ())), preferred_element_type=f32)


def _mm(a, b, dims=_NN):
    return _dot(a.astype(bf16), b.astype(bf16), dims)


def _hi_lo(x):
    hi = x.astype(bf16)
    lo = (x - hi.astype(f32)).astype(bf16)
    return hi, lo


def _mm3(a, b, dims=_NN):
    ah, al = _hi_lo(a)
    bh, bl = _hi_lo(b)
    return _dot(ah, bh, dims) + (_dot(ah, bl, dims) + _dot(al, bh, dims))


def _mm_exact_lhs(a_bf, b, dims=_NN):
    hi = b.astype(bf16)
    r1 = b - hi.astype(f32)
    mid = r1.astype(bf16)
    lo = (r1 - mid.astype(f32)).astype(bf16)
    return _dot(a_bf, hi, dims) + (_dot(a_bf, mid, dims) + _dot(a_bf, lo, dims))


def _head_sum(a, ones_bd):
    hi, lo = _hi_lo(a)
    return _dot(hi, ones_bd) + _dot(lo, ones_bd)


def _sigmoid(x):
    return 1.0 / (1.0 + jnp.exp(-x))


def _norm_mod(x, g, sc, sh):
    y = x * lax.rsqrt(jnp.mean(x * x, axis=-1, keepdims=True) + NORM_EPS)
    return (y * g) * (1.0 + sc) + sh


def _ada_kernel(s_ref, w_ref, b_ref, o_ref):
    s = s_ref[...]
    s = s * _sigmoid(s)
    o_ref[...] = _mm3(s, w_ref[...]) + b_ref[...]


def _ada(s_rows, w_ada, b_ada):
    tn = 1536
    n = N_MOD * D_MODEL
    return pl.pallas_call(
        _ada_kernel,
        grid=(DEPTH, n // tn),
        in_specs=[
            pl.BlockSpec((8, D_MODEL), lambda l, j: (0, 0)),
            pl.BlockSpec((None, D_MODEL, tn), lambda l, j: (l, 0, j)),
            pl.BlockSpec((None, 1, tn), lambda l, j: (l, 0, j)),
        ],
        out_specs=pl.BlockSpec((None, 8, tn), lambda l, j: (l, 0, j)),
        out_shape=jax.ShapeDtypeStruct((DEPTH, 8, n), f32),
        compiler_params=pltpu.CompilerParams(
            dimension_semantics=("arbitrary", "arbitrary"), vmem_limit_bytes=VMEM_LIMIT),
        name="ada",
    )(s_rows, w_ada, b_ada.reshape(DEPTH, 1, n))


def _proj_kernel(x_ref, g_ref, sc_ref, sh_ref, w_ref, o_ref, h_scr):
    @pl.when(pl.program_id(2) == 0)
    def _():
        h_scr[...] = _norm_mod(x_ref[...], g_ref[...], sc_ref[...], sh_ref[...]).astype(bf16)

    o_ref[...] = _dot(h_scr[...], w_ref[...]).astype(bf16)


def _proj(x, g, sc, sh, w_bf):
    bsz, t, _ = x.shape
    tm = min(t, 1024)
    return pl.pallas_call(
        _proj_kernel,
        grid=(bsz, t // tm, P_COLS // PROJ_TN),
        in_specs=[
            pl.BlockSpec((None, tm, D_MODEL), lambda b, i, j: (b, i, 0)),
            pl.BlockSpec((1, D_MODEL), lambda b, i, j: (0, 0)),
            pl.BlockSpec((None, 1, D_MODEL), lambda b, i, j: (b, 0, 0)),
            pl.BlockSpec((None, 1, D_MODEL), lambda b, i, j: (b, 0, 0)),
            pl.BlockSpec((D_MODEL, PROJ_TN), lambda b, i, j: (0, j)),
        ],
        out_specs=pl.BlockSpec((None, tm, PROJ_TN), lambda b, i, j: (b, i, j)),
        out_shape=jax.ShapeDtypeStruct((bsz, t, P_COLS), bf16),
        scratch_shapes=[pltpu.VMEM((tm, D_MODEL), bf16)],
        compiler_params=pltpu.CompilerParams(
            dimension_semantics=("parallel", "parallel", "arbitrary"), vmem_limit_bytes=VMEM_LIMIT),
        name="proj",
    )(x, g, sc, sh, w_bf)


def _shift_rows(u, up_row, dn_row):
    n = u.shape[0]
    rid = lax.broadcasted_iota(jnp.int32, (n, 1), 0)
    up = jnp.where(rid == 0, up_row, pltpu.roll(u, 1, axis=0))
    dn = jnp.where(rid == n - 1, dn_row, pltpu.roll(u, n - 1, axis=0))
    return up, dn


def _log_sigmoid(x):
    return jnp.minimum(x, 0.0) - jnp.log1p(jnp.exp(-jnp.abs(x)))


def _mlstm_kernel(qk_ref, qkp_ref, qkn_ref, v_ref, mlg_ref, taps_ref, cb_ref, gb_ref, c0_ref,
                  h_ref, cfin_ref, c_scr, *, rev, d, grid_conv, nb):
    L = ML_CHUNK
    j = pl.program_id(1)
    jblk = (nb - 1 - j) if rev else j

    @pl.when(j == 0)
    def _():
        c_scr[...] = c0_ref[...]

    qk = qk_ref[...].astype(f32)
    taps = taps_ref[...]
    zero_row = jnp.zeros((1, 2 * ML_QK), f32)
    rid = lax.broadcasted_iota(jnp.int32, (L, 1), 0)
    if grid_conv:
        first_col = (rid % GRID_W) == 0
        last_col = (rid % GRID_W) == GRID_W - 1
        above = jnp.where(jblk > 0, qkp_ref[...].astype(f32), 0.0)
        below = jnp.where(jblk < nb - 1, qkn_ref[...].astype(f32), 0.0)
        ext = jnp.concatenate([above, qk, below], axis=0)
        bases = [(dr, ext[dr * GRID_W:dr * GRID_W + L]) for dr in range(3)]
    else:
        first_col = rid == 0
        last_col = rid == L - 1
        bases = [(1, qk)]
    conv = cb_ref[...]
    for dr, base in bases:
        up, dn = _shift_rows(base, zero_row, zero_row)
        conv = conv + (taps[3 * dr:3 * dr + 1] * jnp.where(first_col, 0.0, up)
                       + taps[3 * dr + 1:3 * dr + 2] * base
                       + taps[3 * dr + 2:3 * dr + 3] * jnp.where(last_col, 0.0, dn))
    q = conv[:, :ML_QK]
    k = conv[:, ML_QK:] * (ML_DQK ** -0.5)

    pre = mlg_ref[...][:, :4 * ML_HEADS].astype(f32) + gb_ref[...]
    pre = GATE_CAP * jnp.tanh(pre * (1.0 / GATE_CAP))
    ig = pre[:, d * ML_HEADS:(d + 1) * ML_HEADS]
    lf = _log_sigmoid(pre[:, (2 + d) * ML_HEADS:(3 + d) * ML_HEADS])
    ii = lax.broadcasted_iota(jnp.int32, (L, L), 0)
    jj = lax.broadcasted_iota(jnp.int32, (L, L), 1)
    incl = (jj >= ii) if rev else (jj <= ii)
    tri = incl.astype(bf16)
    bc = _mm_exact_lhs(tri, lf)
    cols = jnp.concatenate([ig, bc], axis=1)
    e_i = lax.broadcasted_iota(jnp.int32, (2 * ML_HEADS, 2 * ML_HEADS), 0)
    e_j = lax.broadcasted_iota(jnp.int32, (2 * ML_HEADS, 2 * ML_HEADS), 1)
    rows_ = _mm_exact_lhs((e_i == e_j).astype(bf16), cols, _NT)
    one_col = (lax.broadcasted_iota(jnp.int32, (L, ML_DV), 1) == 0).astype(bf16)
    v = v_ref[...]

    hs = range(ML_HEADS)
    qh = [q[:, h * ML_DQK:(h + 1) * ML_DQK] for h in hs]
    kh = [k[:, h * ML_DQK:(h + 1) * ML_DQK] for h in hs]
    vp = [jnp.concatenate([v[:, h * ML_DV:(h + 1) * ML_DV], one_col], axis=1) for h in hs]
    b_col = [bc[:, h:h + 1] for h in hs]
    btot = [(b_col[h][0:1, :] if rev else b_col[h][L - 1:L, :]) for h in hs]
    decay = [jnp.exp(jnp.where(incl, (b_col[h] - rows_[ML_HEADS + h:ML_HEADS + h + 1, :]) + rows_[h:h + 1, :],
                               NEG_BIG)) for h in hs]
    qk_s = [_mm(qh[h], kh[h], _NT) for h in hs]
    qc = [_mm(qh[h], c_scr[h]) for h in hs]
    kw = [kh[h] * jnp.exp((btot[h] - b_col[h]) + ig[:, h:h + 1]) for h in hs]
    kv = [_mm(kw[h], vp[h], _TN) for h in hs]
    sv = [_mm(qk_s[h] * decay[h], vp[h]) for h in hs]
    for h in hs:
        nd = sv[h] + jnp.exp(b_col[h]) * qc[h]
        den = nd[:, ML_DV:ML_DV + 1]
        h_ref[:, h * ML_DV:(h + 1) * ML_DV] = (nd[:, :ML_DV] / jnp.maximum(jnp.abs(den), 1.0)).astype(bf16)
        c_scr[h] = jnp.exp(btot[h]) * c_scr[h] + kv[h]
    cfin_ref[...] = c_scr[...]


def _mlstm(pfull, taps, conv_b, gate_b, c0, rev, d, grid_conv):
    bsz, t, _ = pfull.shape
    L = ML_CHUNK
    nb = t // L
    tb = (lambda j: nb - 1 - j) if rev else (lambda j: j)
    vblk = P_V // ML_V
    qkblk = P_QK // (2 * ML_QK)
    rpb = L // GRID_W
    n_rows = t // GRID_W
    return pl.pallas_call(
        functools.partial(_mlstm_kernel, rev=rev, d=d, grid_conv=grid_conv, nb=nb),
        grid=(bsz, nb),
        in_specs=[
            pl.BlockSpec((None, L, 2 * ML_QK), lambda b, j: (b, tb(j), qkblk)),
            pl.BlockSpec((None, GRID_W, 2 * ML_QK),
                         lambda b, j: (b, jnp.maximum(tb(j) * rpb - 1, 0), qkblk)),
            pl.BlockSpec((None, GRID_W, 2 * ML_QK),
                         lambda b, j: (b, jnp.minimum((tb(j) + 1) * rpb, n_rows - 1), qkblk)),
            pl.BlockSpec((None, L, ML_V), lambda b, j: (b, tb(j), vblk)),
            pl.BlockSpec((None, L, P_MLG_BLK), lambda b, j: (b, tb(j), P_MLG // P_MLG_BLK)),
            pl.BlockSpec((9, 2 * ML_QK), lambda b, j: (0, 0)),
            pl.BlockSpec((1, 2 * ML_QK), lambda b, j: (0, 0)),
            pl.BlockSpec((1, 4 * ML_HEADS), lambda b, j: (0, 0)),
            pl.BlockSpec((None, ML_HEADS, ML_DQK, 2 * ML_DV), lambda b, j: (b, 0, 0, 0)),
        ],
        out_specs=[
            pl.BlockSpec((None, L, ML_V), lambda b, j: (b, tb(j), 0)),
            pl.BlockSpec((None, ML_HEADS, ML_DQK, 2 * ML_DV), lambda b, j: (b, 0, 0, 0)),
        ],
        out_shape=[
            jax.ShapeDtypeStruct((bsz, t, ML_V), bf16),
            jax.ShapeDtypeStruct((bsz, ML_HEADS, ML_DQK, 2 * ML_DV), f32),
        ],
        scratch_shapes=[pltpu.VMEM((ML_HEADS, ML_DQK, 2 * ML_DV), f32)],
        compiler_params=pltpu.CompilerParams(
            dimension_semantics=("parallel", "arbitrary"), vmem_limit_bytes=VMEM_LIMIT),
        name="mlstm_bwd" if rev else "mlstm_fwd",
    )(pfull, pfull, pfull, pfull, pfull, taps, conv_b, gate_b, c0)


def _rw_prepare(p_ref, pp_ref, pn_ref, jblk, nb, d, par, scr, bv_ref):
    mu_ref, wup_ref, w0_ref, aup_ref, a0_ref, kk_ref, ka_ref, rk_ref, ones_ref = par
    r_scr, v_scr, kk_scr, kd_scr, be_scr, ld_scr = scr
    p = p_ref[...].astype(f32)
    up, dn = _shift_rows(p, jnp.where(jblk > 0, pp_ref[RW_HALO - 1:RW_HALO, :].astype(f32), 0.0),
                         jnp.where(jblk < nb - 1, pn_ref[0:1, :].astype(f32), 0.0))
    p = p + mu_ref[...] * (0.5 * (up + dn) - p)
    r = p[:, 0:RW_W]
    k = p[:, RW_W:2 * RW_W]
    v = p[:, 2 * RW_W:3 * RW_W]
    o_wd = 3 * RW_W + d * RW_DECAY_LORA
    o_ad = 3 * RW_W + 2 * RW_DECAY_LORA + d * RW_A_LORA
    lw = w0_ref[d:d + 1, :] + _mm(jnp.tanh(p[:, o_wd:o_wd + RW_DECAY_LORA]), wup_ref[d])
    ld_scr[...] = -EXP_NEG_HALF * _sigmoid(lw)
    a = _sigmoid(a0_ref[d:d + 1, :] + _mm(p[:, o_ad:o_ad + RW_A_LORA], aup_ref[d]))
    ones_bd = ones_ref[...]
    kkr = k * kk_ref[...]
    kk = kkr * lax.rsqrt(_head_sum(kkr * kkr, ones_bd) + 1e-12)
    kd = k * (1.0 + (a - 1.0) * ka_ref[...])
    bv_ref[...] = (_head_sum(r * kd * rk_ref[...], ones_bd) * v).astype(bf16)
    r_scr[...] = r
    v_scr[...] = v
    kk_scr[...] = kk
    kd_scr[...] = kd
    be_scr[...] = kk * a
    return p


def _rwkv_kernel(pf_ref, pfp_ref, pfn_ref, pb_ref, pbp_ref, pbn_ref, s0f_ref, s0b_ref,
                 mu_ref, wup_ref, w0_ref, aup_ref, a0_ref, kk_ref, ka_ref, rk_ref, ones_ref, gup_ref,
                 yf_ref, bvf_ref, g_ref, yb_ref, bvb_ref, sff_ref, sfb_ref,
                 stf_scr, stb_scr, *scr, nb):
    L = RW_CHUNK
    nch = RW_BLOCK // L
    j = pl.program_id(1)

    @pl.when(j == 0)
    def _():
        stf_scr[...] = s0f_ref[...]
        stb_scr[...] = s0b_ref[...]

    par = (mu_ref, wup_ref, w0_ref, aup_ref, a0_ref, kk_ref, ka_ref, rk_ref, ones_ref)
    scr_f, scr_b = scr[:6], scr[6:]
    p_f = _rw_prepare(pf_ref, pfp_ref, pfn_ref, j, nb, 0, par, scr_f, bvf_ref)
    _rw_prepare(pb_ref, pbp_ref, pbn_ref, nb - 1 - j, nb, 1, par, scr_b, bvb_ref)
    o_gd = 3 * RW_W + 2 * RW_DECAY_LORA + 2 * RW_A_LORA
    g_ref[...] = _mm(_sigmoid(p_f[:, o_gd:o_gd + RW_GATE_LORA]), gup_ref[...]).astype(bf16)

    ii = lax.broadcasted_iota(jnp.int32, (L, L), 0)
    jj = lax.broadcasted_iota(jnp.int32, (L, L), 1)
    ii2 = lax.broadcasted_iota(jnp.int32, (L, 2 * L), 0)
    jj2 = lax.broadcasted_iota(jnp.int32, (L, 2 * L), 1) & (L - 1)
    masks = []
    for rev in (False, True):
        incl = (jj >= ii) if rev else (jj <= ii)
        strict = (jj > ii) if rev else (jj < ii)
        incl2 = (jj2 >= ii2) if rev else (jj2 <= ii2)
        masks.append((incl2, strict, incl.astype(bf16)))
    streams = ((scr_f, stf_scr, yf_ref, False), (scr_b, stb_scr, yb_ref, True))

    def chunk(i, carry):
        units = []
        rows_of = []
        for si, (sc, st_scr, y_ref, rev) in enumerate(streams):
            r_scr, v_scr, kk_scr, kd_scr, be_scr, ld_scr = sc
            incl2, strict, tri = masks[si]
            c = (nch - 1 - i) if rev else i
            rows = pl.ds(pl.multiple_of(c * L, L), L)
            rows_of.append(rows)
            ld = ld_scr[rows, :]
            b = _mm_exact_lhs(tri, ld)
            btot = b[0:1, :] if rev else b[L - 1:L, :]
            e_nb = jnp.exp(-b)
            e_end = jnp.exp(btot - b)
            kd_c = kd_scr[rows, :]
            be_c = be_scr[rows, :]
            v_c = v_scr[rows, :]
            aw = -kk_scr[rows, :] * jnp.exp(b - ld)
            rw = r_scr[rows, :] * jnp.exp(b)
            bi = be_c * e_nb
            ki = kd_c * e_nb
            bw = be_c * e_end
            kw = kd_c * e_end
            wend = jnp.exp(btot)
            for h in range(RW_HEADS):
                c_ = slice(h * RW_N, (h + 1) * RW_N)
                units.append(dict(si=si, h=h, incl2=incl2, strict=strict, st=st_scr, aw=aw[:, c_], rw=rw[:, c_],
                                  bi=bi[:, c_], ki=ki[:, c_], bw=bw[:, c_], kw=kw[:, c_], v=v_c[:, c_],
                                  wend=wend[:, c_]))
        us = range(len(units))
        m = [_mm(jnp.concatenate([u["aw"], u["rw"]], axis=0),
                 jnp.concatenate([u["bi"], u["ki"]], axis=0), _NT) for u in units]
        pw = [jnp.where(units[n]["strict"], m[n][:L, :L], 0.0) for n in us]
        mak = [jnp.where(units[n]["strict"], m[n][:L, L:], 0.0) for n in us]
        mr = [jnp.where(units[n]["incl2"], m[n][L:, :], 0.0) for n in us]
        mv = [_mm(mak[n], units[n]["v"]) for n in us]
        x = [jnp.concatenate([units[n]["aw"], mv[n]], axis=1) for n in us]
        for it in range(6):
            if it < 5:
                pr = [_mm(pw[n], jnp.concatenate([x[n], pw[n]], axis=1)) for n in us]
                pw = [pr[n][:, 2 * RW_N:] for n in us]
                x = [x[n] + pr[n][:, :2 * RW_N] for n in us]
            else:
                pr = [_mm(pw[n], x[n]) for n in us]
                x = [x[n] + pr[n] for n in us]
        zero = jnp.zeros((L, RW_N), f32)
        ray = [_mm(mr[n], jnp.concatenate(
            [x[n], jnp.concatenate([zero, units[n]["v"]], axis=1)], axis=0)) for n in us]
        gt = [_mm(x[n], units[n]["bw"], _TN) for n in us]
        vk = [_mm(units[n]["v"], units[n]["kw"], _TN) for n in us]
        s_old = [units[n]["st"][units[n]["h"]] for n in us]
        yy = [_mm(units[n]["rw"] + ray[n][:, :RW_N], s_old[n], _NT) for n in us]
        sg = [_mm(s_old[n], gt[n][:RW_N]) for n in us]
        for si, (sc, st_scr, y_ref, rev) in enumerate(streams):
            mine = [n for n in us if units[n]["si"] == si]
            y_ref[rows_of[si], :] = jnp.concatenate(
                [yy[n] + ray[n][:, RW_N:] for n in mine], axis=1).astype(bf16)
            for n in mine:
                st_scr[units[n]["h"]] = (s_old[n] * units[n]["wend"] + sg[n]) + (gt[n][RW_N:] + vk[n])
        return carry

    lax.fori_loop(0, nch, chunk, 0)
    sff_ref[...] = stf_scr[...]
    sfb_ref[...] = stb_scr[...]


def _rwkv(pfull, s0f, s0b, mu, wup, w0, aup, a0, k_k, k_a, r_k, ones_bd, gup):
    bsz, t, _ = pfull.shape
    tb_ = RW_BLOCK
    nb = t // tb_
    cblk = P_RW // P_RW_PAD
    hpb = tb_ // RW_HALO
    nh = t // RW_HALO
    fwd = lambda j: j
    bwd = lambda j: nb - 1 - j

    def p_specs(tb):
        return [
            pl.BlockSpec((None, tb_, P_RW_PAD), lambda b, j: (b, tb(j), cblk)),
            pl.BlockSpec((None, RW_HALO, P_RW_PAD), lambda b, j: (b, jnp.maximum(tb(j) * hpb - 1, 0), cblk)),
            pl.BlockSpec((None, RW_HALO, P_RW_PAD),
                         lambda b, j: (b, jnp.minimum((tb(j) + 1) * hpb, nh - 1), cblk)),
        ]

    full = lambda shape: pl.BlockSpec(shape, lambda b, j: (0,) * len(shape))
    st_spec = pl.BlockSpec((None, RW_HEADS, RW_N, RW_N), lambda b, j: (b, 0, 0, 0))
    tok = lambda tb: pl.BlockSpec((None, tb_, RW_W), lambda b, j: (b, tb(j), 0))
    tok_shape = jax.ShapeDtypeStruct((bsz, t, RW_W), bf16)
    st_shape = jax.ShapeDtypeStruct((bsz, RW_HEADS, RW_N, RW_N), f32)
    return pl.pallas_call(
        functools.partial(_rwkv_kernel, nb=nb),
        grid=(bsz, nb),
        in_specs=p_specs(fwd) + p_specs(bwd) + [
            st_spec, st_spec,
            full((1, P_RW_PAD)),
            full((2, RW_DECAY_LORA, RW_W)), full((2, RW_W)),
            full((2, RW_A_LORA, RW_W)), full((2, RW_W)),
            full((1, RW_W)), full((1, RW_W)), full((1, RW_W)),
            full((RW_W, RW_W)),
            full((RW_GATE_LORA, RW_W)),
        ],
        out_specs=[tok(fwd), tok(fwd), tok(fwd), tok(bwd), tok(bwd), st_spec, st_spec],
        out_shape=[tok_shape] * 5 + [st_shape] * 2,
        scratch_shapes=[pltpu.VMEM((RW_HEADS, RW_N, RW_N), f32) for _ in range(2)]
        + [pltpu.VMEM((tb_, RW_W), f32) for _ in range(12)],
        compiler_params=pltpu.CompilerParams(
            dimension_semantics=("parallel", "arbitrary"), vmem_limit_bytes=VMEM_LIMIT),
        name="rwkv",
    )(pfull, pfull, pfull, pfull, pfull, pfull, s0f, s0b, mu, wup, w0, aup, a0, k_k, k_a, r_k, ones_bd, gup)


def _merge_kernel(x_ref, hf_ref, hb_ref, o_ref_, yf_ref, yb_ref, bvf_ref, bvb_ref, g_ref, ga_ref, gb_ref,
                  g1_ref, mlg_ref, lnw_ref, lnb_ref, ones_ref, pa_ref, pb_ref, wo_ref, out_ref):
    hsum = hf_ref[...].astype(f32) + hb_ref[...].astype(f32)
    parts = []
    for h in range(ML_HEADS):
        hh = hsum[:, h * ML_DV:(h + 1) * ML_DV]
        parts.append(hh * lax.rsqrt(jnp.mean(hh * hh, axis=-1, keepdims=True) + NORM_EPS))
    a_lat = jnp.concatenate(parts, axis=1) * mlg_ref[...] * _sigmoid(o_ref_[...].astype(f32))
    y = yf_ref[...].astype(f32) + yb_ref[...].astype(f32)
    ones_bd = ones_ref[...]
    inv_n = 1.0 / RW_N
    mu = _head_sum(y, ones_bd) * inv_n
    dev = y - mu
    var = _head_sum(dev * dev, ones_bd) * inv_n
    yn = dev * lax.rsqrt(var + RW_GN_EPS) * lnw_ref[...] + lnb_ref[...]
    b_lat = (yn + (bvf_ref[...].astype(f32) + bvb_ref[...].astype(f32))) * g_ref[...].astype(f32)
    m = (_sigmoid(ga_ref[...].astype(f32)) * _mm(a_lat, pa_ref[...])
         + _sigmoid(gb_ref[...].astype(f32)) * _mm(b_lat, pb_ref[...]))
    out_ref[...] = x_ref[...] + g1_ref[...] * _mm(m, wo_ref[...])


def _merge(x, hf, hb, yf, yb, bvf, bvb, g, pfull, g1, ml_norm_g, ln_w, ln_b, ones_bd, pa_bf, pb_bf, wo_bf):
    bsz, t, _ = x.shape
    tm = min(t, 512)
    full = lambda shape: pl.BlockSpec(shape, lambda b, i: (0,) * len(shape))
    tok = lambda w: pl.BlockSpec((None, tm, w), lambda b, i: (b, i, 0))
    col = lambda w, off: pl.BlockSpec((None, tm, w), lambda b, i: (b, i, off // w))
    return pl.pallas_call(
        _merge_kernel,
        grid=(bsz, t // tm),
        in_specs=[
            tok(D_MODEL), tok(ML_V), tok(ML_V), col(ML_V, P_O),
            tok(RW_W), tok(RW_W), tok(RW_W), tok(RW_W), tok(RW_W),
            col(D_MODEL, P_GA), col(D_MODEL, P_GB),
            pl.BlockSpec((None, 1, D_MODEL), lambda b, i: (b, 0, 0)),
            full((1, ML_V)), full((1, RW_W)), full((1, RW_W)), full((RW_W, RW_W)),
            full((ML_V, D_MODEL)), full((RW_W, D_MODEL)), full((D_MODEL, D_MODEL)),
        ],
        out_specs=tok(D_MODEL),
        out_shape=jax.ShapeDtypeStruct((bsz, t, D_MODEL), f32),
        compiler_params=pltpu.CompilerParams(
            dimension_semantics=("parallel", "parallel"), vmem_limit_bytes=VMEM_LIMIT),
        name="merge",
    )(x, hf, hb, pfull, yf, yb, bvf, bvb, g, pfull, pfull, g1, ml_norm_g, ln_w, ln_b, ones_bd,
      pa_bf, pb_bf, wo_bf)


def _lane_max(x):
    return jnp.max(x, axis=-1, keepdims=True)


def _first_at(x, val, lane_f):
    return jnp.min(jnp.where(x == val, lane_f, float(ROUTE_W)), axis=-1, keepdims=True)


def _route_kernel(x_ref, g_ref, sc_ref, sh_ref, rw_ref, rb_ref, rt_ref):
    h = _norm_mod(x_ref[...], g_ref[...], sc_ref[...], sh_ref[...])
    scores = _sigmoid(_mm3(h, rw_ref[...]))
    sel = scores + rb_ref[...]
    lane = lax.broadcasted_iota(jnp.int32, sel.shape, 1)
    lane_f = lane.astype(f32)
    grp = lane // EXPERTS_PER_GROUP
    neg = -jnp.inf
    best_g = jnp.zeros(sel.shape[:1] + (1,), jnp.int32)
    best_v = None
    for gi in range(N_GROUPS):
        mg = jnp.where(grp == gi, sel, neg)
        m1 = _lane_max(mg)
        i1 = _first_at(mg, m1, lane_f)
        m2 = _lane_max(jnp.where(lane_f == i1, neg, mg))
        gs = m1 + m2
        if gi == 0:
            best_v = gs
        else:
            upd = gs > best_v
            best_g = jnp.where(upd, gi, best_g)
            best_v = jnp.where(upd, gs, best_v)
    cand = jnp.where(grp == best_g, sel, neg)
    v1 = _lane_max(cand)
    i1 = _first_at(cand, v1, lane_f)
    cand2 = jnp.where(lane_f == i1, neg, cand)
    v2 = _lane_max(cand2)
    i2 = _first_at(cand2, v2, lane_f)
    picked = (lane_f == i1) | (lane_f == i2)
    w = jnp.where(picked, scores, 0.0)
    comb = w / jnp.sum(w, axis=-1, keepdims=True)
    rt_ref[...] = jnp.where(lane == ROUTE_GID, best_g.astype(f32), comb)


def _route(x, g, sc, sh, router_w, router_b):
    bsz, t, _ = x.shape
    tm = min(t, 512)
    pad = ROUTE_W - N_EXPERTS
    rw = jnp.pad(router_w, ((0, 0), (0, pad)))
    rb = jnp.pad(router_b.reshape(1, N_EXPERTS), ((0, 0), (0, pad)))
    return pl.pallas_call(
        _route_kernel,
        grid=(bsz, t // tm),
        in_specs=[
            pl.BlockSpec((None, tm, D_MODEL), lambda b, i: (b, i, 0)),
            pl.BlockSpec((1, D_MODEL), lambda b, i: (0, 0)),
            pl.BlockSpec((None, 1, D_MODEL), lambda b, i: (b, 0, 0)),
            pl.BlockSpec((None, 1, D_MODEL), lambda b, i: (b, 0, 0)),
            pl.BlockSpec((D_MODEL, ROUTE_W), lambda b, i: (0, 0)),
            pl.BlockSpec((1, ROUTE_W), lambda b, i: (0, 0)),
        ],
        out_specs=pl.BlockSpec((None, tm, ROUTE_W), lambda b, i: (b, i, 0)),
        out_shape=jax.ShapeDtypeStruct((bsz, t, ROUTE_W), f32),
        compiler_params=pltpu.CompilerParams(
            dimension_semantics=("parallel", "parallel"), vmem_limit_bytes=VMEM_LIMIT),
        name="route",
    )(x, g, sc, sh, rw, rb)


def _moe_kernel(meta_ref, x_ref, g_ref, sc_ref, sh_ref, g2_ref, rt_ref, fg_ref, wg_ref, wu_ref, wd_ref, o_ref,
                xs_scr, acc_scr, cs_scr, pos_scr, *, final_norm, tm, nt):
    sb = MOE_SB
    n_rows = tm + N_GROUPS * sb
    e = pl.program_id(2)
    base = (pl.program_id(0) * nt + pl.program_id(1)) * (2 * N_GROUPS)

    @pl.when(e == 0)
    def _():
        h = _norm_mod(x_ref[...], g_ref[...], sc_ref[...], sh_ref[...]).astype(bf16)
        rt = rt_ref[...]
        lane = lax.broadcasted_iota(jnp.int32, rt.shape, 1)
        gid = jnp.sum(jnp.where(lane == ROUTE_GID, rt, 0.0), axis=-1, keepdims=True)
        g8 = lax.broadcasted_iota(jnp.int32, (tm, 8), 1)
        onehot = gid == g8.astype(f32)
        ii = lax.broadcasted_iota(jnp.int32, (tm, tm), 0)
        jj = lax.broadcasted_iota(jnp.int32, (tm, tm), 1)
        rank = _dot((jj <= ii).astype(bf16), onehot.astype(bf16))
        start = jnp.zeros((1, 8), f32)
        g8r = lax.broadcasted_iota(jnp.int32, (1, 8), 1)
        for gi in range(N_GROUPS):
            start = jnp.where(g8r == gi, meta_ref[base + gi].astype(f32), start)
        pos = jnp.sum(jnp.where(onehot, (start + rank) - 1.0, 0.0), axis=-1, keepdims=True)
        posmat = jnp.broadcast_to(pos, (tm, 8))
        pos_scr[...] = posmat
        e_i = lax.broadcasted_iota(jnp.int32, (8, 8), 0)
        e_j = lax.broadcasted_iota(jnp.int32, (8, 8), 1)
        pos_row = _mm_exact_lhs((e_i == e_j).astype(bf16), posmat, _NT)[0:1, :]
        hi = rt.astype(bf16)
        r1 = rt - hi.astype(f32)
        mid = r1.astype(bf16)
        lo = (r1 - mid.astype(f32)).astype(bf16)
        rt3 = jnp.concatenate([hi, mid, lo], axis=1)
        for c in range(n_rows // sb):
            rid = (lax.broadcasted_iota(jnp.int32, (sb, 1), 0) + c * sb).astype(f32)
            perm = (rid == pos_row).astype(bf16)
            xs_scr[c * sb:(c + 1) * sb, :] = _dot(perm, h).astype(bf16)
            cc = _dot(perm, rt3)
            cs_scr[c * sb:(c + 1) * sb, :] = cc[:, :ROUTE_W] + (cc[:, ROUTE_W:2 * ROUTE_W] + cc[:, 2 * ROUTE_W:])
        acc_scr[...] = jnp.zeros_like(acc_scr)

    grp = e // EXPERTS_PER_GROUP
    seg_start = meta_ref[base + grp]
    seg_blocks = meta_ref[base + N_GROUPS + grp]
    wg = wg_ref[...]
    wu = wu_ref[...]
    wd = wd_ref[...]
    lane_sb = lax.broadcasted_iota(jnp.int32, (sb, ROUTE_W), 1)

    def block(kb, carry):
        rows = pl.ds(pl.multiple_of(seg_start + kb * sb, sb), sb)
        xs = xs_scr[rows, :]
        gate = _dot(xs, wg)
        hid = (gate * _sigmoid(gate)) * _dot(xs, wu)
        ce = jnp.sum(jnp.where(lane_sb == e, cs_scr[rows, :], 0.0), axis=-1, keepdims=True)
        acc_scr[rows, :] += _dot((ce * hid).astype(bf16), wd)
        return carry

    lax.fori_loop(0, seg_blocks, block, 0)

    @pl.when(e == N_EXPERTS - 1)
    def _():
        xs_scr[...] = acc_scr[...].astype(bf16)
        cid = lax.broadcasted_iota(jnp.int32, (1, n_rows), 1).astype(f32)
        for c in range(tm // sb):
            rows = slice(c * sb, (c + 1) * sb)
            unperm = (pos_scr[rows, 0:1] == cid).astype(bf16)
            y = x_ref[rows, :] + g2_ref[...] * _dot(unperm, xs_scr[...])
            if final_norm:
                y = y * lax.rsqrt(jnp.mean(y * y, axis=-1, keepdims=True) + NORM_EPS) * fg_ref[...]
            o_ref[rows, :] = y


def _moe(x, g, sc, sh, g2, rt, final_g, final_norm, w_gate, w_up, w_down):
    bsz, t, _ = x.shape
    tm = min(t, 1024)
    nt = t // tm
    sb = MOE_SB
    n_rows = tm + N_GROUPS * sb
    gid = rt[..., ROUTE_GID].astype(jnp.int32).reshape(bsz, nt, tm)
    cnt = jnp.sum(gid[..., None] == jnp.arange(N_GROUPS), axis=2).astype(jnp.int32)
    nblk = (cnt + (sb - 1)) // sb
    start = (jnp.cumsum(nblk, axis=-1) - nblk) * sb
    meta = jnp.concatenate([start, nblk], axis=-1).reshape(-1).astype(jnp.int32)
    grid_spec = pltpu.PrefetchScalarGridSpec(
        num_scalar_prefetch=1,
        grid=(bsz, nt, N_EXPERTS),
        in_specs=[
            pl.BlockSpec((None, tm, D_MODEL), lambda b, i, e, m: (b, i, 0)),
            pl.BlockSpec((1, D_MODEL), lambda b, i, e, m: (0, 0)),
            pl.BlockSpec((None, 1, D_MODEL), lambda b, i, e, m: (b, 0, 0)),
            pl.BlockSpec((None, 1, D_MODEL), lambda b, i, e, m: (b, 0, 0)),
            pl.BlockSpec((None, 1, D_MODEL), lambda b, i, e, m: (b, 0, 0)),
            pl.BlockSpec((None, tm, ROUTE_W), lambda b, i, e, m: (b, i, 0)),
            pl.BlockSpec((1, D_MODEL), lambda b, i, e, m: (0, 0)),
            pl.BlockSpec((None, D_MODEL, D_EXPERT), lambda b, i, e, m: (e, 0, 0)),
            pl.BlockSpec((None, D_MODEL, D_EXPERT), lambda b, i, e, m: (e, 0, 0)),
            pl.BlockSpec((None, D_EXPERT, D_MODEL), lambda b, i, e, m: (e, 0, 0)),
        ],
        out_specs=pl.BlockSpec((None, tm, D_MODEL), lambda b, i, e, m: (b, i, 0)),
        scratch_shapes=[pltpu.VMEM((n_rows, D_MODEL), bf16), pltpu.VMEM((n_rows, D_MODEL), f32),
                        pltpu.VMEM((n_rows, ROUTE_W), f32), pltpu.VMEM((tm, 8), f32)],
    )
    return pl.pallas_call(
        functools.partial(_moe_kernel, final_norm=final_norm, tm=tm, nt=nt),
        grid_spec=grid_spec,
        out_shape=jax.ShapeDtypeStruct((bsz, t, D_MODEL), f32),
        compiler_params=pltpu.CompilerParams(
            dimension_semantics=("parallel", "parallel", "arbitrary"), vmem_limit_bytes=MOE_VMEM_LIMIT),
        name="moe",
    )(meta, x, g, sc, sh, g2, rt, final_g, w_gate, w_up, w_down)


def _pack_w_in(w_in):
    ml, rw, gt = w_in[:, :ML_COLS], w_in[:, ML_COLS:ML_COLS + RW_COLS], w_in[:, ML_COLS + RW_COLS:]
    qkvo, mlg = ml[:, :2 * ML_QK + 2 * ML_V], ml[:, 2 * ML_QK + 2 * ML_V:]
    z = lambda n: jnp.zeros((D_MODEL, n), w_in.dtype)
    return jnp.concatenate(
        [gt, rw, z(P_RW_PAD - RW_COLS), qkvo, mlg, z(P_MLG_PAD - 4 * ML_HEADS)], axis=1).astype(bf16)


def _mixer(pfull, is_ctx, lp, ml_state, rw_state):
    hs, ml_fin = [], []
    for d in range(2):
        h_d, c_fin = _mlstm(pfull, lp["taps"], lp["conv_b"], lp["gate_b"], ml_state[d], rev=bool(d), d=d,
                            grid_conv=not is_ctx)
        hs.append(h_d)
        ml_fin.append(c_fin)
    yf, bvf, g, yb, bvb, sff, sfb = _rwkv(pfull, rw_state[0], rw_state[1], lp["rw_mu"], lp["rw_w_up"],
                                          lp["rw_w0"], lp["rw_a_up"], lp["rw_a0"], lp["rw_k_k"], lp["rw_k_a"],
                                          lp["rw_r_k"], lp["ones_bd"], lp["gup"])
    return (hs[0], hs[1], yf, yb, bvf, bvb, g), ml_fin, [sff, sfb]


def kernel(x, c, ctx, c_ctx, w_ada, b_ada, norm1_g, norm2_g, w_in, ml_conv_k, ml_conv_b, ml_gate_b, ml_norm_g, rw_mu, rw_w_up, rw_w0, rw_a_up, rw_a0, rw_g_up, rw_k_k, rw_k_a, rw_r_k, rw_ln_w, rw_ln_b, merge_pa, merge_pb, w_out, router_w, router_b, exp_w_gate, exp_w_up, exp_w_down, final_g):
    bsz = x.shape[0]
    s_rows = jnp.zeros((8, D_MODEL), f32).at[:bsz].set(c).at[bsz].set(c_ctx)
    mod = _ada(s_rows, w_ada, b_ada)
    head_id = jnp.arange(RW_W) // RW_N
    ones_bd = (head_id[:, None] == head_id[None, :]).astype(bf16)
    row = lambda v: v.reshape(1, -1)

    x_lat, x_ctx = x, ctx
    for l in range(DEPTH):
        last = l == DEPTH - 1
        mod_lat = mod[l, :bsz].reshape(bsz, 1, N_MOD, D_MODEL)
        mod_ctx = jnp.broadcast_to(mod[l, bsz].reshape(1, 1, N_MOD, D_MODEL), (bsz, 1, N_MOD, D_MODEL))
        lp = dict(taps=ml_conv_k[l].reshape(9, 2 * ML_QK), conv_b=row(ml_conv_b[l]), gate_b=row(ml_gate_b[l]),
                  rw_mu=jnp.pad(row(rw_mu[l]), ((0, 0), (0, P_RW_PAD - RW_COLS))),
                  rw_w_up=rw_w_up[l].astype(bf16), rw_w0=rw_w0[l], rw_a_up=rw_a_up[l].astype(bf16),
                  rw_a0=rw_a0[l], rw_k_k=row(rw_k_k[l]), rw_k_a=row(rw_k_a[l]), rw_r_k=row(rw_r_k[l]),
                  ones_bd=ones_bd, gup=rw_g_up[l].astype(bf16))
        w_in_bf = _pack_w_in(w_in[l])
        pa_bf, pb_bf, wo_bf = merge_pa[l].astype(bf16), merge_pb[l].astype(bf16), w_out[l].astype(bf16)
        experts = (exp_w_gate[l].astype(bf16), exp_w_up[l].astype(bf16), exp_w_down[l].astype(bf16))
        g1n, g2n = row(norm1_g[l]), row(norm2_g[l])
        readout = (row(ml_norm_g[l]), row(rw_ln_w[l]), row(rw_ln_b[l]), ones_bd, pa_bf, pb_bf, wo_bf)

        def m(modv, i):
            return modv[:, :, i]

        p_ctx = _proj(x_ctx, g1n, m(mod_ctx, 1), m(mod_ctx, 0), w_in_bf)
        ml0 = [jnp.zeros((bsz, ML_HEADS, ML_DQK, 2 * ML_DV), f32)] * 2
        rw0 = [jnp.zeros((bsz, RW_HEADS, RW_N, RW_N), f32)] * 2
        mix_c, ml_st, rw_st = _mixer(p_ctx, True, lp, ml0, rw0)

        p_lat = _proj(x_lat, g1n, m(mod_lat, 1), m(mod_lat, 0), w_in_bf)
        mix_l, _, _ = _mixer(p_lat, False, lp, ml_st, rw_st)
        x_lat = _merge(x_lat, *mix_l, p_lat, m(mod_lat, 2), *readout)
        comb = _route(x_lat, g2n, m(mod_lat, 4), m(mod_lat, 3), router_w, router_b)
        x_lat = _moe(x_lat, g2n, m(mod_lat, 4), m(mod_lat, 3), m(mod_lat, 5), comb, row(final_g), last, *experts)
        if not last:
            x_ctx = _merge(x_ctx, *mix_c, p_ctx, m(mod_ctx, 2), *readout)
            comb_c = _route(x_ctx, g2n, m(mod_ctx, 4), m(mod_ctx, 3), router_w, router_b)
            x_ctx = _moe(x_ctx, g2n, m(mod_ctx, 4), m(mod_ctx, 3), m(mod_ctx, 5), comb_c, row(final_g), False,
                         *experts)
    return x_lat
```

```python
import functools

import jax
import jax.numpy as jnp
import numpy as np
from jax import lax
from jax.experimental import pallas as pl
from jax.experimental.pallas import tpu as pltpu

f32 = jnp.float32
bf16 = jnp.bfloat16

D_MODEL = 1024
DEPTH = 2
GRID_W = 64
N_MOD = 6
NORM_EPS = 1e-6

ML_HEADS = 4
ML_DQK = 64
ML_DV = 128
ML_QK = ML_HEADS * ML_DQK
ML_V = ML_HEADS * ML_DV
GATE_CAP = 15.0
ML_COLS = 2 * ML_QK + 2 * ML_V + 4 * ML_HEADS
ML_CHUNK = 256

RW_HEADS = 8
RW_N = 64
RW_W = RW_HEADS * RW_N
RW_DECAY_LORA = 64
RW_A_LORA = 64
RW_GATE_LORA = 128
RW_GN_EPS = 6.4e-4
RW_COLS = 3 * RW_W + 2 * RW_DECAY_LORA + 2 * RW_A_LORA + RW_GATE_LORA
RW_CHUNK = 64
RW_BLOCK = 256
RW_HALO = 16

N_EXPERTS = 16
N_GROUPS = 4
EXPERTS_PER_GROUP = N_EXPERTS // N_GROUPS
D_EXPERT = 512

P_GA = 0
P_GB = D_MODEL
P_RW = 2 * D_MODEL
P_RW_PAD = 2048
P_QK = P_RW + P_RW_PAD
P_V = P_QK + 2 * ML_QK
P_O = P_V + ML_V
P_MLG = P_O + ML_V
P_MLG_PAD = 512
P_MLG_BLK = 128
P_COLS = P_MLG + P_MLG_PAD
PROJ_TN = 1024

VMEM_LIMIT = 48 * 1024 * 1024
MOE_VMEM_LIMIT = 58 * 1024 * 1024
ROUTE_W = 32
ROUTE_GID = N_EXPERTS
MOE_SB = 256
EXP_NEG_HALF = float(np.exp(-0.5))
NEG_BIG = -1e30


_NN = ((1,), (0,))
_NT = ((1,), (1,))
_TN = ((0,), (0,))


def _dot(a, b, dims=_NN):
    return lax.dot_general(a, b, (dims, ((), ())), preferred_element_type=f32)


def _mm(a, b, dims=_NN):
    return _dot(a.astype(bf16), b.astype(bf16), dims)


def _hi_lo(x):
    hi = x.astype(bf16)
    lo = (x - hi.astype(f32)).astype(bf16)
    return hi, lo


def _mm3(a, b, dims=_NN):
    ah, al = _hi_lo(a)
    bh, bl = _hi_lo(b)
    return _dot(ah, bh, dims) + (_dot(ah, bl, dims) + _dot(al, bh, dims))


def _mm_exact_lhs(a_bf, b, dims=_NN):
    hi = b.astype(bf16)
    r1 = b - hi.astype(f32)
    mid = r1.astype(bf16)
    lo = (r1 - mid.astype(f32)).astype(bf16)
    return _dot(a_bf, hi, dims) + (_dot(a_bf, mid, dims) + _dot(a_bf, lo, dims))


def _head_sum(a, ones_bd):
    hi, lo = _hi_lo(a)
    return _dot(hi, ones_bd) + _dot(lo, ones_bd)


def _sigmoid(x):
    return 1.0 / (1.0 + jnp.exp(-x))


def _norm_mod(x, g, sc, sh):
    y = x * lax.rsqrt(jnp.mean(x * x, axis=-1, keepdims=True) + NORM_EPS)
    return (y * g) * (1.0 + sc) + sh


def _ada_kernel(s_ref, w_ref, b_ref, o_ref):
    s = s_ref[...]
    s = s * _sigmoid(s)
    o_ref[...] = _mm3(s, w_ref[...]) + b_ref[...]


def _ada(s_rows, w_ada, b_ada):
    tn = 1536
    n = N_MOD * D_MODEL
    return pl.pallas_call(
        _ada_kernel,
        grid=(DEPTH, n // tn),
        in_specs=[
            pl.BlockSpec((8, D_MODEL), lambda l, j: (0, 0)),
            pl.BlockSpec((None, D_MODEL, tn), lambda l, j: (l, 0, j)),
            pl.BlockSpec((None, 1, tn), lambda l, j: (l, 0, j)),
        ],
        out_specs=pl.BlockSpec((None, 8, tn), lambda l, j: (l, 0, j)),
        out_shape=jax.ShapeDtypeStruct((DEPTH, 8, n), f32),
        compiler_params=pltpu.CompilerParams(
            dimension_semantics=("arbitrary", "arbitrary"), vmem_limit_bytes=VMEM_LIMIT),
        name="ada",
    )(s_rows, w_ada, b_ada.reshape(DEPTH, 1, n))


def _proj_kernel(x_ref, g_ref, sc_ref, sh_ref, w_ref, o_ref, h_scr):
    @pl.when(pl.program_id(2) == 0)
    def _():
        h_scr[...] = _norm_mod(x_ref[...], g_ref[...], sc_ref[...], sh_ref[...]).astype(bf16)

    o_ref[...] = _dot(h_scr[...], w_ref[...]).astype(bf16)


def _proj(x, g, sc, sh, w_bf):
    bsz, t, _ = x.shape
    tm = min(t, 1024)
    return pl.pallas_call(
        _proj_kernel,
        grid=(bsz, t // tm, P_COLS // PROJ_TN),
        in_specs=[
            pl.BlockSpec((None, tm, D_MODEL), lambda b, i, j: (b, i, 0)),
            pl.BlockSpec((1, D_MODEL), lambda b, i, j: (0, 0)),
            pl.BlockSpec((None, 1, D_MODEL), lambda b, i, j: (b, 0, 0)),
            pl.BlockSpec((None, 1, D_MODEL), lambda b, i, j: (b, 0, 0)),
            pl.BlockSpec((D_MODEL, PROJ_TN), lambda b, i, j: (0, j)),
        ],
        out_specs=pl.BlockSpec((None, tm, PROJ_TN), lambda b, i, j: (b, i, j)),
        out_shape=jax.ShapeDtypeStruct((bsz, t, P_COLS), bf16),
        scratch_shapes=[pltpu.VMEM((tm, D_MODEL), bf16)],
        compiler_params=pltpu.CompilerParams(
            dimension_semantics=("parallel", "parallel", "arbitrary"), vmem_limit_bytes=VMEM_LIMIT),
        name="proj",
    )(x, g, sc, sh, w_bf)


def _shift_rows(u, up_row, dn_row):
    n = u.shape[0]
    rid = lax.broadcasted_iota(jnp.int32, (n, 1), 0)
    up = jnp.where(rid == 0, up_row, pltpu.roll(u, 1, axis=0))
    dn = jnp.where(rid == n - 1, dn_row, pltpu.roll(u, n - 1, axis=0))
    return up, dn


def _log_sigmoid(x):
    return jnp.minimum(x, 0.0) - jnp.log1p(jnp.exp(-jnp.abs(x)))


def _mlstm_kernel(qk_ref, qkp_ref, qkn_ref, v_ref, mlg_ref, taps_ref, cb_ref, gb_ref, c0_ref,
                  h_ref, cfin_ref, c_scr, *, rev, d, grid_conv, nb, bsz):
    L = ML_CHUNK
    j = pl.program_id(0)
    jblk = (nb - 1 - j) if rev else j

    @pl.when(j == 0)
    def _():
        c_scr[...] = c0_ref[...]

    taps = taps_ref[...]
    zero_row = jnp.zeros((1, 2 * ML_QK), f32)
    rid = lax.broadcasted_iota(jnp.int32, (L, 1), 0)
    if grid_conv:
        first_col = (rid % GRID_W) == 0
        last_col = (rid % GRID_W) == GRID_W - 1
    else:
        first_col = rid == 0
        last_col = rid == L - 1
    ii = lax.broadcasted_iota(jnp.int32, (L, L), 0)
    jj = lax.broadcasted_iota(jnp.int32, (L, L), 1)
    incl = (jj >= ii) if rev else (jj <= ii)
    tri = incl.astype(bf16)
    e_i = lax.broadcasted_iota(jnp.int32, (2 * ML_HEADS, 2 * ML_HEADS), 0)
    e_j = lax.broadcasted_iota(jnp.int32, (2 * ML_HEADS, 2 * ML_HEADS), 1)
    eye = (e_i == e_j).astype(bf16)
    one_col = (lax.broadcasted_iota(jnp.int32, (L, ML_DV), 1) == 0).astype(bf16)

    units = []
    for b in range(bsz):
        qk = qk_ref[b].astype(f32)
        if grid_conv:
            above = jnp.where(jblk > 0, qkp_ref[b].astype(f32), 0.0)
            below = jnp.where(jblk < nb - 1, qkn_ref[b].astype(f32), 0.0)
            ext = jnp.concatenate([above, qk, below], axis=0)
            bases = [(dr, ext[dr * GRID_W:dr * GRID_W + L]) for dr in range(3)]
        else:
            bases = [(1, qk)]
        conv = cb_ref[...]
        for dr, base in bases:
            up, dn = _shift_rows(base, zero_row, zero_row)
            conv = conv + (taps[3 * dr:3 * dr + 1] * jnp.where(first_col, 0.0, up)
                           + taps[3 * dr + 1:3 * dr + 2] * base
                           + taps[3 * dr + 2:3 * dr + 3] * jnp.where(last_col, 0.0, dn))
        q = conv[:, :ML_QK]
        k = conv[:, ML_QK:] * (ML_DQK ** -0.5)
        pre = mlg_ref[b][:, :4 * ML_HEADS].astype(f32) + gb_ref[...]
        pre = GATE_CAP * jnp.tanh(pre * (1.0 / GATE_CAP))
        ig = pre[:, d * ML_HEADS:(d + 1) * ML_HEADS]
        lf = _log_sigmoid(pre[:, (2 + d) * ML_HEADS:(3 + d) * ML_HEADS])
        bc = _mm_exact_lhs(tri, lf)
        rows_ = _mm_exact_lhs(eye, jnp.concatenate([ig, bc], axis=1), _NT)
        v = v_ref[b]
        for h in range(ML_HEADS):
            b_col = bc[:, h:h + 1]
            units.append(dict(
                b=b, h=h, q=q[:, h * ML_DQK:(h + 1) * ML_DQK], k=k[:, h * ML_DQK:(h + 1) * ML_DQK],
                vp=jnp.concatenate([v[:, h * ML_DV:(h + 1) * ML_DV], one_col], axis=1),
                b_col=b_col, ig_col=ig[:, h:h + 1], btot=b_col[0:1, :] if rev else b_col[L - 1:L, :],
                b_row=rows_[ML_HEADS + h:ML_HEADS + h + 1, :], ig_row=rows_[h:h + 1, :]))
    decay = [jnp.exp(jnp.where(incl, (u["b_col"] - u["b_row"]) + u["ig_row"], NEG_BIG)) for u in units]
    qk_s = [_mm(u["q"], u["k"], _NT) for u in units]
    qc = [_mm(u["q"], c_scr[u["b"], u["h"]]) for u in units]
    kw = [u["k"] * jnp.exp((u["btot"] - u["b_col"]) + u["ig_col"]) for u in units]
    kv = [_mm(kw[n], u["vp"], _TN) for n, u in enumerate(units)]
    sv = [_mm(qk_s[n] * decay[n], u["vp"]) for n, u in enumerate(units)]
    for n, u in enumerate(units):
        b, h = u["b"], u["h"]
        nd = sv[n] + jnp.exp(u["b_col"]) * qc[n]
        den = nd[:, ML_DV:ML_DV + 1]
        h_ref[b, :, h * ML_DV:(h + 1) * ML_DV] = (nd[:, :ML_DV] / jnp.maximum(jnp.abs(den), 1.0)).astype(bf16)
        c_scr[b, h] = jnp.exp(u["btot"]) * c_scr[b, h] + kv[n]
    cfin_ref[...] = c_scr[...]


def _mlstm(pfull, taps, conv_b, gate_b, c0, rev, d, grid_conv):
    bsz, t, _ = pfull.shape
    L = ML_CHUNK
    nb = t // L
    tb = (lambda j: nb - 1 - j) if rev else (lambda j: j)
    vblk = P_V // ML_V
    qkblk = P_QK // (2 * ML_QK)
    rpb = L // GRID_W
    n_rows = t // GRID_W
    return pl.pallas_call(
        functools.partial(_mlstm_kernel, rev=rev, d=d, grid_conv=grid_conv, nb=nb, bsz=bsz),
        grid=(nb,),
        in_specs=[
            pl.BlockSpec((bsz, L, 2 * ML_QK), lambda j: (0, tb(j), qkblk)),
            pl.BlockSpec((bsz, GRID_W, 2 * ML_QK), lambda j: (0, jnp.maximum(tb(j) * rpb - 1, 0), qkblk)),
            pl.BlockSpec((bsz, GRID_W, 2 * ML_QK),
                         lambda j: (0, jnp.minimum((tb(j) + 1) * rpb, n_rows - 1), qkblk)),
            pl.BlockSpec((bsz, L, ML_V), lambda j: (0, tb(j), vblk)),
            pl.BlockSpec((bsz, L, P_MLG_BLK), lambda j: (0, tb(j), P_MLG // P_MLG_BLK)),
            pl.BlockSpec((9, 2 * ML_QK), lambda j: (0, 0)),
            pl.BlockSpec((1, 2 * ML_QK), lambda j: (0, 0)),
            pl.BlockSpec((1, 4 * ML_HEADS), lambda j: (0, 0)),
            pl.BlockSpec((bsz, ML_HEADS, ML_DQK, 2 * ML_DV), lambda j: (0, 0, 0, 0)),
        ],
        out_specs=[
            pl.BlockSpec((bsz, L, ML_V), lambda j: (0, tb(j), 0)),
            pl.BlockSpec((bsz, ML_HEADS, ML_DQK, 2 * ML_DV), lambda j: (0, 0, 0, 0)),
        ],
        out_shape=[
            jax.ShapeDtypeStruct((bsz, t, ML_V), bf16),
            jax.ShapeDtypeStruct((bsz, ML_HEADS, ML_DQK, 2 * ML_DV), f32),
        ],
        scratch_shapes=[pltpu.VMEM((bsz, ML_HEADS, ML_DQK, 2 * ML_DV), f32)],
        compiler_params=pltpu.CompilerParams(
            dimension_semantics=("arbitrary",), vmem_limit_bytes=VMEM_LIMIT),
        name="mlstm_bwd" if rev else "mlstm_fwd",
    )(pfull, pfull, pfull, pfull, pfull, taps, conv_b, gate_b, c0)


def _rw_prepare(p_ref, pp_ref, pn_ref, jblk, nb, d, par, scr, bv_ref):
    mu_ref, wup_ref, w0_ref, aup_ref, a0_ref, kk_ref, ka_ref, rk_ref, ones_ref = par
    r_scr, v_scr, kk_scr, kd_scr, be_scr, ld_scr = scr
    p = p_ref[...].astype(f32)
    up, dn = _shift_rows(p, jnp.where(jblk > 0, pp_ref[RW_HALO - 1:RW_HALO, :].astype(f32), 0.0),
                         jnp.where(jblk < nb - 1, pn_ref[0:1, :].astype(f32), 0.0))
    p = p + mu_ref[...] * (0.5 * (up + dn) - p)
    r = p[:, 0:RW_W]
    k = p[:, RW_W:2 * RW_W]
    v = p[:, 2 * RW_W:3 * RW_W]
    o_wd = 3 * RW_W + d * RW_DECAY_LORA
    o_ad = 3 * RW_W + 2 * RW_DECAY_LORA + d * RW_A_LORA
    lw = w0_ref[d:d + 1, :] + _mm(jnp.tanh(p[:, o_wd:o_wd + RW_DECAY_LORA]), wup_ref[d])
    ld_scr[...] = -EXP_NEG_HALF * _sigmoid(lw)
    a = _sigmoid(a0_ref[d:d + 1, :] + _mm(p[:, o_ad:o_ad + RW_A_LORA], aup_ref[d]))
    ones_bd = ones_ref[...]
    kkr = k * kk_ref[...]
    kk = kkr * lax.rsqrt(_head_sum(kkr * kkr, ones_bd) + 1e-12)
    kd = k * (1.0 + (a - 1.0) * ka_ref[...])
    bv_ref[...] = (_head_sum(r * kd * rk_ref[...], ones_bd) * v).astype(bf16)
    r_scr[...] = r
    v_scr[...] = v
    kk_scr[...] = kk
    kd_scr[...] = kd
    be_scr[...] = kk * a
    return p


def _rwkv_kernel(pf_ref, pfp_ref, pfn_ref, pb_ref, pbp_ref, pbn_ref, s0f_ref, s0b_ref,
                 mu_ref, wup_ref, w0_ref, aup_ref, a0_ref, kk_ref, ka_ref, rk_ref, ones_ref, gup_ref,
                 yf_ref, bvf_ref, g_ref, yb_ref, bvb_ref, sff_ref, sfb_ref,
                 stf_scr, stb_scr, *scr, nb, bsz):
    L = RW_CHUNK
    nch = RW_BLOCK // L
    j = pl.program_id(0)

    @pl.when(j == 0)
    def _():
        stf_scr[...] = s0f_ref[...]
        stb_scr[...] = s0b_ref[...]

    par = (mu_ref, wup_ref, w0_ref, aup_ref, a0_ref, kk_ref, ka_ref, rk_ref, ones_ref)
    scr_f, scr_b = scr[:6], scr[6:]
    o_gd = 3 * RW_W + 2 * RW_DECAY_LORA + 2 * RW_A_LORA
    streams = []
    for b in range(bsz):
        sf = tuple(s.at[b] for s in scr_f)
        sb_ = tuple(s.at[b] for s in scr_b)
        p_f = _rw_prepare(pf_ref.at[b], pfp_ref.at[b], pfn_ref.at[b], j, nb, 0, par, sf, bvf_ref.at[b])
        _rw_prepare(pb_ref.at[b], pbp_ref.at[b], pbn_ref.at[b], nb - 1 - j, nb, 1, par, sb_, bvb_ref.at[b])
        g_ref[b] = _mm(_sigmoid(p_f[:, o_gd:o_gd + RW_GATE_LORA]), gup_ref[...]).astype(bf16)
        streams.append((sf, stf_scr.at[b], yf_ref.at[b], False))
        streams.append((sb_, stb_scr.at[b], yb_ref.at[b], True))

    ii = lax.broadcasted_iota(jnp.int32, (L, L), 0)
    jj = lax.broadcasted_iota(jnp.int32, (L, L), 1)
    ii2 = lax.broadcasted_iota(jnp.int32, (L, 2 * L), 0)
    jj2 = lax.broadcasted_iota(jnp.int32, (L, 2 * L), 1) & (L - 1)
    masks = []
    for rev in (False, True):
        incl = (jj >= ii) if rev else (jj <= ii)
        strict = (jj > ii) if rev else (jj < ii)
        incl2 = (jj2 >= ii2) if rev else (jj2 <= ii2)
        masks.append((incl2, strict, incl.astype(bf16)))

    def chunk(i, carry):
        units = []
        rows_of = []
        for si, (sc, st_scr, y_ref, rev) in enumerate(streams):
            r_scr, v_scr, kk_scr, kd_scr, be_scr, ld_scr = sc
            incl2, strict, tri = masks[int(rev)]
            c = (nch - 1 - i) if rev else i
            rows = pl.ds(pl.multiple_of(c * L, L), L)
            rows_of.append(rows)
            ld = ld_scr[rows, :]
            b = _mm_exact_lhs(tri, ld)
            btot = b[0:1, :] if rev else b[L - 1:L, :]
            e_nb = jnp.exp(-b)
            e_end = jnp.exp(btot - b)
            kd_c = kd_scr[rows, :]
            be_c = be_scr[rows, :]
            v_c = v_scr[rows, :]
            aw = -kk_scr[rows, :] * jnp.exp(b - ld)
            rw = r_scr[rows, :] * jnp.exp(b)
            bi = be_c * e_nb
            ki = kd_c * e_nb
            bw = be_c * e_end
            kw = kd_c * e_end
            wend = jnp.exp(btot)
            for h in range(RW_HEADS):
                c_ = slice(h * RW_N, (h + 1) * RW_N)
                units.append(dict(si=si, h=h, incl2=incl2, strict=strict, st=st_scr, aw=aw[:, c_], rw=rw[:, c_],
                                  bi=bi[:, c_], ki=ki[:, c_], bw=bw[:, c_], kw=kw[:, c_], v=v_c[:, c_],
                                  wend=wend[:, c_]))
        us = range(len(units))
        m = [_mm(jnp.concatenate([u["aw"], u["rw"]], axis=0),
                 jnp.concatenate([u["bi"], u["ki"]], axis=0), _NT) for u in units]
        pw = [jnp.where(units[n]["strict"], m[n][:L, :L], 0.0) for n in us]
        mak = [jnp.where(units[n]["strict"], m[n][:L, L:], 0.0) for n in us]
        mr = [jnp.where(units[n]["incl2"], m[n][L:, :], 0.0) for n in us]
        mv = [_mm(mak[n], units[n]["v"]) for n in us]
        x = [jnp.concatenate([units[n]["aw"], mv[n]], axis=1) for n in us]
        for it in range(6):
            if it < 5:
                pr = [_mm(pw[n], jnp.concatenate([x[n], pw[n]], axis=1)) for n in us]
                pw = [pr[n][:, 2 * RW_N:] for n in us]
                x = [x[n] + pr[n][:, :2 * RW_N] for n in us]
            else:
                pr = [_mm(pw[n], x[n]) for n in us]
                x = [x[n] + pr[n] for n in us]
        zero = jnp.zeros((L, RW_N), f32)
        ray = [_mm(mr[n], jnp.concatenate(
            [x[n], jnp.concatenate([zero, units[n]["v"]], axis=1)], axis=0)) for n in us]
        gt = [_mm(x[n], units[n]["bw"], _TN) for n in us]
        vk = [_mm(units[n]["v"], units[n]["kw"], _TN) for n in us]
        s_old = [units[n]["st"][units[n]["h"]] for n in us]
        yy = [_mm(units[n]["rw"] + ray[n][:, :RW_N], s_old[n], _NT) for n in us]
        sg = [_mm(s_old[n], gt[n][:RW_N]) for n in us]
        for si, (sc, st_scr, y_ref, rev) in enumerate(streams):
            mine = [n for n in us if units[n]["si"] == si]
            y_ref[rows_of[si], :] = jnp.concatenate(
                [yy[n] + ray[n][:, RW_N:] for n in mine], axis=1).astype(bf16)
            for n in mine:
                st_scr[units[n]["h"]] = (s_old[n] * units[n]["wend"] + sg[n]) + (gt[n][RW_N:] + vk[n])
        return carry

    lax.fori_loop(0, nch, chunk, 0)
    sff_ref[...] = stf_scr[...]
    sfb_ref[...] = stb_scr[...]


def _rwkv(pfull, s0f, s0b, mu, wup, w0, aup, a0, k_k, k_a, r_k, ones_bd, gup):
    bsz, t, _ = pfull.shape
    tb_ = RW_BLOCK
    nb = t // tb_
    cblk = P_RW // P_RW_PAD
    hpb = tb_ // RW_HALO
    nh = t // RW_HALO
    fwd = lambda j: j
    bwd = lambda j: nb - 1 - j

    def p_specs(tb):
        return [
            pl.BlockSpec((bsz, tb_, P_RW_PAD), lambda j: (0, tb(j), cblk)),
            pl.BlockSpec((bsz, RW_HALO, P_RW_PAD), lambda j: (0, jnp.maximum(tb(j) * hpb - 1, 0), cblk)),
            pl.BlockSpec((bsz, RW_HALO, P_RW_PAD), lambda j: (0, jnp.minimum((tb(j) + 1) * hpb, nh - 1), cblk)),
        ]

    full = lambda shape: pl.BlockSpec(shape, lambda j: (0,) * len(shape))
    st_spec = full((bsz, RW_HEADS, RW_N, RW_N))
    tok = lambda tb: pl.BlockSpec((bsz, tb_, RW_W), lambda j: (0, tb(j), 0))
    tok_shape = jax.ShapeDtypeStruct((bsz, t, RW_W), bf16)
    st_shape = jax.ShapeDtypeStruct((bsz, RW_HEADS, RW_N, RW_N), f32)
    return pl.pallas_call(
        functools.partial(_rwkv_kernel, nb=nb, bsz=bsz),
        grid=(nb,),
        in_specs=p_specs(fwd) + p_specs(bwd) + [
            st_spec, st_spec,
            full((1, P_RW_PAD)),
            full((2, RW_DECAY_LORA, RW_W)), full((2, RW_W)),
            full((2, RW_A_LORA, RW_W)), full((2, RW_W)),
            full((1, RW_W)), full((1, RW_W)), full((1, RW_W)),
            full((RW_W, RW_W)),
            full((RW_GATE_LORA, RW_W)),
        ],
        out_specs=[tok(fwd), tok(fwd), tok(fwd), tok(bwd), tok(bwd), st_spec, st_spec],
        out_shape=[tok_shape] * 5 + [st_shape] * 2,
        scratch_shapes=[pltpu.VMEM((bsz, RW_HEADS, RW_N, RW_N), f32) for _ in range(2)]
        + [pltpu.VMEM((bsz, tb_, RW_W), f32) for _ in range(12)],
        compiler_params=pltpu.CompilerParams(
            dimension_semantics=("arbitrary",), vmem_limit_bytes=VMEM_LIMIT),
        name="rwkv",
    )(pfull, pfull, pfull, pfull, pfull, pfull, s0f, s0b, mu, wup, w0, aup, a0, k_k, k_a, r_k, ones_bd, gup)


def _merge_kernel(x_ref, hf_ref, hb_ref, o_ref_, yf_ref, yb_ref, bvf_ref, bvb_ref, g_ref, ga_ref, gb_ref,
                  g1_ref, mlg_ref, lnw_ref, lnb_ref, ones_ref, pa_ref, pb_ref, wo_ref, out_ref):
    hsum = hf_ref[...].astype(f32) + hb_ref[...].astype(f32)
    parts = []
    for h in range(ML_HEADS):
        hh = hsum[:, h * ML_DV:(h + 1) * ML_DV]
        parts.append(hh * lax.rsqrt(jnp.mean(hh * hh, axis=-1, keepdims=True) + NORM_EPS))
    a_lat = jnp.concatenate(parts, axis=1) * mlg_ref[...] * _sigmoid(o_ref_[...].astype(f32))
    y = yf_ref[...].astype(f32) + yb_ref[...].astype(f32)
    ones_bd = ones_ref[...]
    inv_n = 1.0 / RW_N
    mu = _head_sum(y, ones_bd) * inv_n
    dev = y - mu
    var = _head_sum(dev * dev, ones_bd) * inv_n
    yn = dev * lax.rsqrt(var + RW_GN_EPS) * lnw_ref[...] + lnb_ref[...]
    b_lat = (yn + (bvf_ref[...].astype(f32) + bvb_ref[...].astype(f32))) * g_ref[...].astype(f32)
    m = (_sigmoid(ga_ref[...].astype(f32)) * _mm(a_lat, pa_ref[...])
         + _sigmoid(gb_ref[...].astype(f32)) * _mm(b_lat, pb_ref[...]))
    out_ref[...] = x_ref[...] + g1_ref[...] * _mm(m, wo_ref[...])


def _merge(x, hf, hb, yf, yb, bvf, bvb, g, pfull, g1, ml_norm_g, ln_w, ln_b, ones_bd, pa_bf, pb_bf, wo_bf):
    bsz, t, _ = x.shape
    tm = min(t, 512)
    full = lambda shape: pl.BlockSpec(shape, lambda b, i: (0,) * len(shape))
    tok = lambda w: pl.BlockSpec((None, tm, w), lambda b, i: (b, i, 0))
    col = lambda w, off: pl.BlockSpec((None, tm, w), lambda b, i: (b, i, off // w))
    return pl.pallas_call(
        _merge_kernel,
        grid=(bsz, t // tm),
        in_specs=[
            tok(D_MODEL), tok(ML_V), tok(ML_V), col(ML_V, P_O),
            tok(RW_W), tok(RW_W), tok(RW_W), tok(RW_W), tok(RW_W),
            col(D_MODEL, P_GA), col(D_MODEL, P_GB),
            pl.BlockSpec((None, 1, D_MODEL), lambda b, i: (b, 0, 0)),
            full((1, ML_V)), full((1, RW_W)), full((1, RW_W)), full((RW_W, RW_W)),
            full((ML_V, D_MODEL)), full((RW_W, D_MODEL)), full((D_MODEL, D_MODEL)),
        ],
        out_specs=tok(D_MODEL),
        out_shape=jax.ShapeDtypeStruct((bsz, t, D_MODEL), f32),
        compiler_params=pltpu.CompilerParams(
            dimension_semantics=("parallel", "parallel"), vmem_limit_bytes=VMEM_LIMIT),
        name="merge",
    )(x, hf, hb, pfull, yf, yb, bvf, bvb, g, pfull, pfull, g1, ml_norm_g, ln_w, ln_b, ones_bd,
      pa_bf, pb_bf, wo_bf)


def _lane_max(x):
    return jnp.max(x, axis=-1, keepdims=True)


def _first_at(x, val, lane_f):
    return jnp.min(jnp.where(x == val, lane_f, float(ROUTE_W)), axis=-1, keepdims=True)


def _route_kernel(x_ref, g_ref, sc_ref, sh_ref, rw_ref, rb_ref, rt_ref):
    h = _norm_mod(x_ref[...], g_ref[...], sc_ref[...], sh_ref[...])
    scores = _sigmoid(_mm3(h, rw_ref[...]))
    sel = scores + rb_ref[...]
    lane = lax.broadcasted_iota(jnp.int32, sel.shape, 1)
    lane_f = lane.astype(f32)
    grp = lane // EXPERTS_PER_GROUP
    neg = -jnp.inf
    best_g = jnp.zeros(sel.shape[:1] + (1,), jnp.int32)
    best_v = None
    for gi in range(N_GROUPS):
        mg = jnp.where(grp == gi, sel, neg)
        m1 = _lane_max(mg)
        i1 = _first_at(mg, m1, lane_f)
        m2 = _lane_max(jnp.where(lane_f == i1, neg, mg))
        gs = m1 + m2
        if gi == 0:
            best_v = gs
        else:
            upd = gs > best_v
            best_g = jnp.where(upd, gi, best_g)
            best_v = jnp.where(upd, gs, best_v)
    cand = jnp.where(grp == best_g, sel, neg)
    v1 = _lane_max(cand)
    i1 = _first_at(cand, v1, lane_f)
    cand2 = jnp.where(lane_f == i1, neg, cand)
    v2 = _lane_max(cand2)
    i2 = _first_at(cand2, v2, lane_f)
    picked = (lane_f == i1) | (lane_f == i2)
    w = jnp.where(picked, scores, 0.0)
    comb = w / jnp.sum(w, axis=-1, keepdims=True)
    rt_ref[...] = jnp.where(lane == ROUTE_GID, best_g.astype(f32), comb)


def _route(x, g, sc, sh, router_w, router_b):
    bsz, t, _ = x.shape
    tm = min(t, 512)
    pad = ROUTE_W - N_EXPERTS
    rw = jnp.pad(router_w, ((0, 0), (0, pad)))
    rb = jnp.pad(router_b.reshape(1, N_EXPERTS), ((0, 0), (0, pad)))
    return pl.pallas_call(
        _route_kernel,
        grid=(bsz, t // tm),
        in_specs=[
            pl.BlockSpec((None, tm, D_MODEL), lambda b, i: (b, i, 0)),
            pl.BlockSpec((1, D_MODEL), lambda b, i: (0, 0)),
            pl.BlockSpec((None, 1, D_MODEL), lambda b, i: (b, 0, 0)),
            pl.BlockSpec((None, 1, D_MODEL), lambda b, i: (b, 0, 0)),
            pl.BlockSpec((D_MODEL, ROUTE_W), lambda b, i: (0, 0)),
            pl.BlockSpec((1, ROUTE_W), lambda b, i: (0, 0)),
        ],
        out_specs=pl.BlockSpec((None, tm, ROUTE_W), lambda b, i: (b, i, 0)),
        out_shape=jax.ShapeDtypeStruct((bsz, t, ROUTE_W), f32),
        compiler_params=pltpu.CompilerParams(
            dimension_semantics=("parallel", "parallel"), vmem_limit_bytes=VMEM_LIMIT),
        name="route",
    )(x, g, sc, sh, rw, rb)


def _moe_kernel(meta_ref, x_ref, g_ref, sc_ref, sh_ref, g2_ref, rt_ref, fg_ref, wg_ref, wu_ref, wd_ref, o_ref,
                xs_scr, acc_scr, cs_scr, pos_scr, *, final_norm, tm, nt):
    sb = MOE_SB
    n_rows = tm + N_GROUPS * sb
    e = pl.program_id(2)
    base = (pl.program_id(0) * nt + pl.program_id(1)) * (2 * N_GROUPS)

    @pl.when(e == 0)
    def _():
        h = _norm_mod(x_ref[...], g_ref[...], sc_ref[...], sh_ref[...]).astype(bf16)
        rt = rt_ref[...]
        lane = lax.broadcasted_iota(jnp.int32, rt.shape, 1)
        gid = jnp.sum(jnp.where(lane == ROUTE_GID, rt, 0.0), axis=-1, keepdims=True)
        g8 = lax.broadcasted_iota(jnp.int32, (tm, 8), 1)
        onehot = gid == g8.astype(f32)
        ii = lax.broadcasted_iota(jnp.int32, (tm, tm), 0)
        jj = lax.broadcasted_iota(jnp.int32, (tm, tm), 1)
        rank = _dot((jj <= ii).astype(bf16), onehot.astype(bf16))
        start = jnp.zeros((1, 8), f32)
        g8r = lax.broadcasted_iota(jnp.int32, (1, 8), 1)
        for gi in range(N_GROUPS):
            start = jnp.where(g8r == gi, meta_ref[base + gi].astype(f32), start)
        pos = jnp.sum(jnp.where(onehot, (start + rank) - 1.0, 0.0), axis=-1, keepdims=True)
        posmat = jnp.broadcast_to(pos, (tm, 8))
        pos_scr[...] = posmat
        e_i = lax.broadcasted_iota(jnp.int32, (8, 8), 0)
        e_j = lax.broadcasted_iota(jnp.int32, (8, 8), 1)
        pos_row = _mm_exact_lhs((e_i == e_j).astype(bf16), posmat, _NT)[0:1, :]
        hi = rt.astype(bf16)
        r1 = rt - hi.astype(f32)
        mid = r1.astype(bf16)
        lo = (r1 - mid.astype(f32)).astype(bf16)
        rt3 = jnp.concatenate([hi, mid, lo], axis=1)
        for c in range(n_rows // sb):
            rid = (lax.broadcasted_iota(jnp.int32, (sb, 1), 0) + c * sb).astype(f32)
            perm = (rid == pos_row).astype(bf16)
            xs_scr[c * sb:(c + 1) * sb, :] = _dot(perm, h).astype(bf16)
            cc = _dot(perm, rt3)
            cs_scr[c * sb:(c + 1) * sb, :] = cc[:, :ROUTE_W] + (cc[:, ROUTE_W:2 * ROUTE_W] + cc[:, 2 * ROUTE_W:])
        acc_scr[...] = jnp.zeros_like(acc_scr)

    grp = e // EXPERTS_PER_GROUP
    seg_start = meta_ref[base + grp]
    seg_blocks = meta_ref[base + N_GROUPS + grp]
    wg = wg_ref[...]
    wu = wu_ref[...]
    wd = wd_ref[...]
    lane_sb = lax.broadcasted_iota(jnp.int32, (sb, ROUTE_W), 1)

    def block(kb, carry):
        rows = pl.ds(pl.multiple_of(seg_start + kb * sb, sb), sb)
        xs = xs_scr[rows, :]
        gate = _dot(xs, wg)
        hid = (gate * _sigmoid(gate)) * _dot(xs, wu)
        ce = jnp.sum(jnp.where(lane_sb == e, cs_scr[rows, :], 0.0), axis=-1, keepdims=True)
        acc_scr[rows, :] += _dot((ce * hid).astype(bf16), wd)
        return carry

    lax.fori_loop(0, seg_blocks, block, 0)

    @pl.when(e == N_EXPERTS - 1)
    def _():
        xs_scr[...] = acc_scr[...].astype(bf16)
        cid = lax.broadcasted_iota(jnp.int32, (1, n_rows), 1).astype(f32)
        for c in range(tm // sb):
            rows = slice(c * sb, (c + 1) * sb)
            unperm = (pos_scr[rows, 0:1] == cid).astype(bf16)
            y = x_ref[rows, :] + g2_ref[...] * _dot(unperm, xs_scr[...])
            if final_norm:
                y = y * lax.rsqrt(jnp.mean(y * y, axis=-1, keepdims=True) + NORM_EPS) * fg_ref[...]
            o_ref[rows, :] = y


def _moe(x, g, sc, sh, g2, rt, final_g, final_norm, w_gate, w_up, w_down):
    bsz, t, _ = x.shape
    tm = min(t, 1024)
    nt = t // tm
    sb = MOE_SB
    n_rows = tm + N_GROUPS * sb
    gid = rt[..., ROUTE_GID].astype(jnp.int32).reshape(bsz, nt, tm)
    cnt = jnp.sum(gid[..., None] == jnp.arange(N_GROUPS), axis=2).astype(jnp.int32)
    nblk = (cnt + (sb - 1)) // sb
    start = (jnp.cumsum(nblk, axis=-1) - nblk) * sb
    meta = jnp.concatenate([start, nblk], axis=-1).reshape(-1).astype(jnp.int32)
    grid_spec = pltpu.PrefetchScalarGridSpec(
        num_scalar_prefetch=1,
        grid=(bsz, nt, N_EXPERTS),
        in_specs=[
            pl.BlockSpec((None, tm, D_MODEL), lambda b, i, e, m: (b, i, 0)),
            pl.BlockSpec((1, D_MODEL), lambda b, i, e, m: (0, 0)),
            pl.BlockSpec((None, 1, D_MODEL), lambda b, i, e, m: (b, 0, 0)),
            pl.BlockSpec((None, 1, D_MODEL), lambda b, i, e, m: (b, 0, 0)),
            pl.BlockSpec((None, 1, D_MODEL), lambda b, i, e, m: (b, 0, 0)),
            pl.BlockSpec((None, tm, ROUTE_W), lambda b, i, e, m: (b, i, 0)),
            pl.BlockSpec((1, D_MODEL), lambda b, i, e, m: (0, 0)),
            pl.BlockSpec((None, D_MODEL, D_EXPERT), lambda b, i, e, m: (e, 0, 0)),
            pl.BlockSpec((None, D_MODEL, D_EXPERT), lambda b, i, e, m: (e, 0, 0)),
            pl.BlockSpec((None, D_EXPERT, D_MODEL), lambda b, i, e, m: (e, 0, 0)),
        ],
        out_specs=pl.BlockSpec((None, tm, D_MODEL), lambda b, i, e, m: (b, i, 0)),
        scratch_shapes=[pltpu.VMEM((n_rows, D_MODEL), bf16), pltpu.VMEM((n_rows, D_MODEL), f32),
                        pltpu.VMEM((n_rows, ROUTE_W), f32), pltpu.VMEM((tm, 8), f32)],
    )
    return pl.pallas_call(
        functools.partial(_moe_kernel, final_norm=final_norm, tm=tm, nt=nt),
        grid_spec=grid_spec,
        out_shape=jax.ShapeDtypeStruct((bsz, t, D_MODEL), f32),
        compiler_params=pltpu.CompilerParams(
            dimension_semantics=("parallel", "parallel", "arbitrary"), vmem_limit_bytes=MOE_VMEM_LIMIT),
        name="moe",
    )(meta, x, g, sc, sh, g2, rt, final_g, w_gate, w_up, w_down)


def _pack_w_in(w_in):
    ml, rw, gt = w_in[:, :ML_COLS], w_in[:, ML_COLS:ML_COLS + RW_COLS], w_in[:, ML_COLS + RW_COLS:]
    qkvo, mlg = ml[:, :2 * ML_QK + 2 * ML_V], ml[:, 2 * ML_QK + 2 * ML_V:]
    z = lambda n: jnp.zeros((D_MODEL, n), w_in.dtype)
    return jnp.concatenate(
        [gt, rw, z(P_RW_PAD - RW_COLS), qkvo, mlg, z(P_MLG_PAD - 4 * ML_HEADS)], axis=1).astype(bf16)


def _mixer(pfull, is_ctx, lp, ml_state, rw_state):
    hs, ml_fin = [], []
    for d in range(2):
        h_d, c_fin = _mlstm(pfull, lp["taps"], lp["conv_b"], lp["gate_b"], ml_state[d], rev=bool(d), d=d,
                            grid_conv=not is_ctx)
        hs.append(h_d)
        ml_fin.append(c_fin)
    yf, bvf, g, yb, bvb, sff, sfb = _rwkv(pfull, rw_state[0], rw_state[1], lp["rw_mu"], lp["rw_w_up"],
                                          lp["rw_w0"], lp["rw_a_up"], lp["rw_a0"], lp["rw_k_k"], lp["rw_k_a"],
                                          lp["rw_r_k"], lp["ones_bd"], lp["gup"])
    return (hs[0], hs[1], yf, yb, bvf, bvb, g), ml_fin, [sff, sfb]


def kernel(x, c, ctx, c_ctx, w_ada, b_ada, norm1_g, norm2_g, w_in, ml_conv_k, ml_conv_b, ml_gate_b, ml_norm_g, rw_mu, rw_w_up, rw_w0, rw_a_up, rw_a0, rw_g_up, rw_k_k, rw_k_a, rw_r_k, rw_ln_w, rw_ln_b, merge_pa, merge_pb, w_out, router_w, router_b, exp_w_gate, exp_w_up, exp_w_down, final_g):
    bsz = x.shape[0]
    s_rows = jnp.zeros((8, D_MODEL), f32).at[:bsz].set(c).at[bsz].set(c_ctx)
    mod = _ada(s_rows, w_ada, b_ada)
    head_id = jnp.arange(RW_W) // RW_N
    ones_bd = (head_id[:, None] == head_id[None, :]).astype(bf16)
    row = lambda v: v.reshape(1, -1)

    x_lat, x_ctx = x, ctx
    for l in range(DEPTH):
        last = l == DEPTH - 1
        mod_lat = mod[l, :bsz].reshape(bsz, 1, N_MOD, D_MODEL)
        mod_ctx = jnp.broadcast_to(mod[l, bsz].reshape(1, 1, N_MOD, D_MODEL), (bsz, 1, N_MOD, D_MODEL))
        lp = dict(taps=ml_conv_k[l].reshape(9, 2 * ML_QK), conv_b=row(ml_conv_b[l]), gate_b=row(ml_gate_b[l]),
                  rw_mu=jnp.pad(row(rw_mu[l]), ((0, 0), (0, P_RW_PAD - RW_COLS))),
                  rw_w_up=rw_w_up[l].astype(bf16), rw_w0=rw_w0[l], rw_a_up=rw_a_up[l].astype(bf16),
                  rw_a0=rw_a0[l], rw_k_k=row(rw_k_k[l]), rw_k_a=row(rw_k_a[l]), rw_r_k=row(rw_r_k[l]),
                  ones_bd=ones_bd, gup=rw_g_up[l].astype(bf16))
        w_in_bf = _pack_w_in(w_in[l])
        pa_bf, pb_bf, wo_bf = merge_pa[l].astype(bf16), merge_pb[l].astype(bf16), w_out[l].astype(bf16)
        experts = (exp_w_gate[l].astype(bf16), exp_w_up[l].astype(bf16), exp_w_down[l].astype(bf16))
        g1n, g2n = row(norm1_g[l]), row(norm2_g[l])
        readout = (row(ml_norm_g[l]), row(rw_ln_w[l]), row(rw_ln_b[l]), ones_bd, pa_bf, pb_bf, wo_bf)

        def m(modv, i):
            return modv[:, :, i]

        p_ctx = _proj(x_ctx, g1n, m(mod_ctx, 1), m(mod_ctx, 0), w_in_bf)
        ml0 = [jnp.zeros((bsz, ML_HEADS, ML_DQK, 2 * ML_DV), f32)] * 2
        rw0 = [jnp.zeros((bsz, RW_HEADS, RW_N, RW_N), f32)] * 2
        mix_c, ml_st, rw_st = _mixer(p_ctx, True, lp, ml0, rw0)

        p_lat = _proj(x_lat, g1n, m(mod_lat, 1), m(mod_lat, 0), w_in_bf)
        mix_l, _, _ = _mixer(p_lat, False, lp, ml_st, rw_st)
        x_lat = _merge(x_lat, *mix_l, p_lat, m(mod_lat, 2), *readout)
        comb = _route(x_lat, g2n, m(mod_lat, 4), m(mod_lat, 3), router_w, router_b)
        x_lat = _moe(x_lat, g2n, m(mod_lat, 4), m(mod_lat, 3), m(mod_lat, 5), comb, row(final_g), last, *experts)
        if not last:
            x_ctx = _merge(x_ctx, *mix_c, p_ctx, m(mod_ctx, 2), *readout)
            comb_c = _route(x_ctx, g2n, m(mod_ctx, 4), m(mod_ctx, 3), router_w, router_b)
            x_ctx = _moe(x_ctx, g2n, m(mod_ctx, 4), m(mod_ctx, 3), m(mod_ctx, 5), comb_c, row(final_g), False,
                         *experts)
    return x_lat
```

```python
import functools

import jax
import jax.numpy as jnp
import numpy as np
from jax import lax
from jax.experimental import pallas as pl
from jax.experimental.pallas import tpu as pltpu

f32 = jnp.float32
bf16 = jnp.bfloat16

D_MODEL = 1024
DEPTH = 2
GRID_W = 64
N_MOD = 6
NORM_EPS = 1e-6

ML_HEADS = 4
ML_DQK = 64
ML_DV = 128
ML_QK = ML_HEADS * ML_DQK
ML_V = ML_HEADS * ML_DV
GATE_CAP = 15.0
ML_COLS = 2 * ML_QK + 2 * ML_V + 4 * ML_HEADS
ML_CHUNK = 256

RW_HEADS = 8
RW_N = 64
RW_W = RW_HEADS * RW_N
RW_DECAY_LORA = 64
RW_A_LORA = 64
RW_GATE_LORA = 128
RW_GN_EPS = 6.4e-4
RW_COLS = 3 * RW_W + 2 * RW_DECAY_LORA + 2 * RW_A_LORA + RW_GATE_LORA
RW_CHUNK = 64
RW_BLOCK = 256
RW_HALO = 16

N_EXPERTS = 16
N_GROUPS = 4
EXPERTS_PER_GROUP = N_EXPERTS // N_GROUPS
D_EXPERT = 512

P_GA = 0
P_GB = D_MODEL
P_RW = 2 * D_MODEL
P_RW_PAD = 2048
P_QK = P_RW + P_RW_PAD
P_V = P_QK + 2 * ML_QK
P_O = P_V + ML_V
P_MLG = P_O + ML_V
P_MLG_PAD = 512
P_MLG_BLK = 128
P_COLS = P_MLG + P_MLG_PAD
PROJ_TN = 1024

VMEM_LIMIT = 48 * 1024 * 1024
MOE_VMEM_LIMIT = 58 * 1024 * 1024
ROUTE_W = 32
ROUTE_GID = N_EXPERTS
MOE_SB = 256
MOE_ALIGN = 16
EXP_NEG_HALF = float(np.exp(-0.5))
NEG_BIG = -1e30


_NN = ((1,), (0,))
_NT = ((1,), (1,))
_TN = ((0,), (0,))


def _dot(a, b, dims=_NN):
    return lax.dot_general(a, b, (dims, ((), ())), preferred_element_type=f32)


def _mm(a, b, dims=_NN):
    return _dot(a.astype(bf16), b.astype(bf16), dims)


def _hi_lo(x):
    hi = x.astype(bf16)
    lo = (x - hi.astype(f32)).astype(bf16)
    return hi, lo


def _mm3(a, b, dims=_NN):
    ah, al = _hi_lo(a)
    bh, bl = _hi_lo(b)
    return _dot(ah, bh, dims) + (_dot(ah, bl, dims) + _dot(al, bh, dims))


def _mm_exact_lhs(a_bf, b, dims=_NN):
    hi = b.astype(bf16)
    r1 = b - hi.astype(f32)
    mid = r1.astype(bf16)
    lo = (r1 - mid.astype(f32)).astype(bf16)
    return _dot(a_bf, hi, dims) + (_dot(a_bf, mid, dims) + _dot(a_bf, lo, dims))


def _head_sum(a, ones_bd):
    return _dot(a.astype(bf16), ones_bd)


def _sigmoid(x):
    return 1.0 / (1.0 + jnp.exp(-x))


def _norm_mod(x, g, sc, sh):
    y = x * lax.rsqrt(jnp.mean(x * x, axis=-1, keepdims=True) + NORM_EPS)
    return (y * g) * (1.0 + sc) + sh


def _ada_kernel(s_ref, w_ref, b_ref, o_ref):
    s = s_ref[...]
    s = s * _sigmoid(s)
    o_ref[...] = _mm3(s, w_ref[...]) + b_ref[...]


def _ada(s_rows, w_ada, b_ada):
    tn = 1536
    n = N_MOD * D_MODEL
    return pl.pallas_call(
        _ada_kernel,
        grid=(DEPTH, n // tn),
        in_specs=[
            pl.BlockSpec((8, D_MODEL), lambda l, j: (0, 0)),
            pl.BlockSpec((None, D_MODEL, tn), lambda l, j: (l, 0, j)),
            pl.BlockSpec((None, 1, tn), lambda l, j: (l, 0, j)),
        ],
        out_specs=pl.BlockSpec((None, 8, tn), lambda l, j: (l, 0, j)),
        out_shape=jax.ShapeDtypeStruct((DEPTH, 8, n), f32),
        compiler_params=pltpu.CompilerParams(
            dimension_semantics=("arbitrary", "arbitrary"), vmem_limit_bytes=VMEM_LIMIT),
        name="ada",
    )(s_rows, w_ada, b_ada.reshape(DEPTH, 1, n))


def _proj_kernel(x_ref, g_ref, sc_ref, sh_ref, w_ref, o_ref, h_scr):
    @pl.when(pl.program_id(2) == 0)
    def _():
        h_scr[...] = _norm_mod(x_ref[...], g_ref[...], sc_ref[...], sh_ref[...]).astype(bf16)

    o_ref[...] = _dot(h_scr[...], w_ref[...]).astype(bf16)


def _proj(x, g, sc, sh, w_bf):
    bsz, t, _ = x.shape
    tm = min(t, 1024)
    return pl.pallas_call(
        _proj_kernel,
        grid=(bsz, t // tm, P_COLS // PROJ_TN),
        in_specs=[
            pl.BlockSpec((None, tm, D_MODEL), lambda b, i, j: (b, i, 0)),
            pl.BlockSpec((1, D_MODEL), lambda b, i, j: (0, 0)),
            pl.BlockSpec((None, 1, D_MODEL), lambda b, i, j: (b, 0, 0)),
            pl.BlockSpec((None, 1, D_MODEL), lambda b, i, j: (b, 0, 0)),
            pl.BlockSpec((D_MODEL, PROJ_TN), lambda b, i, j: (0, j)),
        ],
        out_specs=pl.BlockSpec((None, tm, PROJ_TN), lambda b, i, j: (b, i, j)),
        out_shape=jax.ShapeDtypeStruct((bsz, t, P_COLS), bf16),
        scratch_shapes=[pltpu.VMEM((tm, D_MODEL), bf16)],
        compiler_params=pltpu.CompilerParams(
            dimension_semantics=("parallel", "parallel", "arbitrary"), vmem_limit_bytes=VMEM_LIMIT),
        name="proj",
    )(x, g, sc, sh, w_bf)


def _shift_rows(u, up_row, dn_row):
    n = u.shape[0]
    rid = lax.broadcasted_iota(jnp.int32, (n, 1), 0)
    up = jnp.where(rid == 0, up_row, pltpu.roll(u, 1, axis=0))
    dn = jnp.where(rid == n - 1, dn_row, pltpu.roll(u, n - 1, axis=0))
    return up, dn


def _log_sigmoid(x):
    return jnp.minimum(x, 0.0) - jnp.log1p(jnp.exp(-jnp.abs(x)))


def _mlstm_kernel(qk_ref, qkp_ref, qkn_ref, v_ref, mlg_ref, taps_ref, cb_ref, gb_ref, c0_ref,
                  h_ref, cfin_ref, c_scr, *, rev, d, grid_conv, nb, bsz):
    L = ML_CHUNK
    j = pl.program_id(0)
    jblk = (nb - 1 - j) if rev else j

    @pl.when(j == 0)
    def _():
        c_scr[...] = c0_ref[...]

    taps = taps_ref[...]
    zero_row = jnp.zeros((1, 2 * ML_QK), f32)
    rid = lax.broadcasted_iota(jnp.int32, (L, 1), 0)
    if grid_conv:
        first_col = (rid % GRID_W) == 0
        last_col = (rid % GRID_W) == GRID_W - 1
    else:
        first_col = rid == 0
        last_col = rid == L - 1
    ii = lax.broadcasted_iota(jnp.int32, (L, L), 0)
    jj = lax.broadcasted_iota(jnp.int32, (L, L), 1)
    incl = (jj >= ii) if rev else (jj <= ii)
    tri = incl.astype(bf16)
    e_i = lax.broadcasted_iota(jnp.int32, (2 * ML_HEADS, 2 * ML_HEADS), 0)
    e_j = lax.broadcasted_iota(jnp.int32, (2 * ML_HEADS, 2 * ML_HEADS), 1)
    eye = (e_i == e_j).astype(bf16)
    one_col = (lax.broadcasted_iota(jnp.int32, (L, ML_DV), 1) == 0).astype(bf16)

    units = []
    for b in range(bsz):
        qk = qk_ref[b].astype(f32)
        if grid_conv:
            above = jnp.where(jblk > 0, qkp_ref[b].astype(f32), 0.0)
            below = jnp.where(jblk < nb - 1, qkn_ref[b].astype(f32), 0.0)
            ext = jnp.concatenate([above, qk, below], axis=0)
            bases = [(dr, ext[dr * GRID_W:dr * GRID_W + L]) for dr in range(3)]
        else:
            bases = [(1, qk)]
        conv = cb_ref[...]
        for dr, base in bases:
            up, dn = _shift_rows(base, zero_row, zero_row)
            conv = conv + (taps[3 * dr:3 * dr + 1] * jnp.where(first_col, 0.0, up)
                           + taps[3 * dr + 1:3 * dr + 2] * base
                           + taps[3 * dr + 2:3 * dr + 3] * jnp.where(last_col, 0.0, dn))
        q = conv[:, :ML_QK]
        k = conv[:, ML_QK:] * (ML_DQK ** -0.5)
        pre = mlg_ref[b][:, :4 * ML_HEADS].astype(f32) + gb_ref[...]
        pre = GATE_CAP * jnp.tanh(pre * (1.0 / GATE_CAP))
        ig = pre[:, d * ML_HEADS:(d + 1) * ML_HEADS]
        lf = _log_sigmoid(pre[:, (2 + d) * ML_HEADS:(3 + d) * ML_HEADS])
        bc = _mm_exact_lhs(tri, lf)
        rows_ = _mm_exact_lhs(eye, jnp.concatenate([ig, bc], axis=1), _NT)
        v = v_ref[b]
        for h in range(ML_HEADS):
            b_col = bc[:, h:h + 1]
            units.append(dict(
                b=b, h=h, q=q[:, h * ML_DQK:(h + 1) * ML_DQK], k=k[:, h * ML_DQK:(h + 1) * ML_DQK],
                vp=jnp.concatenate([v[:, h * ML_DV:(h + 1) * ML_DV], one_col], axis=1),
                b_col=b_col, ig_col=ig[:, h:h + 1], btot=b_col[0:1, :] if rev else b_col[L - 1:L, :],
                b_row=rows_[ML_HEADS + h:ML_HEADS + h + 1, :], ig_row=rows_[h:h + 1, :]))
    decay = [jnp.exp(jnp.where(incl, (u["b_col"] - u["b_row"]) + u["ig_row"], NEG_BIG)) for u in units]
    qk_s = [_mm(u["q"], u["k"], _NT) for u in units]
    qc = [_mm(u["q"], c_scr[u["b"], u["h"]]) for u in units]
    kw = [u["k"] * jnp.exp((u["btot"] - u["b_col"]) + u["ig_col"]) for u in units]
    kv = [_mm(kw[n], u["vp"], _TN) for n, u in enumerate(units)]
    sv = [_mm(qk_s[n] * decay[n], u["vp"]) for n, u in enumerate(units)]
    for n, u in enumerate(units):
        b, h = u["b"], u["h"]
        nd = sv[n] + jnp.exp(u["b_col"]) * qc[n]
        den = nd[:, ML_DV:ML_DV + 1]
        h_ref[b, :, h * ML_DV:(h + 1) * ML_DV] = (nd[:, :ML_DV] / jnp.maximum(jnp.abs(den), 1.0)).astype(bf16)
        c_scr[b, h] = jnp.exp(u["btot"]) * c_scr[b, h] + kv[n]
    cfin_ref[...] = c_scr[...]


def _mlstm(pfull, taps, conv_b, gate_b, c0, rev, d, grid_conv):
    bsz, t, _ = pfull.shape
    L = ML_CHUNK
    nb = t // L
    tb = (lambda j: nb - 1 - j) if rev else (lambda j: j)
    vblk = P_V // ML_V
    qkblk = P_QK // (2 * ML_QK)
    rpb = L // GRID_W
    n_rows = t // GRID_W
    return pl.pallas_call(
        functools.partial(_mlstm_kernel, rev=rev, d=d, grid_conv=grid_conv, nb=nb, bsz=bsz),
        grid=(nb,),
        in_specs=[
            pl.BlockSpec((bsz, L, 2 * ML_QK), lambda j: (0, tb(j), qkblk)),
            pl.BlockSpec((bsz, GRID_W, 2 * ML_QK), lambda j: (0, jnp.maximum(tb(j) * rpb - 1, 0), qkblk)),
            pl.BlockSpec((bsz, GRID_W, 2 * ML_QK),
                         lambda j: (0, jnp.minimum((tb(j) + 1) * rpb, n_rows - 1), qkblk)),
            pl.BlockSpec((bsz, L, ML_V), lambda j: (0, tb(j), vblk)),
            pl.BlockSpec((bsz, L, P_MLG_BLK), lambda j: (0, tb(j), P_MLG // P_MLG_BLK)),
            pl.BlockSpec((9, 2 * ML_QK), lambda j: (0, 0)),
            pl.BlockSpec((1, 2 * ML_QK), lambda j: (0, 0)),
            pl.BlockSpec((1, 4 * ML_HEADS), lambda j: (0, 0)),
            pl.BlockSpec((bsz, ML_HEADS, ML_DQK, 2 * ML_DV), lambda j: (0, 0, 0, 0)),
        ],
        out_specs=[
            pl.BlockSpec((bsz, L, ML_V), lambda j: (0, tb(j), 0)),
            pl.BlockSpec((bsz, ML_HEADS, ML_DQK, 2 * ML_DV), lambda j: (0, 0, 0, 0)),
        ],
        out_shape=[
            jax.ShapeDtypeStruct((bsz, t, ML_V), bf16),
            jax.ShapeDtypeStruct((bsz, ML_HEADS, ML_DQK, 2 * ML_DV), f32),
        ],
        scratch_shapes=[pltpu.VMEM((bsz, ML_HEADS, ML_DQK, 2 * ML_DV), f32)],
        compiler_params=pltpu.CompilerParams(
            dimension_semantics=("arbitrary",), vmem_limit_bytes=VMEM_LIMIT),
        name="mlstm_bwd" if rev else "mlstm_fwd",
    )(pfull, pfull, pfull, pfull, pfull, taps, conv_b, gate_b, c0)


def _rw_prepare(p_ref, pp_ref, pn_ref, jblk, nb, d, par, scr, bv_ref):
    mu_ref, wup_ref, w0_ref, aup_ref, a0_ref, kk_ref, ka_ref, rk_ref, ones_ref, nbr_ref = par
    r_scr, v_scr, kk_scr, kd_scr, be_scr, ld_scr = scr
    tb_ = RW_BLOCK
    p_bf = p_ref[...]
    p = p_bf.astype(f32)
    nb_avg = _dot(nbr_ref[...], p_bf)
    rid8 = lax.broadcasted_iota(jnp.int32, (8, 1), 0)
    prev_row = jnp.where(jblk > 0, pp_ref[RW_HALO - 1:RW_HALO, :].astype(f32), 0.0)
    next_row = jnp.where(jblk < nb - 1, pn_ref[0:1, :].astype(f32), 0.0)
    nb_avg = jnp.concatenate([nb_avg[0:8] + jnp.where(rid8 == 0, 0.5 * prev_row, 0.0),
                              nb_avg[8:tb_ - 8],
                              nb_avg[tb_ - 8:] + jnp.where(rid8 == 7, 0.5 * next_row, 0.0)], axis=0)
    p = p + mu_ref[...] * (nb_avg - p)
    r = p[:, 0:RW_W]
    k = p[:, RW_W:2 * RW_W]
    v = p[:, 2 * RW_W:3 * RW_W]
    o_wd = 3 * RW_W + d * RW_DECAY_LORA
    o_ad = 3 * RW_W + 2 * RW_DECAY_LORA + d * RW_A_LORA
    lw = w0_ref[d:d + 1, :] + _mm(jnp.tanh(p[:, o_wd:o_wd + RW_DECAY_LORA]), wup_ref[d])
    ld_scr[...] = -EXP_NEG_HALF * _sigmoid(lw)
    a = _sigmoid(a0_ref[d:d + 1, :] + _mm(p[:, o_ad:o_ad + RW_A_LORA], aup_ref[d]))
    ones_bd = ones_ref[...]
    kkr = k * kk_ref[...]
    kk = kkr * lax.rsqrt(_head_sum(kkr * kkr, ones_bd) + 1e-12)
    kd = k * (1.0 + (a - 1.0) * ka_ref[...])
    bv_ref[...] = (_head_sum(r * kd * rk_ref[...], ones_bd) * v).astype(bf16)
    r_scr[...] = r
    v_scr[...] = v
    kk_scr[...] = kk
    kd_scr[...] = kd
    be_scr[...] = kk * a
    return p


def _rwkv_kernel(pf_ref, pfp_ref, pfn_ref, pb_ref, pbp_ref, pbn_ref, s0f_ref, s0b_ref,
                 mu_ref, wup_ref, w0_ref, aup_ref, a0_ref, kk_ref, ka_ref, rk_ref, ones_ref, gup_ref, nbr_ref,
                 yf_ref, bvf_ref, g_ref, yb_ref, bvb_ref, sff_ref, sfb_ref,
                 stf_scr, stb_scr, *scr, nb, bsz):
    L = RW_CHUNK
    nch = RW_BLOCK // L
    j = pl.program_id(0)

    @pl.when(j == 0)
    def _():
        stf_scr[...] = s0f_ref[...]
        stb_scr[...] = s0b_ref[...]

    par = (mu_ref, wup_ref, w0_ref, aup_ref, a0_ref, kk_ref, ka_ref, rk_ref, ones_ref, nbr_ref)
    scr_f, scr_b = scr[:6], scr[6:]
    o_gd = 3 * RW_W + 2 * RW_DECAY_LORA + 2 * RW_A_LORA
    streams = []
    for b in range(bsz):
        sf = tuple(s.at[b] for s in scr_f)
        sb_ = tuple(s.at[b] for s in scr_b)
        p_f = _rw_prepare(pf_ref.at[b], pfp_ref.at[b], pfn_ref.at[b], j, nb, 0, par, sf, bvf_ref.at[b])
        _rw_prepare(pb_ref.at[b], pbp_ref.at[b], pbn_ref.at[b], nb - 1 - j, nb, 1, par, sb_, bvb_ref.at[b])
        g_ref[b] = _mm(_sigmoid(p_f[:, o_gd:o_gd + RW_GATE_LORA]), gup_ref[...]).astype(bf16)
        streams.append((sf, stf_scr.at[b], yf_ref.at[b], False))
        streams.append((sb_, stb_scr.at[b], yb_ref.at[b], True))

    ii = lax.broadcasted_iota(jnp.int32, (L, L), 0)
    jj = lax.broadcasted_iota(jnp.int32, (L, L), 1)
    ii2 = lax.broadcasted_iota(jnp.int32, (L, 2 * L), 0)
    jj2 = lax.broadcasted_iota(jnp.int32, (L, 2 * L), 1) & (L - 1)
    masks = []
    for rev in (False, True):
        incl = (jj >= ii) if rev else (jj <= ii)
        strict = (jj > ii) if rev else (jj < ii)
        incl2 = (jj2 >= ii2) if rev else (jj2 <= ii2)
        masks.append((incl2, strict, incl.astype(bf16)))

    def chunk(i, carry):
        units = []
        rows_of = []
        for si, (sc, st_scr, y_ref, rev) in enumerate(streams):
            r_scr, v_scr, kk_scr, kd_scr, be_scr, ld_scr = sc
            incl2, strict, tri = masks[int(rev)]
            c = (nch - 1 - i) if rev else i
            rows = pl.ds(pl.multiple_of(c * L, L), L)
            rows_of.append(rows)
            ld = ld_scr[rows, :]
            b = _mm_exact_lhs(tri, ld)
            btot = b[0:1, :] if rev else b[L - 1:L, :]
            e_nb = jnp.exp(-b)
            e_end = jnp.exp(btot - b)
            kd_c = kd_scr[rows, :]
            be_c = be_scr[rows, :]
            v_c = v_scr[rows, :]
            aw = -kk_scr[rows, :] * jnp.exp(b - ld)
            rw = r_scr[rows, :] * jnp.exp(b)
            bi = be_c * e_nb
            ki = kd_c * e_nb
            bw = be_c * e_end
            kw = kd_c * e_end
            wend = jnp.exp(btot)
            for h in range(RW_HEADS):
                c_ = slice(h * RW_N, (h + 1) * RW_N)
                units.append(dict(si=si, h=h, incl2=incl2, strict=strict, st=st_scr, aw=aw[:, c_], rw=rw[:, c_],
                                  bi=bi[:, c_], ki=ki[:, c_], bw=bw[:, c_], kw=kw[:, c_], v=v_c[:, c_],
                                  wend=wend[:, c_]))
        us = range(len(units))
        m = [_mm(jnp.concatenate([u["aw"], u["rw"]], axis=0),
                 jnp.concatenate([u["bi"], u["ki"]], axis=0), _NT) for u in units]
        pw = [jnp.where(units[n]["strict"], m[n][:L, :L], 0.0) for n in us]
        mak = [jnp.where(units[n]["strict"], m[n][:L, L:], 0.0) for n in us]
        mr = [jnp.where(units[n]["incl2"], m[n][L:, :], 0.0) for n in us]
        mv = [_mm(mak[n], units[n]["v"]) for n in us]
        x = [jnp.concatenate([units[n]["aw"], mv[n]], axis=1) for n in us]
        for it in range(6):
            if it < 5:
                pr = [_mm(pw[n], jnp.concatenate([x[n], pw[n]], axis=1)) for n in us]
                pw = [pr[n][:, 2 * RW_N:] for n in us]
                x = [x[n] + pr[n][:, :2 * RW_N] for n in us]
            else:
                pr = [_mm(pw[n], x[n]) for n in us]
                x = [x[n] + pr[n] for n in us]
        zero = jnp.zeros((L, RW_N), f32)
        ray = [_mm(mr[n], jnp.concatenate(
            [x[n], jnp.concatenate([zero, units[n]["v"]], axis=1)], axis=0)) for n in us]
        gt = [_mm(x[n], units[n]["bw"], _TN) for n in us]
        vk = [_mm(units[n]["v"], units[n]["kw"], _TN) for n in us]
        s_old = [units[n]["st"][units[n]["h"]] for n in us]
        yy = [_mm(units[n]["rw"] + ray[n][:, :RW_N], s_old[n], _NT) for n in us]
        sg = [_mm(s_old[n], gt[n][:RW_N]) for n in us]
        for si, (sc, st_scr, y_ref, rev) in enumerate(streams):
            mine = [n for n in us if units[n]["si"] == si]
            y_ref[rows_of[si], :] = jnp.concatenate(
                [yy[n] + ray[n][:, RW_N:] for n in mine], axis=1).astype(bf16)
            for n in mine:
                st_scr[units[n]["h"]] = (s_old[n] * units[n]["wend"] + sg[n]) + (gt[n][RW_N:] + vk[n])
        return carry

    lax.fori_loop(0, nch, chunk, 0)
    sff_ref[...] = stf_scr[...]
    sfb_ref[...] = stb_scr[...]


def _rwkv(pfull, s0f, s0b, mu, wup, w0, aup, a0, k_k, k_a, r_k, ones_bd, gup, nbr):
    bsz, t, _ = pfull.shape
    tb_ = RW_BLOCK
    nb = t // tb_
    cblk = P_RW // P_RW_PAD
    hpb = tb_ // RW_HALO
    nh = t // RW_HALO
    fwd = lambda j: j
    bwd = lambda j: nb - 1 - j

    def p_specs(tb):
        return [
            pl.BlockSpec((bsz, tb_, P_RW_PAD), lambda j: (0, tb(j), cblk)),
            pl.BlockSpec((bsz, RW_HALO, P_RW_PAD), lambda j: (0, jnp.maximum(tb(j) * hpb - 1, 0), cblk)),
            pl.BlockSpec((bsz, RW_HALO, P_RW_PAD), lambda j: (0, jnp.minimum((tb(j) + 1) * hpb, nh - 1), cblk)),
        ]

    full = lambda shape: pl.BlockSpec(shape, lambda j: (0,) * len(shape))
    st_spec = full((bsz, RW_HEADS, RW_N, RW_N))
    tok = lambda tb: pl.BlockSpec((bsz, tb_, RW_W), lambda j: (0, tb(j), 0))
    tok_shape = jax.ShapeDtypeStruct((bsz, t, RW_W), bf16)
    st_shape = jax.ShapeDtypeStruct((bsz, RW_HEADS, RW_N, RW_N), f32)
    return pl.pallas_call(
        functools.partial(_rwkv_kernel, nb=nb, bsz=bsz),
        grid=(nb,),
        in_specs=p_specs(fwd) + p_specs(bwd) + [
            st_spec, st_spec,
            full((1, P_RW_PAD)),
            full((2, RW_DECAY_LORA, RW_W)), full((2, RW_W)),
            full((2, RW_A_LORA, RW_W)), full((2, RW_W)),
            full((1, RW_W)), full((1, RW_W)), full((1, RW_W)),
            full((RW_W, RW_W)),
            full((RW_GATE_LORA, RW_W)),
            full((tb_, tb_)),
        ],
        out_specs=[tok(fwd), tok(fwd), tok(fwd), tok(bwd), tok(bwd), st_spec, st_spec],
        out_shape=[tok_shape] * 5 + [st_shape] * 2,
        scratch_shapes=[pltpu.VMEM((bsz, RW_HEADS, RW_N, RW_N), f32) for _ in range(2)]
        + [pltpu.VMEM((bsz, tb_, RW_W), f32) for _ in range(12)],
        compiler_params=pltpu.CompilerParams(
            dimension_semantics=("arbitrary",), vmem_limit_bytes=VMEM_LIMIT),
        name="rwkv",
    )(pfull, pfull, pfull, pfull, pfull, pfull, s0f, s0b, mu, wup, w0, aup, a0, k_k, k_a, r_k, ones_bd, gup, nbr)


def _merge_kernel(x_ref, hf_ref, hb_ref, o_ref_, yf_ref, yb_ref, bvf_ref, bvb_ref, g_ref, ga_ref, gb_ref,
                  g1_ref, mlg_ref, lnw_ref, lnb_ref, ones_ref, pa_ref, pb_ref, wo_ref, out_ref):
    hsum = hf_ref[...].astype(f32) + hb_ref[...].astype(f32)
    parts = []
    for h in range(ML_HEADS):
        hh = hsum[:, h * ML_DV:(h + 1) * ML_DV]
        parts.append(hh * lax.rsqrt(jnp.mean(hh * hh, axis=-1, keepdims=True) + NORM_EPS))
    a_lat = jnp.concatenate(parts, axis=1) * mlg_ref[...] * _sigmoid(o_ref_[...].astype(f32))
    y = yf_ref[...].astype(f32) + yb_ref[...].astype(f32)
    ones_bd = ones_ref[...]
    inv_n = 1.0 / RW_N
    mu = _head_sum(y, ones_bd) * inv_n
    dev = y - mu
    var = _head_sum(dev * dev, ones_bd) * inv_n
    yn = dev * lax.rsqrt(var + RW_GN_EPS) * lnw_ref[...] + lnb_ref[...]
    b_lat = (yn + (bvf_ref[...].astype(f32) + bvb_ref[...].astype(f32))) * g_ref[...].astype(f32)
    m = (_sigmoid(ga_ref[...].astype(f32)) * _mm(a_lat, pa_ref[...])
         + _sigmoid(gb_ref[...].astype(f32)) * _mm(b_lat, pb_ref[...]))
    out_ref[...] = x_ref[...] + g1_ref[...] * _mm(m, wo_ref[...])


def _merge(x, hf, hb, yf, yb, bvf, bvb, g, pfull, g1, ml_norm_g, ln_w, ln_b, ones_bd, pa_bf, pb_bf, wo_bf):
    bsz, t, _ = x.shape
    tm = min(t, 512)
    full = lambda shape: pl.BlockSpec(shape, lambda b, i: (0,) * len(shape))
    tok = lambda w: pl.BlockSpec((None, tm, w), lambda b, i: (b, i, 0))
    col = lambda w, off: pl.BlockSpec((None, tm, w), lambda b, i: (b, i, off // w))
    return pl.pallas_call(
        _merge_kernel,
        grid=(bsz, t // tm),
        in_specs=[
            tok(D_MODEL), tok(ML_V), tok(ML_V), col(ML_V, P_O),
            tok(RW_W), tok(RW_W), tok(RW_W), tok(RW_W), tok(RW_W),
            col(D_MODEL, P_GA), col(D_MODEL, P_GB),
            pl.BlockSpec((None, 1, D_MODEL), lambda b, i: (b, 0, 0)),
            full((1, ML_V)), full((1, RW_W)), full((1, RW_W)), full((RW_W, RW_W)),
            full((ML_V, D_MODEL)), full((RW_W, D_MODEL)), full((D_MODEL, D_MODEL)),
        ],
        out_specs=tok(D_MODEL),
        out_shape=jax.ShapeDtypeStruct((bsz, t, D_MODEL), f32),
        compiler_params=pltpu.CompilerParams(
            dimension_semantics=("parallel", "parallel"), vmem_limit_bytes=VMEM_LIMIT),
        name="merge",
    )(x, hf, hb, pfull, yf, yb, bvf, bvb, g, pfull, pfull, g1, ml_norm_g, ln_w, ln_b, ones_bd,
      pa_bf, pb_bf, wo_bf)


def _lane_max(x):
    return jnp.max(x, axis=-1, keepdims=True)


def _first_at(x, val, lane_f):
    return jnp.min(jnp.where(x == val, lane_f, float(ROUTE_W)), axis=-1, keepdims=True)


def _route_kernel(x_ref, g_ref, sc_ref, sh_ref, rw_ref, rb_ref, rt_ref):
    h = _norm_mod(x_ref[...], g_ref[...], sc_ref[...], sh_ref[...])
    scores = _sigmoid(_mm3(h, rw_ref[...]))
    sel = scores + rb_ref[...]
    lane = lax.broadcasted_iota(jnp.int32, sel.shape, 1)
    lane_f = lane.astype(f32)
    grp = lane // EXPERTS_PER_GROUP
    neg = -jnp.inf
    best_g = jnp.zeros(sel.shape[:1] + (1,), jnp.int32)
    best_v = None
    for gi in range(N_GROUPS):
        mg = jnp.where(grp == gi, sel, neg)
        m1 = _lane_max(mg)
        i1 = _first_at(mg, m1, lane_f)
        m2 = _lane_max(jnp.where(lane_f == i1, neg, mg))
        gs = m1 + m2
        if gi == 0:
            best_v = gs
        else:
            upd = gs > best_v
            best_g = jnp.where(upd, gi, best_g)
            best_v = jnp.where(upd, gs, best_v)
    cand = jnp.where(grp == best_g, sel, neg)
    v1 = _lane_max(cand)
    i1 = _first_at(cand, v1, lane_f)
    cand2 = jnp.where(lane_f == i1, neg, cand)
    v2 = _lane_max(cand2)
    i2 = _first_at(cand2, v2, lane_f)
    picked = (lane_f == i1) | (lane_f == i2)
    w = jnp.where(picked, scores, 0.0)
    comb = w / jnp.sum(w, axis=-1, keepdims=True)
    rt_ref[...] = jnp.where(lane == ROUTE_GID, best_g.astype(f32), comb)


def _route(x, g, sc, sh, router_w, router_b):
    bsz, t, _ = x.shape
    tm = min(t, 512)
    pad = ROUTE_W - N_EXPERTS
    rw = jnp.pad(router_w, ((0, 0), (0, pad)))
    rb = jnp.pad(router_b.reshape(1, N_EXPERTS), ((0, 0), (0, pad)))
    return pl.pallas_call(
        _route_kernel,
        grid=(bsz, t // tm),
        in_specs=[
            pl.BlockSpec((None, tm, D_MODEL), lambda b, i: (b, i, 0)),
            pl.BlockSpec((1, D_MODEL), lambda b, i: (0, 0)),
            pl.BlockSpec((None, 1, D_MODEL), lambda b, i: (b, 0, 0)),
            pl.BlockSpec((None, 1, D_MODEL), lambda b, i: (b, 0, 0)),
            pl.BlockSpec((D_MODEL, ROUTE_W), lambda b, i: (0, 0)),
            pl.BlockSpec((1, ROUTE_W), lambda b, i: (0, 0)),
        ],
        out_specs=pl.BlockSpec((None, tm, ROUTE_W), lambda b, i: (b, i, 0)),
        out_shape=jax.ShapeDtypeStruct((bsz, t, ROUTE_W), f32),
        compiler_params=pltpu.CompilerParams(
            dimension_semantics=("parallel", "parallel"), vmem_limit_bytes=VMEM_LIMIT),
        name="route",
    )(x, g, sc, sh, rw, rb)


def _moe_kernel(meta_ref, x_ref, g_ref, sc_ref, sh_ref, g2_ref, rt_ref, fg_ref, wg_ref, wu_ref, wd_ref, o_ref,
                xs_scr, ys_scr, cs_scr, pos_scr, *, final_norm, tm, nt):
    sb = MOE_SB
    n_rows = tm + 2 * sb
    grp = pl.program_id(2)
    base = (pl.program_id(0) * nt + pl.program_id(1)) * (2 * N_GROUPS)

    @pl.when(grp == 0)
    def _():
        h = _norm_mod(x_ref[...], g_ref[...], sc_ref[...], sh_ref[...]).astype(bf16)
        rt = rt_ref[...]
        lane = lax.broadcasted_iota(jnp.int32, rt.shape, 1)
        gid = jnp.sum(jnp.where(lane == ROUTE_GID, rt, 0.0), axis=-1, keepdims=True)
        g8 = lax.broadcasted_iota(jnp.int32, (tm, 8), 1)
        onehot = gid == g8.astype(f32)
        ii = lax.broadcasted_iota(jnp.int32, (tm, tm), 0)
        jj = lax.broadcasted_iota(jnp.int32, (tm, tm), 1)
        rank = _dot((jj <= ii).astype(bf16), onehot.astype(bf16))
        start = jnp.zeros((1, 8), f32)
        g8r = lax.broadcasted_iota(jnp.int32, (1, 8), 1)
        for gi in range(N_GROUPS):
            start = jnp.where(g8r == gi, meta_ref[base + gi].astype(f32), start)
        pos = jnp.sum(jnp.where(onehot, (start + rank) - 1.0, 0.0), axis=-1, keepdims=True)
        posmat = jnp.broadcast_to(pos, (tm, 8))
        pos_scr[...] = posmat
        e_i = lax.broadcasted_iota(jnp.int32, (8, 8), 0)
        e_j = lax.broadcasted_iota(jnp.int32, (8, 8), 1)
        pos_row = _mm_exact_lhs((e_i == e_j).astype(bf16), posmat, _NT)[0:1, :]
        hi = rt.astype(bf16)
        r1 = rt - hi.astype(f32)
        mid = r1.astype(bf16)
        lo = (r1 - mid.astype(f32)).astype(bf16)
        rt3 = jnp.concatenate([hi, mid, lo], axis=1)
        for c in range(n_rows // sb):
            rid = (lax.broadcasted_iota(jnp.int32, (sb, 1), 0) + c * sb).astype(f32)
            perm = (rid == pos_row).astype(bf16)
            xs_scr[c * sb:(c + 1) * sb, :] = _dot(perm, h).astype(bf16)
            cc = _dot(perm, rt3)
            cs_scr[c * sb:(c + 1) * sb, :] = cc[:, :ROUTE_W] + (cc[:, ROUTE_W:2 * ROUTE_W] + cc[:, 2 * ROUTE_W:])
        ys_scr[...] = jnp.zeros_like(ys_scr)

    seg_start = meta_ref[base + grp]
    seg_blocks = meta_ref[base + N_GROUPS + grp]
    lane_sb = lax.broadcasted_iota(jnp.int32, (sb, ROUTE_W), 1)

    def block(kb, carry):
        rows = pl.ds(pl.multiple_of(seg_start + kb * sb, MOE_ALIGN), sb)
        xs = xs_scr[rows, :]
        cs = cs_scr[rows, :]
        acc = None
        for k in range(EXPERTS_PER_GROUP):
            gate = _dot(xs, wg_ref[k])
            hid = (gate * _sigmoid(gate)) * _dot(xs, wu_ref[k])
            ce = jnp.sum(jnp.where(lane_sb == grp * EXPERTS_PER_GROUP + k, cs, 0.0), axis=-1, keepdims=True)
            part = _dot((ce * hid).astype(bf16), wd_ref[k])
            acc = part if acc is None else acc + part
        ys_scr[rows, :] = acc.astype(bf16)
        return carry

    lax.fori_loop(0, seg_blocks, block, 0)

    @pl.when(grp == N_GROUPS - 1)
    def _():
        cid = lax.broadcasted_iota(jnp.int32, (1, n_rows), 1).astype(f32)
        for c in range(tm // sb):
            rows = slice(c * sb, (c + 1) * sb)
            unperm = (pos_scr[rows, 0:1] == cid).astype(bf16)
            y = x_ref[rows, :] + g2_ref[...] * _dot(unperm, ys_scr[...])
            if final_norm:
                y = y * lax.rsqrt(jnp.mean(y * y, axis=-1, keepdims=True) + NORM_EPS) * fg_ref[...]
            o_ref[rows, :] = y


def _moe(x, g, sc, sh, g2, rt, final_g, final_norm, w_gate, w_up, w_down):
    bsz, t, _ = x.shape
    tm = min(t, 1024)
    nt = t // tm
    sb = MOE_SB
    n_rows = tm + 2 * sb
    gid = rt[..., ROUTE_GID].astype(jnp.int32).reshape(bsz, nt, tm)
    cnt = jnp.sum(gid[..., None] == jnp.arange(N_GROUPS), axis=2).astype(jnp.int32)
    nblk = (cnt + (sb - 1)) // sb
    seg = ((cnt + (MOE_ALIGN - 1)) // MOE_ALIGN) * MOE_ALIGN
    start = jnp.cumsum(seg, axis=-1) - seg
    meta = jnp.concatenate([start, nblk], axis=-1).reshape(-1).astype(jnp.int32)
    wspec = lambda shape: pl.BlockSpec((EXPERTS_PER_GROUP,) + shape, lambda b, i, e, m: (e, 0, 0))
    grid_spec = pltpu.PrefetchScalarGridSpec(
        num_scalar_prefetch=1,
        grid=(bsz, nt, N_GROUPS),
        in_specs=[
            pl.BlockSpec((None, tm, D_MODEL), lambda b, i, e, m: (b, i, 0)),
            pl.BlockSpec((1, D_MODEL), lambda b, i, e, m: (0, 0)),
            pl.BlockSpec((None, 1, D_MODEL), lambda b, i, e, m: (b, 0, 0)),
            pl.BlockSpec((None, 1, D_MODEL), lambda b, i, e, m: (b, 0, 0)),
            pl.BlockSpec((None, 1, D_MODEL), lambda b, i, e, m: (b, 0, 0)),
            pl.BlockSpec((None, tm, ROUTE_W), lambda b, i, e, m: (b, i, 0)),
            pl.BlockSpec((1, D_MODEL), lambda b, i, e, m: (0, 0)),
            wspec((D_MODEL, D_EXPERT)), wspec((D_MODEL, D_EXPERT)), wspec((D_EXPERT, D_MODEL)),
        ],
        out_specs=pl.BlockSpec((None, tm, D_MODEL), lambda b, i, e, m: (b, i, 0)),
        scratch_shapes=[pltpu.VMEM((n_rows, D_MODEL), bf16), pltpu.VMEM((n_rows, D_MODEL), bf16),
                        pltpu.VMEM((n_rows, ROUTE_W), f32), pltpu.VMEM((tm, 8), f32)],
    )
    return pl.pallas_call(
        functools.partial(_moe_kernel, final_norm=final_norm, tm=tm, nt=nt),
        grid_spec=grid_spec,
        out_shape=jax.ShapeDtypeStruct((bsz, t, D_MODEL), f32),
        compiler_params=pltpu.CompilerParams(
            dimension_semantics=("parallel", "parallel", "arbitrary"), vmem_limit_bytes=MOE_VMEM_LIMIT),
        name="moe",
    )(meta, x, g, sc, sh, g2, rt, final_g, w_gate, w_up, w_down)


def _pack_w_in(w_in):
    ml, rw, gt = w_in[:, :ML_COLS], w_in[:, ML_COLS:ML_COLS + RW_COLS], w_in[:, ML_COLS + RW_COLS:]
    qkvo, mlg = ml[:, :2 * ML_QK + 2 * ML_V], ml[:, 2 * ML_QK + 2 * ML_V:]
    z = lambda n: jnp.zeros((D_MODEL, n), w_in.dtype)
    return jnp.concatenate(
        [gt, rw, z(P_RW_PAD - RW_COLS), qkvo, mlg, z(P_MLG_PAD - 4 * ML_HEADS)], axis=1).astype(bf16)


def _mixer(pfull, is_ctx, lp, ml_state, rw_state):
    hs, ml_fin = [], []
    for d in range(2):
        h_d, c_fin = _mlstm(pfull, lp["taps"], lp["conv_b"], lp["gate_b"], ml_state[d], rev=bool(d), d=d,
                            grid_conv=not is_ctx)
        hs.append(h_d)
        ml_fin.append(c_fin)
    yf, bvf, g, yb, bvb, sff, sfb = _rwkv(pfull, rw_state[0], rw_state[1], lp["rw_mu"], lp["rw_w_up"],
                                          lp["rw_w0"], lp["rw_a_up"], lp["rw_a0"], lp["rw_k_k"], lp["rw_k_a"],
                                          lp["rw_r_k"], lp["ones_bd"], lp["gup"], lp["nbr"])
    return (hs[0], hs[1], yf, yb, bvf, bvb, g), ml_fin, [sff, sfb]


def kernel(x, c, ctx, c_ctx, w_ada, b_ada, norm1_g, norm2_g, w_in, ml_conv_k, ml_conv_b, ml_gate_b, ml_norm_g, rw_mu, rw_w_up, rw_w0, rw_a_up, rw_a0, rw_g_up, rw_k_k, rw_k_a, rw_r_k, rw_ln_w, rw_ln_b, merge_pa, merge_pb, w_out, router_w, router_b, exp_w_gate, exp_w_up, exp_w_down, final_g):
    bsz = x.shape[0]
    s_rows = jnp.zeros((8, D_MODEL), f32).at[:bsz].set(c).at[bsz].set(c_ctx)
    mod = _ada(s_rows, w_ada, b_ada)
    head_id = jnp.arange(RW_W) // RW_N
    ones_bd = (head_id[:, None] == head_id[None, :]).astype(bf16)
    tok_id = jnp.arange(RW_BLOCK)
    nbr = (0.5 * (jnp.abs(tok_id[:, None] - tok_id[None, :]) == 1)).astype(bf16)
    row = lambda v: v.reshape(1, -1)

    x_lat, x_ctx = x, ctx
    for l in range(DEPTH):
        last = l == DEPTH - 1
        mod_lat = mod[l, :bsz].reshape(bsz, 1, N_MOD, D_MODEL)
        mod_ctx = jnp.broadcast_to(mod[l, bsz].reshape(1, 1, N_MOD, D_MODEL), (bsz, 1, N_MOD, D_MODEL))
        lp = dict(taps=ml_conv_k[l].reshape(9, 2 * ML_QK), conv_b=row(ml_conv_b[l]), gate_b=row(ml_gate_b[l]),
                  rw_mu=jnp.pad(row(rw_mu[l]), ((0, 0), (0, P_RW_PAD - RW_COLS))),
                  rw_w_up=rw_w_up[l].astype(bf16), rw_w0=rw_w0[l], rw_a_up=rw_a_up[l].astype(bf16),
                  rw_a0=rw_a0[l], rw_k_k=row(rw_k_k[l]), rw_k_a=row(rw_k_a[l]), rw_r_k=row(rw_r_k[l]),
                  ones_bd=ones_bd, gup=rw_g_up[l].astype(bf16), nbr=nbr)
        w_in_bf = _pack_w_in(w_in[l])
        pa_bf, pb_bf, wo_bf = merge_pa[l].astype(bf16), merge_pb[l].astype(bf16), w_out[l].astype(bf16)
        experts = (exp_w_gate[l].astype(bf16), exp_w_up[l].astype(bf16), exp_w_down[l].astype(bf16))
        g1n, g2n = row(norm1_g[l]), row(norm2_g[l])
        readout = (row(ml_norm_g[l]), row(rw_ln_w[l]), row(rw_ln_b[l]), ones_bd, pa_bf, pb_bf, wo_bf)

        def m(modv, i):
            return modv[:, :, i]

        p_ctx = _proj(x_ctx, g1n, m(mod_ctx, 1), m(mod_ctx, 0), w_in_bf)
        ml0 = [jnp.zeros((bsz, ML_HEADS, ML_DQK, 2 * ML_DV), f32)] * 2
        rw0 = [jnp.zeros((bsz, RW_HEADS, RW_N, RW_N), f32)] * 2
        mix_c, ml_st, rw_st = _mixer(p_ctx, True, lp, ml0, rw0)

        p_lat = _proj(x_lat, g1n, m(mod_lat, 1), m(mod_lat, 0), w_in_bf)
        mix_l, _, _ = _mixer(p_lat, False, lp, ml_st, rw_st)
        x_lat = _merge(x_lat, *mix_l, p_lat, m(mod_lat, 2), *readout)
        comb = _route(x_lat, g2n, m(mod_lat, 4), m(mod_lat, 3), router_w, router_b)
        x_lat = _moe(x_lat, g2n, m(mod_lat, 4), m(mod_lat, 3), m(mod_lat, 5), comb, row(final_g), last, *experts)
        if not last:
            x_ctx = _merge(x_ctx, *mix_c, p_ctx, m(mod_ctx, 2), *readout)
            comb_c = _route(x_ctx, g2n, m(mod_ctx, 4), m(mod_ctx, 3), router_w, router_b)
            x_ctx = _moe(x_ctx, g2n, m(mod_ctx, 4), m(mod_ctx, 3), m(mod_ctx, 5), comb_c, row(final_g), False,
                         *experts)
    return x_lat
```

```python
import functools

import jax
import jax.numpy as jnp
import numpy as np
from jax import lax
from jax.experimental import pallas as pl
from jax.experimental.pallas import tpu as pltpu

f32 = jnp.float32
bf16 = jnp.bfloat16

D_MODEL = 1024
DEPTH = 2
GRID_W = 64
N_MOD = 6
NORM_EPS = 1e-6

ML_HEADS = 4
ML_DQK = 64
ML_DV = 128
ML_QK = ML_HEADS * ML_DQK
ML_V = ML_HEADS * ML_DV
GATE_CAP = 15.0
ML_COLS = 2 * ML_QK + 2 * ML_V + 4 * ML_HEADS
ML_CHUNK = 256

RW_HEADS = 8
RW_N = 64
RW_W = RW_HEADS * RW_N
RW_DECAY_LORA = 64
RW_A_LORA = 64
RW_GATE_LORA = 128
RW_GN_EPS = 6.4e-4
RW_COLS = 3 * RW_W + 2 * RW_DECAY_LORA + 2 * RW_A_LORA + RW_GATE_LORA
RW_CHUNK = 64
RW_BLOCK = 256
RW_HALO = 16

N_EXPERTS = 16
N_GROUPS = 4
EXPERTS_PER_GROUP = N_EXPERTS // N_GROUPS
D_EXPERT = 512

P_GA = 0
P_GB = D_MODEL
P_RW = 2 * D_MODEL
P_RW_PAD = 2048
P_MLG = P_RW + RW_COLS
P_MLG_BLK = 128
P_QK = P_RW + P_RW_PAD
P_V = P_QK + 2 * ML_QK
P_O = P_V + ML_V
P_COLS = P_O + ML_V
PROJ_TN = P_COLS // 2

VMEM_LIMIT = 48 * 1024 * 1024
MOE_VMEM_LIMIT = 58 * 1024 * 1024
ROUTE_W = 32
ROUTE_GID = N_EXPERTS
MOE_SB = 256
MOE_ALIGN = 16
EXP_NEG_HALF = float(np.exp(-0.5))
NEG_BIG = -1e30


_NN = ((1,), (0,))
_NT = ((1,), (1,))
_TN = ((0,), (0,))


def _dot(a, b, dims=_NN):
    return lax.dot_general(a, b, (dims, ((), ())), preferred_element_type=f32)


def _mm(a, b, dims=_NN):
    return _dot(a.astype(bf16), b.astype(bf16), dims)


def _hi_lo(x):
    hi = x.astype(bf16)
    lo = (x - hi.astype(f32)).astype(bf16)
    return hi, lo


def _mm3(a, b, dims=_NN):
    ah, al = _hi_lo(a)
    bh, bl = _hi_lo(b)
    return _dot(ah, bh, dims) + (_dot(ah, bl, dims) + _dot(al, bh, dims))


def _mm_exact_lhs(a_bf, b, dims=_NN):
    hi = b.astype(bf16)
    r1 = b - hi.astype(f32)
    mid = r1.astype(bf16)
    lo = (r1 - mid.astype(f32)).astype(bf16)
    return _dot(a_bf, hi, dims) + (_dot(a_bf, mid, dims) + _dot(a_bf, lo, dims))


def _head_sum(a, ones_bd):
    return _dot(a.astype(bf16), ones_bd)


def _sigmoid(x):
    return 1.0 / (1.0 + jnp.exp(-x))


def _norm_mod(x, g, sc, sh):
    y = x * lax.rsqrt(jnp.mean(x * x, axis=-1, keepdims=True) + NORM_EPS)
    return (y * g) * (1.0 + sc) + sh


def _ada_kernel(s_ref, w_ref, b_ref, o_ref):
    s = s_ref[...]
    s = s * _sigmoid(s)
    o_ref[...] = _mm3(s, w_ref[...]) + b_ref[...]


def _ada(s_rows, w_ada, b_ada):
    tn = 1536
    n = N_MOD * D_MODEL
    return pl.pallas_call(
        _ada_kernel,
        grid=(DEPTH, n // tn),
        in_specs=[
            pl.BlockSpec((8, D_MODEL), lambda l, j: (0, 0)),
            pl.BlockSpec((None, D_MODEL, tn), lambda l, j: (l, 0, j)),
            pl.BlockSpec((None, 1, tn), lambda l, j: (l, 0, j)),
        ],
        out_specs=pl.BlockSpec((None, 8, tn), lambda l, j: (l, 0, j)),
        out_shape=jax.ShapeDtypeStruct((DEPTH, 8, n), f32),
        compiler_params=pltpu.CompilerParams(
            dimension_semantics=("arbitrary", "arbitrary"), vmem_limit_bytes=VMEM_LIMIT),
        name="ada",
    )(s_rows, w_ada, b_ada.reshape(DEPTH, 1, n))


def _proj_kernel(x_ref, g_ref, sc_ref, sh_ref, w_ref, o_ref, h_scr):
    @pl.when(pl.program_id(2) == 0)
    def _():
        h_scr[...] = _norm_mod(x_ref[...], g_ref[...], sc_ref[...], sh_ref[...]).astype(bf16)

    o_ref[...] = _dot(h_scr[...], w_ref[...]).astype(bf16)


def _proj(x, g, sc, sh, w_bf):
    bsz, t, _ = x.shape
    tm = min(t, 1024)
    return pl.pallas_call(
        _proj_kernel,
        grid=(bsz, t // tm, P_COLS // PROJ_TN),
        in_specs=[
            pl.BlockSpec((None, tm, D_MODEL), lambda b, i, j: (b, i, 0)),
            pl.BlockSpec((1, D_MODEL), lambda b, i, j: (0, 0)),
            pl.BlockSpec((None, 1, D_MODEL), lambda b, i, j: (b, 0, 0)),
            pl.BlockSpec((None, 1, D_MODEL), lambda b, i, j: (b, 0, 0)),
            pl.BlockSpec((D_MODEL, PROJ_TN), lambda b, i, j: (0, j)),
        ],
        out_specs=pl.BlockSpec((None, tm, PROJ_TN), lambda b, i, j: (b, i, j)),
        out_shape=jax.ShapeDtypeStruct((bsz, t, P_COLS), bf16),
        scratch_shapes=[pltpu.VMEM((tm, D_MODEL), bf16)],
        compiler_params=pltpu.CompilerParams(
            dimension_semantics=("parallel", "parallel", "arbitrary"), vmem_limit_bytes=VMEM_LIMIT),
        name="proj",
    )(x, g, sc, sh, w_bf)


def _shift_rows(u, up_row, dn_row):
    n = u.shape[0]
    rid = lax.broadcasted_iota(jnp.int32, (n, 1), 0)
    up = jnp.where(rid == 0, up_row, pltpu.roll(u, 1, axis=0))
    dn = jnp.where(rid == n - 1, dn_row, pltpu.roll(u, n - 1, axis=0))
    return up, dn


def _log_sigmoid(x):
    return jnp.minimum(x, 0.0) - jnp.log1p(jnp.exp(-jnp.abs(x)))


def _mlstm_kernel(*refs, rev, d, grid_conv, nb, bsz, conv_done):
    if conv_done:
        qk_ref, v_ref, mlg_ref, gb_ref, c0_ref, h_ref, cfin_ref, c_scr = refs
    else:
        (qk_ref, qkp_ref, qkn_ref, v_ref, mlg_ref, taps_ref, cb_ref, gb_ref, c0_ref,
         h_ref, cfin_ref, qkc_ref, c_scr) = refs
    L = ML_CHUNK
    j = pl.program_id(0)
    jblk = (nb - 1 - j) if rev else j

    @pl.when(j == 0)
    def _():
        c_scr[...] = c0_ref[...]

    zero_row = jnp.zeros((1, 2 * ML_QK), f32)
    rid = lax.broadcasted_iota(jnp.int32, (L, 1), 0)
    if grid_conv:
        first_col = (rid % GRID_W) == 0
        last_col = (rid % GRID_W) == GRID_W - 1
    else:
        first_col = rid == 0
        last_col = rid == L - 1
    ii = lax.broadcasted_iota(jnp.int32, (L, L), 0)
    jj = lax.broadcasted_iota(jnp.int32, (L, L), 1)
    incl = (jj >= ii) if rev else (jj <= ii)
    tri = incl.astype(bf16)
    e_i = lax.broadcasted_iota(jnp.int32, (2 * ML_HEADS, 2 * ML_HEADS), 0)
    e_j = lax.broadcasted_iota(jnp.int32, (2 * ML_HEADS, 2 * ML_HEADS), 1)
    eye = (e_i == e_j).astype(bf16)
    one_col = (lax.broadcasted_iota(jnp.int32, (L, ML_DV), 1) == 0).astype(bf16)

    units = []
    for b in range(bsz):
        qk = qk_ref[b].astype(f32)
        if conv_done:
            q, k = qk[:, :ML_QK], qk[:, ML_QK:]
        else:
            taps = taps_ref[...]
            if grid_conv:
                above = jnp.where(jblk > 0, qkp_ref[b].astype(f32), 0.0)
                below = jnp.where(jblk < nb - 1, qkn_ref[b].astype(f32), 0.0)
                ext = jnp.concatenate([above, qk, below], axis=0)
                bases = [(dr, ext[dr * GRID_W:dr * GRID_W + L]) for dr in range(3)]
            else:
                bases = [(1, qk)]
            conv = cb_ref[...]
            for dr, base in bases:
                up, dn = _shift_rows(base, zero_row, zero_row)
                conv = conv + (taps[3 * dr:3 * dr + 1] * jnp.where(first_col, 0.0, up)
                               + taps[3 * dr + 1:3 * dr + 2] * base
                               + taps[3 * dr + 2:3 * dr + 3] * jnp.where(last_col, 0.0, dn))
            q = conv[:, :ML_QK]
            k = conv[:, ML_QK:] * (ML_DQK ** -0.5)
            qkc_ref[b] = jnp.concatenate([q, k], axis=1).astype(bf16)
        pre = mlg_ref[b][:, :4 * ML_HEADS].astype(f32) + gb_ref[...]
        pre = GATE_CAP * jnp.tanh(pre * (1.0 / GATE_CAP))
        ig = pre[:, d * ML_HEADS:(d + 1) * ML_HEADS]
        lf = _log_sigmoid(pre[:, (2 + d) * ML_HEADS:(3 + d) * ML_HEADS])
        bc = _mm_exact_lhs(tri, lf)
        rows_ = _mm_exact_lhs(eye, jnp.concatenate([ig, bc], axis=1), _NT)
        v = v_ref[b]
        for h in range(ML_HEADS):
            b_col = bc[:, h:h + 1]
            units.append(dict(
                b=b, h=h, q=q[:, h * ML_DQK:(h + 1) * ML_DQK], k=k[:, h * ML_DQK:(h + 1) * ML_DQK],
                vp=jnp.concatenate([v[:, h * ML_DV:(h + 1) * ML_DV], one_col], axis=1),
                b_col=b_col, ig_col=ig[:, h:h + 1], btot=b_col[0:1, :] if rev else b_col[L - 1:L, :],
                b_row=rows_[ML_HEADS + h:ML_HEADS + h + 1, :], ig_row=rows_[h:h + 1, :]))
    decay = [jnp.exp(jnp.where(incl, (u["b_col"] - u["b_row"]) + u["ig_row"], NEG_BIG)) for u in units]
    qk_s = [_mm(u["q"], u["k"], _NT) for u in units]
    qc = [_mm(u["q"], c_scr[u["b"], u["h"]]) for u in units]
    kw = [u["k"] * jnp.exp((u["btot"] - u["b_col"]) + u["ig_col"]) for u in units]
    kv = [_mm(kw[n], u["vp"], _TN) for n, u in enumerate(units)]
    sv = [_mm(qk_s[n] * decay[n], u["vp"]) for n, u in enumerate(units)]
    for n, u in enumerate(units):
        b, h = u["b"], u["h"]
        nd = sv[n] + jnp.exp(u["b_col"]) * qc[n]
        den = nd[:, ML_DV:ML_DV + 1]
        h_ref[b, :, h * ML_DV:(h + 1) * ML_DV] = (nd[:, :ML_DV] / jnp.maximum(jnp.abs(den), 1.0)).astype(bf16)
        c_scr[b, h] = jnp.exp(u["btot"]) * c_scr[b, h] + kv[n]
    cfin_ref[...] = c_scr[...]


def _mlstm(pfull, taps, conv_b, gate_b, c0, rev, d, grid_conv, qk_conv=None):
    bsz, t, _ = pfull.shape
    L = ML_CHUNK
    nb = t // L
    tb = (lambda j: nb - 1 - j) if rev else (lambda j: j)
    vblk = P_V // ML_V
    qkblk = P_QK // (2 * ML_QK)
    rpb = L // GRID_W
    n_rows = t // GRID_W
    conv_done = qk_conv is not None
    tok = lambda w, blk: pl.BlockSpec((bsz, L, w), lambda j: (0, tb(j), blk))
    full = lambda shape: pl.BlockSpec(shape, lambda j: (0,) * len(shape))
    st_spec = full((bsz, ML_HEADS, ML_DQK, 2 * ML_DV))
    common = [tok(ML_V, vblk), tok(P_MLG_BLK, P_MLG // P_MLG_BLK)]
    if conv_done:
        in_specs = [tok(2 * ML_QK, 0)] + common + [full((1, 4 * ML_HEADS)), st_spec]
        args = (qk_conv, pfull, pfull, gate_b, c0)
    else:
        in_specs = [
            tok(2 * ML_QK, qkblk),
            pl.BlockSpec((bsz, GRID_W, 2 * ML_QK), lambda j: (0, jnp.maximum(tb(j) * rpb - 1, 0), qkblk)),
            pl.BlockSpec((bsz, GRID_W, 2 * ML_QK),
                         lambda j: (0, jnp.minimum((tb(j) + 1) * rpb, n_rows - 1), qkblk)),
        ] + common + [full((9, 2 * ML_QK)), full((1, 2 * ML_QK)), full((1, 4 * ML_HEADS)), st_spec]
        args = (pfull, pfull, pfull, pfull, pfull, taps, conv_b, gate_b, c0)
    out_specs = [tok(ML_V, 0), st_spec]
    out_shape = [jax.ShapeDtypeStruct((bsz, t, ML_V), bf16),
                 jax.ShapeDtypeStruct((bsz, ML_HEADS, ML_DQK, 2 * ML_DV), f32)]
    if not conv_done:
        out_specs.append(tok(2 * ML_QK, 0))
        out_shape.append(jax.ShapeDtypeStruct((bsz, t, 2 * ML_QK), bf16))
    return pl.pallas_call(
        functools.partial(_mlstm_kernel, rev=rev, d=d, grid_conv=grid_conv, nb=nb, bsz=bsz,
                          conv_done=conv_done),
        grid=(nb,),
        in_specs=in_specs,
        out_specs=out_specs,
        out_shape=out_shape,
        scratch_shapes=[pltpu.VMEM((bsz, ML_HEADS, ML_DQK, 2 * ML_DV), f32)],
        compiler_params=pltpu.CompilerParams(
            dimension_semantics=("arbitrary",), vmem_limit_bytes=VMEM_LIMIT),
        name="mlstm_bwd" if rev else "mlstm_fwd",
    )(*args)


def _rw_prepare(p_ref, pp_ref, pn_ref, jblk, nb, d, par, scr, bv_ref):
    mu_ref, wup_ref, w0_ref, aup_ref, a0_ref, kk_ref, ka_ref, rk_ref, ones_ref, nbr_ref = par
    r_scr, v_scr, kk_scr, kd_scr, be_scr, ld_scr = scr
    tb_ = RW_BLOCK
    p_bf = p_ref[...]
    p = p_bf.astype(f32)
    nb_avg = _dot(nbr_ref[...], p_bf)
    rid8 = lax.broadcasted_iota(jnp.int32, (8, 1), 0)
    prev_row = jnp.where(jblk > 0, pp_ref[RW_HALO - 1:RW_HALO, :].astype(f32), 0.0)
    next_row = jnp.where(jblk < nb - 1, pn_ref[0:1, :].astype(f32), 0.0)
    nb_avg = jnp.concatenate([nb_avg[0:8] + jnp.where(rid8 == 0, 0.5 * prev_row, 0.0),
                              nb_avg[8:tb_ - 8],
                              nb_avg[tb_ - 8:] + jnp.where(rid8 == 7, 0.5 * next_row, 0.0)], axis=0)
    p = p + mu_ref[...] * (nb_avg - p)
    r = p[:, 0:RW_W]
    k = p[:, RW_W:2 * RW_W]
    v = p[:, 2 * RW_W:3 * RW_W]
    o_wd = 3 * RW_W + d * RW_DECAY_LORA
    o_ad = 3 * RW_W + 2 * RW_DECAY_LORA + d * RW_A_LORA
    lw = w0_ref[d:d + 1, :] + _mm(jnp.tanh(p[:, o_wd:o_wd + RW_DECAY_LORA]), wup_ref[d])
    ld_scr[...] = -EXP_NEG_HALF * _sigmoid(lw)
    a = _sigmoid(a0_ref[d:d + 1, :] + _mm(p[:, o_ad:o_ad + RW_A_LORA], aup_ref[d]))
    ones_bd = ones_ref[...]
    kkr = k * kk_ref[...]
    kk = kkr * lax.rsqrt(_head_sum(kkr * kkr, ones_bd) + 1e-12)
    kd = k * (1.0 + (a - 1.0) * ka_ref[...])
    bv_ref[...] = (_head_sum(r * kd * rk_ref[...], ones_bd) * v).astype(bf16)
    r_scr[...] = r
    v_scr[...] = v
    kk_scr[...] = kk
    kd_scr[...] = kd
    be_scr[...] = kk * a
    return p


def _rwkv_kernel(pf_ref, pfp_ref, pfn_ref, pb_ref, pbp_ref, pbn_ref, s0f_ref, s0b_ref,
                 mu_ref, wup_ref, w0_ref, aup_ref, a0_ref, kk_ref, ka_ref, rk_ref, ones_ref, gup_ref, nbr_ref,
                 yf_ref, bvf_ref, g_ref, yb_ref, bvb_ref, sff_ref, sfb_ref,
                 stf_scr, stb_scr, *scr, nb, bsz):
    L = RW_CHUNK
    nch = RW_BLOCK // L
    j = pl.program_id(0)

    @pl.when(j == 0)
    def _():
        stf_scr[...] = s0f_ref[...]
        stb_scr[...] = s0b_ref[...]

    par = (mu_ref, wup_ref, w0_ref, aup_ref, a0_ref, kk_ref, ka_ref, rk_ref, ones_ref, nbr_ref)
    scr_f, scr_b = scr[:6], scr[6:]
    o_gd = 3 * RW_W + 2 * RW_DECAY_LORA + 2 * RW_A_LORA
    streams = []
    for b in range(bsz):
        sf = tuple(s.at[b] for s in scr_f)
        sb_ = tuple(s.at[b] for s in scr_b)
        p_f = _rw_prepare(pf_ref.at[b], pfp_ref.at[b], pfn_ref.at[b], j, nb, 0, par, sf, bvf_ref.at[b])
        _rw_prepare(pb_ref.at[b], pbp_ref.at[b], pbn_ref.at[b], nb - 1 - j, nb, 1, par, sb_, bvb_ref.at[b])
        g_ref[b] = _mm(_sigmoid(p_f[:, o_gd:o_gd + RW_GATE_LORA]), gup_ref[...]).astype(bf16)
        streams.append((sf, stf_scr.at[b], yf_ref.at[b], False))
        streams.append((sb_, stb_scr.at[b], yb_ref.at[b], True))

    ii = lax.broadcasted_iota(jnp.int32, (L, L), 0)
    jj = lax.broadcasted_iota(jnp.int32, (L, L), 1)
    ii2 = lax.broadcasted_iota(jnp.int32, (L, 2 * L), 0)
    jj2 = lax.broadcasted_iota(jnp.int32, (L, 2 * L), 1) & (L - 1)
    masks = []
    for rev in (False, True):
        incl = (jj >= ii) if rev else (jj <= ii)
        strict = (jj > ii) if rev else (jj < ii)
        incl2 = (jj2 >= ii2) if rev else (jj2 <= ii2)
        masks.append((incl2, strict, incl.astype(bf16)))

    def chunk(i, carry):
        units = []
        rows_of = []
        for si, (sc, st_scr, y_ref, rev) in enumerate(streams):
            r_scr, v_scr, kk_scr, kd_scr, be_scr, ld_scr = sc
            incl2, strict, tri = masks[int(rev)]
            c = (nch - 1 - i) if rev else i
            rows = pl.ds(pl.multiple_of(c * L, L), L)
            rows_of.append(rows)
            ld = ld_scr[rows, :]
            b = _mm_exact_lhs(tri, ld)
            btot = b[0:1, :] if rev else b[L - 1:L, :]
            e_nb = jnp.exp(-b)
            e_end = jnp.exp(btot - b)
            kd_c = kd_scr[rows, :]
            be_c = be_scr[rows, :]
            v_c = v_scr[rows, :]
            aw = -kk_scr[rows, :] * jnp.exp(b - ld)
            rw = r_scr[rows, :] * jnp.exp(b)
            bi = be_c * e_nb
            ki = kd_c * e_nb
            bw = be_c * e_end
            kw = kd_c * e_end
            wend = jnp.exp(btot)
            for h in range(RW_HEADS):
                c_ = slice(h * RW_N, (h + 1) * RW_N)
                units.append(dict(si=si, h=h, incl2=incl2, strict=strict, st=st_scr, aw=aw[:, c_], rw=rw[:, c_],
                                  bi=bi[:, c_], ki=ki[:, c_], bw=bw[:, c_], kw=kw[:, c_], v=v_c[:, c_],
                                  wend=wend[:, c_]))
        us = range(len(units))
        m = [_mm(jnp.concatenate([u["aw"], u["rw"]], axis=0),
                 jnp.concatenate([u["bi"], u["ki"]], axis=0), _NT) for u in units]
        pw = [jnp.where(units[n]["strict"], m[n][:L, :L], 0.0) for n in us]
        mak = [jnp.where(units[n]["strict"], m[n][:L, L:], 0.0) for n in us]
        mr = [jnp.where(units[n]["incl2"], m[n][L:, :], 0.0) for n in us]
        mv = [_mm(mak[n], units[n]["v"]) for n in us]
        x = [jnp.concatenate([units[n]["aw"], mv[n]], axis=1) for n in us]
        for it in range(6):
            if it < 5:
                pr = [_mm(pw[n], jnp.concatenate([x[n], pw[n]], axis=1)) for n in us]
                pw = [pr[n][:, 2 * RW_N:] for n in us]
                x = [x[n] + pr[n][:, :2 * RW_N] for n in us]
            else:
                pr = [_mm(pw[n], x[n]) for n in us]
                x = [x[n] + pr[n] for n in us]
        zero = jnp.zeros((L, RW_N), f32)
        ray = [_mm(mr[n], jnp.concatenate(
            [x[n], jnp.concatenate([zero, units[n]["v"]], axis=1)], axis=0)) for n in us]
        gt = [_mm(x[n], units[n]["bw"], _TN) for n in us]
        vk = [_mm(units[n]["v"], units[n]["kw"], _TN) for n in us]
        s_old = [units[n]["st"][units[n]["h"]] for n in us]
        yy = [_mm(units[n]["rw"] + ray[n][:, :RW_N], s_old[n], _NT) for n in us]
        sg = [_mm(s_old[n], gt[n][:RW_N]) for n in us]
        for si, (sc, st_scr, y_ref, rev) in enumerate(streams):
            mine = [n for n in us if units[n]["si"] == si]
            y_ref[rows_of[si], :] = jnp.concatenate(
                [yy[n] + ray[n][:, RW_N:] for n in mine], axis=1).astype(bf16)
            for n in mine:
                st_scr[units[n]["h"]] = (s_old[n] * units[n]["wend"] + sg[n]) + (gt[n][RW_N:] + vk[n])
        return carry

    lax.fori_loop(0, nch, chunk, 0)
    sff_ref[...] = stf_scr[...]
    sfb_ref[...] = stb_scr[...]


def _rwkv(pfull, s0f, s0b, mu, wup, w0, aup, a0, k_k, k_a, r_k, ones_bd, gup, nbr):
    bsz, t, _ = pfull.shape
    tb_ = RW_BLOCK
    nb = t // tb_
    cblk = P_RW // P_RW_PAD
    hpb = tb_ // RW_HALO
    nh = t // RW_HALO
    fwd = lambda j: j
    bwd = lambda j: nb - 1 - j

    def p_specs(tb):
        return [
            pl.BlockSpec((bsz, tb_, P_RW_PAD), lambda j: (0, tb(j), cblk)),
            pl.BlockSpec((bsz, RW_HALO, P_RW_PAD), lambda j: (0, jnp.maximum(tb(j) * hpb - 1, 0), cblk)),
            pl.BlockSpec((bsz, RW_HALO, P_RW_PAD), lambda j: (0, jnp.minimum((tb(j) + 1) * hpb, nh - 1), cblk)),
        ]

    full = lambda shape: pl.BlockSpec(shape, lambda j: (0,) * len(shape))
    st_spec = full((bsz, RW_HEADS, RW_N, RW_N))
    tok = lambda tb: pl.BlockSpec((bsz, tb_, RW_W), lambda j: (0, tb(j), 0))
    tok_shape = jax.ShapeDtypeStruct((bsz, t, RW_W), bf16)
    st_shape = jax.ShapeDtypeStruct((bsz, RW_HEADS, RW_N, RW_N), f32)
    return pl.pallas_call(
        functools.partial(_rwkv_kernel, nb=nb, bsz=bsz),
        grid=(nb,),
        in_specs=p_specs(fwd) + p_specs(bwd) + [
            st_spec, st_spec,
            full((1, P_RW_PAD)),
            full((2, RW_DECAY_LORA, RW_W)), full((2, RW_W)),
            full((2, RW_A_LORA, RW_W)), full((2, RW_W)),
            full((1, RW_W)), full((1, RW_W)), full((1, RW_W)),
            full((RW_W, RW_W)),
            full((RW_GATE_LORA, RW_W)),
            full((tb_, tb_)),
        ],
        out_specs=[tok(fwd), tok(fwd), tok(fwd), tok(bwd), tok(bwd), st_spec, st_spec],
        out_shape=[tok_shape] * 5 + [st_shape] * 2,
        scratch_shapes=[pltpu.VMEM((bsz, RW_HEADS, RW_N, RW_N), f32) for _ in range(2)]
        + [pltpu.VMEM((bsz, tb_, RW_W), f32) for _ in range(12)],
        compiler_params=pltpu.CompilerParams(
            dimension_semantics=("arbitrary",), vmem_limit_bytes=VMEM_LIMIT),
        name="rwkv",
    )(pfull, pfull, pfull, pfull, pfull, pfull, s0f, s0b, mu, wup, w0, aup, a0, k_k, k_a, r_k, ones_bd, gup, nbr)


def _merge_kernel(x_ref, hf_ref, hb_ref, o_ref_, yf_ref, yb_ref, bvf_ref, bvb_ref, g_ref, ga_ref, gb_ref,
                  g1_ref, mlg_ref, lnw_ref, lnb_ref, ones_ref, pa_ref, pb_ref, wo_ref, out_ref):
    hsum = hf_ref[...].astype(f32) + hb_ref[...].astype(f32)
    parts = []
    for h in range(ML_HEADS):
        hh = hsum[:, h * ML_DV:(h + 1) * ML_DV]
        parts.append(hh * lax.rsqrt(jnp.mean(hh * hh, axis=-1, keepdims=True) + NORM_EPS))
    a_lat = jnp.concatenate(parts, axis=1) * mlg_ref[...] * _sigmoid(o_ref_[...].astype(f32))
    y = yf_ref[...].astype(f32) + yb_ref[...].astype(f32)
    ones_bd = ones_ref[...]
    inv_n = 1.0 / RW_N
    mu = _head_sum(y, ones_bd) * inv_n
    dev = y - mu
    var = _head_sum(dev * dev, ones_bd) * inv_n
    yn = dev * lax.rsqrt(var + RW_GN_EPS) * lnw_ref[...] + lnb_ref[...]
    b_lat = (yn + (bvf_ref[...].astype(f32) + bvb_ref[...].astype(f32))) * g_ref[...].astype(f32)
    m = (_sigmoid(ga_ref[...].astype(f32)) * _mm(a_lat, pa_ref[...])
         + _sigmoid(gb_ref[...].astype(f32)) * _mm(b_lat, pb_ref[...]))
    out_ref[...] = x_ref[...] + g1_ref[...] * _mm(m, wo_ref[...])


def _merge(x, hf, hb, yf, yb, bvf, bvb, g, pfull, g1, ml_norm_g, ln_w, ln_b, ones_bd, pa_bf, pb_bf, wo_bf):
    bsz, t, _ = x.shape
    tm = min(t, 512)
    full = lambda shape: pl.BlockSpec(shape, lambda b, i: (0,) * len(shape))
    tok = lambda w: pl.BlockSpec((None, tm, w), lambda b, i: (b, i, 0))
    col = lambda w, off: pl.BlockSpec((None, tm, w), lambda b, i: (b, i, off // w))
    return pl.pallas_call(
        _merge_kernel,
        grid=(bsz, t // tm),
        in_specs=[
            tok(D_MODEL), tok(ML_V), tok(ML_V), col(ML_V, P_O),
            tok(RW_W), tok(RW_W), tok(RW_W), tok(RW_W), tok(RW_W),
            col(D_MODEL, P_GA), col(D_MODEL, P_GB),
            pl.BlockSpec((None, 1, D_MODEL), lambda b, i: (b, 0, 0)),
            full((1, ML_V)), full((1, RW_W)), full((1, RW_W)), full((RW_W, RW_W)),
            full((ML_V, D_MODEL)), full((RW_W, D_MODEL)), full((D_MODEL, D_MODEL)),
        ],
        out_specs=tok(D_MODEL),
        out_shape=jax.ShapeDtypeStruct((bsz, t, D_MODEL), f32),
        compiler_params=pltpu.CompilerParams(
            dimension_semantics=("parallel", "parallel"), vmem_limit_bytes=VMEM_LIMIT),
        name="merge",
    )(x, hf, hb, pfull, yf, yb, bvf, bvb, g, pfull, pfull, g1, ml_norm_g, ln_w, ln_b, ones_bd,
      pa_bf, pb_bf, wo_bf)


def _lane_max(x):
    return jnp.max(x, axis=-1, keepdims=True)


def _first_at(x, val, lane_f):
    return jnp.min(jnp.where(x == val, lane_f, float(ROUTE_W)), axis=-1, keepdims=True)


def _route_kernel(x_ref, g_ref, sc_ref, sh_ref, rw_ref, rb_ref, rt_ref):
    h = _norm_mod(x_ref[...], g_ref[...], sc_ref[...], sh_ref[...])
    scores = _sigmoid(_mm3(h, rw_ref[...]))
    sel = scores + rb_ref[...]
    lane = lax.broadcasted_iota(jnp.int32, sel.shape, 1)
    lane_f = lane.astype(f32)
    grp = lane // EXPERTS_PER_GROUP
    neg = -jnp.inf
    best_g = jnp.zeros(sel.shape[:1] + (1,), jnp.int32)
    best_v = None
    for gi in range(N_GROUPS):
        mg = jnp.where(grp == gi, sel, neg)
        m1 = _lane_max(mg)
        i1 = _first_at(mg, m1, lane_f)
        m2 = _lane_max(jnp.where(lane_f == i1, neg, mg))
        gs = m1 + m2
        if gi == 0:
            best_v = gs
        else:
            upd = gs > best_v
            best_g = jnp.where(upd, gi, best_g)
            best_v = jnp.where(upd, gs, best_v)
    cand = jnp.where(grp == best_g, sel, neg)
    v1 = _lane_max(cand)
    i1 = _first_at(cand, v1, lane_f)
    cand2 = jnp.where(lane_f == i1, neg, cand)
    v2 = _lane_max(cand2)
    i2 = _first_at(cand2, v2, lane_f)
    picked = (lane_f == i1) | (lane_f == i2)
    w = jnp.where(picked, scores, 0.0)
    comb = w / jnp.sum(w, axis=-1, keepdims=True)
    rt_ref[...] = jnp.where(lane == ROUTE_GID, best_g.astype(f32), comb)


def _route(x, g, sc, sh, router_w, router_b):
    bsz, t, _ = x.shape
    tm = min(t, 512)
    pad = ROUTE_W - N_EXPERTS
    rw = jnp.pad(router_w, ((0, 0), (0, pad)))
    rb = jnp.pad(router_b.reshape(1, N_EXPERTS), ((0, 0), (0, pad)))
    return pl.pallas_call(
        _route_kernel,
        grid=(bsz, t // tm),
        in_specs=[
            pl.BlockSpec((None, tm, D_MODEL), lambda b, i: (b, i, 0)),
            pl.BlockSpec((1, D_MODEL), lambda b, i: (0, 0)),
            pl.BlockSpec((None, 1, D_MODEL), lambda b, i: (b, 0, 0)),
            pl.BlockSpec((None, 1, D_MODEL), lambda b, i: (b, 0, 0)),
            pl.BlockSpec((D_MODEL, ROUTE_W), lambda b, i: (0, 0)),
            pl.BlockSpec((1, ROUTE_W), lambda b, i: (0, 0)),
        ],
        out_specs=pl.BlockSpec((None, tm, ROUTE_W), lambda b, i: (b, i, 0)),
        out_shape=jax.ShapeDtypeStruct((bsz, t, ROUTE_W), f32),
        compiler_params=pltpu.CompilerParams(
            dimension_semantics=("parallel", "parallel"), vmem_limit_bytes=VMEM_LIMIT),
        name="route",
    )(x, g, sc, sh, rw, rb)


def _moe_kernel(meta_ref, x_ref, g_ref, sc_ref, sh_ref, g2_ref, rt_ref, fg_ref, wg_ref, wu_ref, wd_ref, o_ref,
                xs_scr, ys_scr, cs_scr, pos_scr, *, final_norm, tm, nt):
    sb = MOE_SB
    n_rows = tm + 2 * sb
    grp = pl.program_id(2)
    base = (pl.program_id(0) * nt + pl.program_id(1)) * (2 * N_GROUPS)

    @pl.when(grp == 0)
    def _():
        h = _norm_mod(x_ref[...], g_ref[...], sc_ref[...], sh_ref[...]).astype(bf16)
        rt = rt_ref[...]
        lane = lax.broadcasted_iota(jnp.int32, rt.shape, 1)
        gid = jnp.sum(jnp.where(lane == ROUTE_GID, rt, 0.0), axis=-1, keepdims=True)
        g8 = lax.broadcasted_iota(jnp.int32, (tm, 8), 1)
        onehot = gid == g8.astype(f32)
        ii = lax.broadcasted_iota(jnp.int32, (tm, tm), 0)
        jj = lax.broadcasted_iota(jnp.int32, (tm, tm), 1)
        rank = _dot((jj <= ii).astype(bf16), onehot.astype(bf16))
        start = jnp.zeros((1, 8), f32)
        g8r = lax.broadcasted_iota(jnp.int32, (1, 8), 1)
        for gi in range(N_GROUPS):
            start = jnp.where(g8r == gi, meta_ref[base + gi].astype(f32), start)
        pos = jnp.sum(jnp.where(onehot, (start + rank) - 1.0, 0.0), axis=-1, keepdims=True)
        posmat = jnp.broadcast_to(pos, (tm, 8))
        pos_scr[...] = posmat
        e_i = lax.broadcasted_iota(jnp.int32, (8, 8), 0)
        e_j = lax.broadcasted_iota(jnp.int32, (8, 8), 1)
        pos_row = _mm_exact_lhs((e_i == e_j).astype(bf16), posmat, _NT)[0:1, :]
        hi = rt.astype(bf16)
        r1 = rt - hi.astype(f32)
        mid = r1.astype(bf16)
        lo = (r1 - mid.astype(f32)).astype(bf16)
        rt3 = jnp.concatenate([hi, mid, lo], axis=1)
        for c in range(n_rows // sb):
            rid = (lax.broadcasted_iota(jnp.int32, (sb, 1), 0) + c * sb).astype(f32)
            perm = (rid == pos_row).astype(bf16)
            xs_scr[c * sb:(c + 1) * sb, :] = _dot(perm, h).astype(bf16)
            cc = _dot(perm, rt3)
            cs_scr[c * sb:(c + 1) * sb, :] = cc[:, :ROUTE_W] + (cc[:, ROUTE_W:2 * ROUTE_W] + cc[:, 2 * ROUTE_W:])
        ys_scr[...] = jnp.zeros_like(ys_scr)

    seg_start = meta_ref[base + grp]
    seg_blocks = meta_ref[base + N_GROUPS + grp]
    lane_sb = lax.broadcasted_iota(jnp.int32, (sb, ROUTE_W), 1)

    def block(kb, carry):
        rows = pl.ds(pl.multiple_of(seg_start + kb * sb, MOE_ALIGN), sb)
        xs = xs_scr[rows, :]
        cs = cs_scr[rows, :]
        acc = None
        for k in range(EXPERTS_PER_GROUP):
            gate = _dot(xs, wg_ref[k])
            hid = (gate * _sigmoid(gate)) * _dot(xs, wu_ref[k])
            ce = jnp.sum(jnp.where(lane_sb == grp * EXPERTS_PER_GROUP + k, cs, 0.0), axis=-1, keepdims=True)
            part = _dot((ce * hid).astype(bf16), wd_ref[k])
            acc = part if acc is None else acc + part
        ys_scr[rows, :] = acc.astype(bf16)
        return carry

    lax.fori_loop(0, seg_blocks, block, 0)

    @pl.when(grp == N_GROUPS - 1)
    def _():
        cid = lax.broadcasted_iota(jnp.int32, (1, n_rows), 1).astype(f32)
        chunks = [slice(c * sb, (c + 1) * sb) for c in range(tm // sb)]
        ys = [_dot((pos_scr[rows, 0:1] == cid).astype(bf16), ys_scr[...]) for rows in chunks]
        ys = [x_ref[rows, :] + g2_ref[...] * ys[c] for c, rows in enumerate(chunks)]
        if final_norm:
            ms = [jnp.mean(y * y, axis=-1, keepdims=True) for y in ys]
            ys = [y * lax.rsqrt(ms[c] + NORM_EPS) * fg_ref[...] for c, y in enumerate(ys)]
        for c, rows in enumerate(chunks):
            o_ref[rows, :] = ys[c]


def _moe(x, g, sc, sh, g2, rt, final_g, final_norm, w_gate, w_up, w_down):
    bsz, t, _ = x.shape
    tm = min(t, 1024)
    nt = t // tm
    sb = MOE_SB
    n_rows = tm + 2 * sb
    gid = rt[..., ROUTE_GID].astype(jnp.int32).reshape(bsz, nt, tm)
    cnt = jnp.sum(gid[..., None] == jnp.arange(N_GROUPS), axis=2).astype(jnp.int32)
    nblk = (cnt + (sb - 1)) // sb
    seg = ((cnt + (MOE_ALIGN - 1)) // MOE_ALIGN) * MOE_ALIGN
    start = jnp.cumsum(seg, axis=-1) - seg
    meta = jnp.concatenate([start, nblk], axis=-1).reshape(-1).astype(jnp.int32)
    wspec = lambda shape: pl.BlockSpec((EXPERTS_PER_GROUP,) + shape, lambda b, i, e, m: (e, 0, 0))
    grid_spec = pltpu.PrefetchScalarGridSpec(
        num_scalar_prefetch=1,
        grid=(bsz, nt, N_GROUPS),
        in_specs=[
            pl.BlockSpec((None, tm, D_MODEL), lambda b, i, e, m: (b, i, 0)),
            pl.BlockSpec((1, D_MODEL), lambda b, i, e, m: (0, 0)),
            pl.BlockSpec((None, 1, D_MODEL), lambda b, i, e, m: (b, 0, 0)),
            pl.BlockSpec((None, 1, D_MODEL), lambda b, i, e, m: (b, 0, 0)),
            pl.BlockSpec((None, 1, D_MODEL), lambda b, i, e, m: (b, 0, 0)),
            pl.BlockSpec((None, tm, ROUTE_W), lambda b, i, e, m: (b, i, 0)),
            pl.BlockSpec((1, D_MODEL), lambda b, i, e, m: (0, 0)),
            wspec((D_MODEL, D_EXPERT)), wspec((D_MODEL, D_EXPERT)), wspec((D_EXPERT, D_MODEL)),
        ],
        out_specs=pl.BlockSpec((None, tm, D_MODEL), lambda b, i, e, m: (b, i, 0)),
        scratch_shapes=[pltpu.VMEM((n_rows, D_MODEL), bf16), pltpu.VMEM((n_rows, D_MODEL), bf16),
                        pltpu.VMEM((n_rows, ROUTE_W), f32), pltpu.VMEM((tm, 8), f32)],
    )
    return pl.pallas_call(
        functools.partial(_moe_kernel, final_norm=final_norm, tm=tm, nt=nt),
        grid_spec=grid_spec,
        out_shape=jax.ShapeDtypeStruct((bsz, t, D_MODEL), f32),
        compiler_params=pltpu.CompilerParams(
            dimension_semantics=("parallel", "parallel", "arbitrary"), vmem_limit_bytes=MOE_VMEM_LIMIT),
        name="moe",
    )(meta, x, g, sc, sh, g2, rt, final_g, w_gate, w_up, w_down)


def _pack_w_in(w_in):
    ml, rw, gt = w_in[:, :ML_COLS], w_in[:, ML_COLS:ML_COLS + RW_COLS], w_in[:, ML_COLS + RW_COLS:]
    qkvo, mlg = ml[:, :2 * ML_QK + 2 * ML_V], ml[:, 2 * ML_QK + 2 * ML_V:]
    z = lambda n: jnp.zeros((D_MODEL, n), w_in.dtype)
    return jnp.concatenate(
        [gt, rw, mlg, z(P_RW_PAD - RW_COLS - 4 * ML_HEADS), qkvo], axis=1).astype(bf16)


def _mixer(pfull, is_ctx, lp, ml_state, rw_state):
    conv_args = (pfull, lp["taps"], lp["conv_b"], lp["gate_b"])
    h_f, c_f, qk_conv = _mlstm(*conv_args, ml_state[0], rev=False, d=0, grid_conv=not is_ctx)
    h_b, c_b = _mlstm(*conv_args, ml_state[1], rev=True, d=1, grid_conv=not is_ctx, qk_conv=qk_conv)
    hs, ml_fin = [h_f, h_b], [c_f, c_b]
    yf, bvf, g, yb, bvb, sff, sfb = _rwkv(pfull, rw_state[0], rw_state[1], lp["rw_mu"], lp["rw_w_up"],
                                          lp["rw_w0"], lp["rw_a_up"], lp["rw_a0"], lp["rw_k_k"], lp["rw_k_a"],
                                          lp["rw_r_k"], lp["ones_bd"], lp["gup"], lp["nbr"])
    return (hs[0], hs[1], yf, yb, bvf, bvb, g), ml_fin, [sff, sfb]


def kernel(x, c, ctx, c_ctx, w_ada, b_ada, norm1_g, norm2_g, w_in, ml_conv_k, ml_conv_b, ml_gate_b, ml_norm_g, rw_mu, rw_w_up, rw_w0, rw_a_up, rw_a0, rw_g_up, rw_k_k, rw_k_a, rw_r_k, rw_ln_w, rw_ln_b, merge_pa, merge_pb, w_out, router_w, router_b, exp_w_gate, exp_w_up, exp_w_down, final_g):
    bsz = x.shape[0]
    s_rows = jnp.zeros((8, D_MODEL), f32).at[:bsz].set(c).at[bsz].set(c_ctx)
    mod = _ada(s_rows, w_ada, b_ada)
    head_id = jnp.arange(RW_W) // RW_N
    ones_bd = (head_id[:, None] == head_id[None, :]).astype(bf16)
    tok_id = jnp.arange(RW_BLOCK)
    nbr = (0.5 * (jnp.abs(tok_id[:, None] - tok_id[None, :]) == 1)).astype(bf16)
    row = lambda v: v.reshape(1, -1)

    x_lat, x_ctx = x, ctx
    for l in range(DEPTH):
        last = l == DEPTH - 1
        mod_lat = mod[l, :bsz].reshape(bsz, 1, N_MOD, D_MODEL)
        mod_ctx = jnp.broadcast_to(mod[l, bsz].reshape(1, 1, N_MOD, D_MODEL), (bsz, 1, N_MOD, D_MODEL))
        lp = dict(taps=ml_conv_k[l].reshape(9, 2 * ML_QK), conv_b=row(ml_conv_b[l]), gate_b=row(ml_gate_b[l]),
                  rw_mu=jnp.pad(row(rw_mu[l]), ((0, 0), (0, P_RW_PAD - RW_COLS))),
                  rw_w_up=rw_w_up[l].astype(bf16), rw_w0=rw_w0[l], rw_a_up=rw_a_up[l].astype(bf16),
                  rw_a0=rw_a0[l], rw_k_k=row(rw_k_k[l]), rw_k_a=row(rw_k_a[l]), rw_r_k=row(rw_r_k[l]),
                  ones_bd=ones_bd, gup=rw_g_up[l].astype(bf16), nbr=nbr)
        w_in_bf = _pack_w_in(w_in[l])
        pa_bf, pb_bf, wo_bf = merge_pa[l].astype(bf16), merge_pb[l].astype(bf16), w_out[l].astype(bf16)
        experts = (exp_w_gate[l].astype(bf16), exp_w_up[l].astype(bf16), exp_w_down[l].astype(bf16))
        g1n, g2n = row(norm1_g[l]), row(norm2_g[l])
        readout = (row(ml_norm_g[l]), row(rw_ln_w[l]), row(rw_ln_b[l]), ones_bd, pa_bf, pb_bf, wo_bf)

        def m(modv, i):
            return modv[:, :, i]

        p_ctx = _proj(x_ctx, g1n, m(mod_ctx, 1), m(mod_ctx, 0), w_in_bf)
        ml0 = [jnp.zeros((bsz, ML_HEADS, ML_DQK, 2 * ML_DV), f32)] * 2
        rw0 = [jnp.zeros((bsz, RW_HEADS, RW_N, RW_N), f32)] * 2
        mix_c, ml_st, rw_st = _mixer(p_ctx, True, lp, ml0, rw0)

        p_lat = _proj(x_lat, g1n, m(mod_lat, 1), m(mod_lat, 0), w_in_bf)
        mix_l, _, _ = _mixer(p_lat, False, lp, ml_st, rw_st)
        x_lat = _merge(x_lat, *mix_l, p_lat, m(mod_lat, 2), *readout)
        comb = _route(x_lat, g2n, m(mod_lat, 4), m(mod_lat, 3), router_w, router_b)
        x_lat = _moe(x_lat, g2n, m(mod_lat, 4), m(mod_lat, 3), m(mod_lat, 5), comb, row(final_g), last, *experts)
        if not last:
            x_ctx = _merge(x_ctx, *mix_c, p_ctx, m(mod_ctx, 2), *readout)
            comb_c = _route(x_ctx, g2n, m(mod_ctx, 4), m(mod_ctx, 3), router_w, router_b)
            x_ctx = _moe(x_ctx, g2n, m(mod_ctx, 4), m(mod_ctx, 3), m(mod_ctx, 5), comb_c, row(final_g), False,
                         *experts)
    return x_lat
```

```python
import functools

import jax
import jax.numpy as jnp
import numpy as np
from jax import lax
from jax.experimental import pallas as pl
from jax.experimental.pallas import tpu as pltpu

f32 = jnp.float32
bf16 = jnp.bfloat16

D_MODEL = 1024
DEPTH = 2
GRID_W = 64
N_MOD = 6
NORM_EPS = 1e-6

ML_HEADS = 4
ML_DQK = 64
ML_DV = 128
ML_QK = ML_HEADS * ML_DQK
ML_V = ML_HEADS * ML_DV
GATE_CAP = 15.0
ML_COLS = 2 * ML_QK + 2 * ML_V + 4 * ML_HEADS
ML_CHUNK = 256

RW_HEADS = 8
RW_N = 64
RW_W = RW_HEADS * RW_N
RW_DECAY_LORA = 64
RW_A_LORA = 64
RW_GATE_LORA = 128
RW_GN_EPS = 6.4e-4
RW_COLS = 3 * RW_W + 2 * RW_DECAY_LORA + 2 * RW_A_LORA + RW_GATE_LORA
RW_CHUNK = 64
RW_BLOCK = 256
RW_HALO = 16

N_EXPERTS = 16
N_GROUPS = 4
EXPERTS_PER_GROUP = N_EXPERTS // N_GROUPS
D_EXPERT = 512

P_GA = 0
P_GB = D_MODEL
P_RW = 2 * D_MODEL
P_RW_PAD = 2048
P_MLG = P_RW + RW_COLS
P_MLG_BLK = 128
P_QK = P_RW + P_RW_PAD
P_V = P_QK + 2 * ML_QK
P_O = P_V + ML_V
P_COLS = P_O + ML_V
PROJ_TN = P_COLS // 2

VMEM_LIMIT = 48 * 1024 * 1024
MOE_VMEM_LIMIT = 58 * 1024 * 1024
ROUTE_W = 32
ROUTE_GID = N_EXPERTS
MOE_SB = 256
MOE_ALIGN = 16
EXP_NEG_HALF = float(np.exp(-0.5))
NEG_BIG = -1e30


_NN = ((1,), (0,))
_NT = ((1,), (1,))
_TN = ((0,), (0,))


def _dot(a, b, dims=_NN):
    return lax.dot_general(a, b, (dims, ((), ())), preferred_element_type=f32)


def _mm(a, b, dims=_NN):
    return _dot(a.astype(bf16), b.astype(bf16), dims)


def _hi_lo(x):
    hi = x.astype(bf16)
    lo = (x - hi.astype(f32)).astype(bf16)
    return hi, lo


def _mm3(a, b, dims=_NN):
    ah, al = _hi_lo(a)
    bh, bl = _hi_lo(b)
    return _dot(ah, bh, dims) + (_dot(ah, bl, dims) + _dot(al, bh, dims))


def _mm_exact_lhs(a_bf, b, dims=_NN):
    hi = b.astype(bf16)
    r1 = b - hi.astype(f32)
    mid = r1.astype(bf16)
    lo = (r1 - mid.astype(f32)).astype(bf16)
    return _dot(a_bf, hi, dims) + (_dot(a_bf, mid, dims) + _dot(a_bf, lo, dims))


def _head_sum(a, ones_bd):
    return _dot(a.astype(bf16), ones_bd)


def _sigmoid(x):
    return 1.0 / (1.0 + jnp.exp(-x))


def _norm_mod(x, g, sc, sh):
    y = x * lax.rsqrt(jnp.mean(x * x, axis=-1, keepdims=True) + NORM_EPS)
    return (y * g) * (1.0 + sc) + sh


def _ada_kernel(s_ref, w_ref, b_ref, o_ref):
    s = s_ref[...]
    s = s * _sigmoid(s)
    o_ref[...] = _mm3(s, w_ref[...]) + b_ref[...]


def _ada(s_rows, w_ada, b_ada):
    tn = 1536
    n = N_MOD * D_MODEL
    return pl.pallas_call(
        _ada_kernel,
        grid=(DEPTH, n // tn),
        in_specs=[
            pl.BlockSpec((8, D_MODEL), lambda l, j: (0, 0)),
            pl.BlockSpec((None, D_MODEL, tn), lambda l, j: (l, 0, j)),
            pl.BlockSpec((None, 1, tn), lambda l, j: (l, 0, j)),
        ],
        out_specs=pl.BlockSpec((None, 8, tn), lambda l, j: (l, 0, j)),
        out_shape=jax.ShapeDtypeStruct((DEPTH, 8, n), f32),
        compiler_params=pltpu.CompilerParams(
            dimension_semantics=("arbitrary", "arbitrary"), vmem_limit_bytes=VMEM_LIMIT),
        name="ada",
    )(s_rows, w_ada, b_ada.reshape(DEPTH, 1, n))


def _proj_kernel(x_ref, g_ref, sc_ref, sh_ref, w_ref, o_ref, h_scr):
    @pl.when(pl.program_id(2) == 0)
    def _():
        h_scr[...] = _norm_mod(x_ref[...], g_ref[...], sc_ref[...], sh_ref[...]).astype(bf16)

    o_ref[...] = _dot(h_scr[...], w_ref[...]).astype(bf16)


def _proj(x, g, sc, sh, w_bf):
    bsz, t, _ = x.shape
    tm = min(t, 1024)
    return pl.pallas_call(
        _proj_kernel,
        grid=(bsz, t // tm, P_COLS // PROJ_TN),
        in_specs=[
            pl.BlockSpec((None, tm, D_MODEL), lambda b, i, j: (b, i, 0)),
            pl.BlockSpec((1, D_MODEL), lambda b, i, j: (0, 0)),
            pl.BlockSpec((None, 1, D_MODEL), lambda b, i, j: (b, 0, 0)),
            pl.BlockSpec((None, 1, D_MODEL), lambda b, i, j: (b, 0, 0)),
            pl.BlockSpec((D_MODEL, PROJ_TN), lambda b, i, j: (0, j)),
        ],
        out_specs=pl.BlockSpec((None, tm, PROJ_TN), lambda b, i, j: (b, i, j)),
        out_shape=jax.ShapeDtypeStruct((bsz, t, P_COLS), bf16),
        scratch_shapes=[pltpu.VMEM((tm, D_MODEL), bf16)],
        compiler_params=pltpu.CompilerParams(
            dimension_semantics=("parallel", "parallel", "arbitrary"), vmem_limit_bytes=VMEM_LIMIT),
        name="proj",
    )(x, g, sc, sh, w_bf)


def _shift_rows(u, up_row, dn_row):
    n = u.shape[0]
    rid = lax.broadcasted_iota(jnp.int32, (n, 1), 0)
    up = jnp.where(rid == 0, up_row, pltpu.roll(u, 1, axis=0))
    dn = jnp.where(rid == n - 1, dn_row, pltpu.roll(u, n - 1, axis=0))
    return up, dn


def _log_sigmoid(x):
    return jnp.minimum(x, 0.0) - jnp.log1p(jnp.exp(-jnp.abs(x)))


def _mlstm_kernel(*refs, rev, d, grid_conv, nb, bsz, conv_done):
    if conv_done:
        qk_ref, v_ref, mlg_ref, gb_ref, c0_ref, h_ref, cfin_ref, c_scr = refs
    else:
        (qk_ref, qkp_ref, qkn_ref, v_ref, mlg_ref, taps_ref, cb_ref, gb_ref, c0_ref,
         h_ref, cfin_ref, qkc_ref, c_scr) = refs
    L = ML_CHUNK
    j = pl.program_id(0)
    jblk = (nb - 1 - j) if rev else j

    @pl.when(j == 0)
    def _():
        c_scr[...] = c0_ref[...]

    zero_row = jnp.zeros((1, 2 * ML_QK), f32)
    rid = lax.broadcasted_iota(jnp.int32, (L, 1), 0)
    if grid_conv:
        first_col = (rid % GRID_W) == 0
        last_col = (rid % GRID_W) == GRID_W - 1
    else:
        first_col = rid == 0
        last_col = rid == L - 1
    ii = lax.broadcasted_iota(jnp.int32, (L, L), 0)
    jj = lax.broadcasted_iota(jnp.int32, (L, L), 1)
    incl = (jj >= ii) if rev else (jj <= ii)
    tri = incl.astype(bf16)
    e_i = lax.broadcasted_iota(jnp.int32, (2 * ML_HEADS, 2 * ML_HEADS), 0)
    e_j = lax.broadcasted_iota(jnp.int32, (2 * ML_HEADS, 2 * ML_HEADS), 1)
    eye = (e_i == e_j).astype(bf16)
    one_col = (lax.broadcasted_iota(jnp.int32, (L, ML_DV), 1) == 0).astype(bf16)

    units = []
    for b in range(bsz):
        qk = qk_ref[b].astype(f32)
        if conv_done:
            q, k = qk[:, :ML_QK], qk[:, ML_QK:]
        else:
            taps = taps_ref[...]
            if grid_conv:
                above = jnp.where(jblk > 0, qkp_ref[b].astype(f32), 0.0)
                below = jnp.where(jblk < nb - 1, qkn_ref[b].astype(f32), 0.0)
                ext = jnp.concatenate([above, qk, below], axis=0)
                bases = [(dr, ext[dr * GRID_W:dr * GRID_W + L]) for dr in range(3)]
            else:
                bases = [(1, qk)]
            conv = cb_ref[...]
            for dr, base in bases:
                up, dn = _shift_rows(base, zero_row, zero_row)
                conv = conv + (taps[3 * dr:3 * dr + 1] * jnp.where(first_col, 0.0, up)
                               + taps[3 * dr + 1:3 * dr + 2] * base
                               + taps[3 * dr + 2:3 * dr + 3] * jnp.where(last_col, 0.0, dn))
            q = conv[:, :ML_QK]
            k = conv[:, ML_QK:] * (ML_DQK ** -0.5)
            qkc_ref[b] = jnp.concatenate([q, k], axis=1).astype(bf16)
        pre = mlg_ref[b][:, :4 * ML_HEADS].astype(f32) + gb_ref[...]
        pre = GATE_CAP * jnp.tanh(pre * (1.0 / GATE_CAP))
        ig = pre[:, d * ML_HEADS:(d + 1) * ML_HEADS]
        lf = _log_sigmoid(pre[:, (2 + d) * ML_HEADS:(3 + d) * ML_HEADS])
        bc = _mm_exact_lhs(tri, lf)
        rows_ = _mm_exact_lhs(eye, jnp.concatenate([ig, bc], axis=1), _NT)
        v = v_ref[b]
        for h in range(ML_HEADS):
            b_col = bc[:, h:h + 1]
            units.append(dict(
                b=b, h=h, q=q[:, h * ML_DQK:(h + 1) * ML_DQK], k=k[:, h * ML_DQK:(h + 1) * ML_DQK],
                vp=jnp.concatenate([v[:, h * ML_DV:(h + 1) * ML_DV], one_col], axis=1),
                b_col=b_col, ig_col=ig[:, h:h + 1], btot=b_col[0:1, :] if rev else b_col[L - 1:L, :],
                b_row=rows_[ML_HEADS + h:ML_HEADS + h + 1, :], ig_row=rows_[h:h + 1, :]))
    decay = [jnp.exp(jnp.where(incl, (u["b_col"] - u["b_row"]) + u["ig_row"], NEG_BIG)) for u in units]
    qk_s = [_mm(u["q"], u["k"], _NT) for u in units]
    qc = [_mm(u["q"], c_scr[u["b"], u["h"]]) for u in units]
    kw = [u["k"] * jnp.exp((u["btot"] - u["b_col"]) + u["ig_col"]) for u in units]
    kv = [_mm(kw[n], u["vp"], _TN) for n, u in enumerate(units)]
    sv = [_mm(qk_s[n] * decay[n], u["vp"]) for n, u in enumerate(units)]
    for n, u in enumerate(units):
        b, h = u["b"], u["h"]
        nd = sv[n] + jnp.exp(u["b_col"]) * qc[n]
        den = nd[:, ML_DV:ML_DV + 1]
        h_ref[b, :, h * ML_DV:(h + 1) * ML_DV] = (nd[:, :ML_DV] / jnp.maximum(jnp.abs(den), 1.0)).astype(bf16)
        c_scr[b, h] = jnp.exp(u["btot"]) * c_scr[b, h] + kv[n]
    cfin_ref[...] = c_scr[...]


def _mlstm(pfull, taps, conv_b, gate_b, c0, rev, d, grid_conv, qk_conv=None):
    bsz, t, _ = pfull.shape
    L = ML_CHUNK
    nb = t // L
    tb = (lambda j: nb - 1 - j) if rev else (lambda j: j)
    vblk = P_V // ML_V
    qkblk = P_QK // (2 * ML_QK)
    rpb = L // GRID_W
    n_rows = t // GRID_W
    conv_done = qk_conv is not None
    tok = lambda w, blk: pl.BlockSpec((bsz, L, w), lambda j: (0, tb(j), blk))
    full = lambda shape: pl.BlockSpec(shape, lambda j: (0,) * len(shape))
    st_spec = full((bsz, ML_HEADS, ML_DQK, 2 * ML_DV))
    common = [tok(ML_V, vblk), tok(P_MLG_BLK, P_MLG // P_MLG_BLK)]
    if conv_done:
        in_specs = [tok(2 * ML_QK, 0)] + common + [full((1, 4 * ML_HEADS)), st_spec]
        args = (qk_conv, pfull, pfull, gate_b, c0)
    else:
        in_specs = [
            tok(2 * ML_QK, qkblk),
            pl.BlockSpec((bsz, GRID_W, 2 * ML_QK), lambda j: (0, jnp.maximum(tb(j) * rpb - 1, 0), qkblk)),
            pl.BlockSpec((bsz, GRID_W, 2 * ML_QK),
                         lambda j: (0, jnp.minimum((tb(j) + 1) * rpb, n_rows - 1), qkblk)),
        ] + common + [full((9, 2 * ML_QK)), full((1, 2 * ML_QK)), full((1, 4 * ML_HEADS)), st_spec]
        args = (pfull, pfull, pfull, pfull, pfull, taps, conv_b, gate_b, c0)
    out_specs = [tok(ML_V, 0), st_spec]
    out_shape = [jax.ShapeDtypeStruct((bsz, t, ML_V), bf16),
                 jax.ShapeDtypeStruct((bsz, ML_HEADS, ML_DQK, 2 * ML_DV), f32)]
    if not conv_done:
        out_specs.append(tok(2 * ML_QK, 0))
        out_shape.append(jax.ShapeDtypeStruct((bsz, t, 2 * ML_QK), bf16))
    return pl.pallas_call(
        functools.partial(_mlstm_kernel, rev=rev, d=d, grid_conv=grid_conv, nb=nb, bsz=bsz,
                          conv_done=conv_done),
        grid=(nb,),
        in_specs=in_specs,
        out_specs=out_specs,
        out_shape=out_shape,
        scratch_shapes=[pltpu.VMEM((bsz, ML_HEADS, ML_DQK, 2 * ML_DV), f32)],
        compiler_params=pltpu.CompilerParams(
            dimension_semantics=("arbitrary",), vmem_limit_bytes=VMEM_LIMIT),
        name="mlstm_bwd" if rev else "mlstm_fwd",
    )(*args)


def _rw_prepare(p_ref, pp_ref, pn_ref, jblk, nb, d, par, scr, bv_ref):
    mu_ref, wup_ref, w0_ref, aup_ref, a0_ref, kk_ref, ka_ref, rk_ref, ones_ref, nbr_ref = par
    r_scr, v_scr, kk_scr, kd_scr, be_scr, ld_scr = scr
    tb_ = RW_BLOCK
    p_bf = p_ref[...]
    p = p_bf.astype(f32)
    nb_avg = _dot(nbr_ref[...], p_bf)
    rid8 = lax.broadcasted_iota(jnp.int32, (8, 1), 0)
    prev_row = jnp.where(jblk > 0, pp_ref[RW_HALO - 1:RW_HALO, :].astype(f32), 0.0)
    next_row = jnp.where(jblk < nb - 1, pn_ref[0:1, :].astype(f32), 0.0)
    nb_avg = jnp.concatenate([nb_avg[0:8] + jnp.where(rid8 == 0, 0.5 * prev_row, 0.0),
                              nb_avg[8:tb_ - 8],
                              nb_avg[tb_ - 8:] + jnp.where(rid8 == 7, 0.5 * next_row, 0.0)], axis=0)
    p = p + mu_ref[...] * (nb_avg - p)
    r = p[:, 0:RW_W]
    k = p[:, RW_W:2 * RW_W]
    v = p[:, 2 * RW_W:3 * RW_W]
    o_wd = 3 * RW_W + d * RW_DECAY_LORA
    o_ad = 3 * RW_W + 2 * RW_DECAY_LORA + d * RW_A_LORA
    lw = w0_ref[d:d + 1, :] + _mm(jnp.tanh(p[:, o_wd:o_wd + RW_DECAY_LORA]), wup_ref[d])
    ld_scr[...] = -EXP_NEG_HALF * _sigmoid(lw)
    a = _sigmoid(a0_ref[d:d + 1, :] + _mm(p[:, o_ad:o_ad + RW_A_LORA], aup_ref[d]))
    ones_bd = ones_ref[...]
    kkr = k * kk_ref[...]
    kk = kkr * lax.rsqrt(_head_sum(kkr * kkr, ones_bd) + 1e-12)
    kd = k * (1.0 + (a - 1.0) * ka_ref[...])
    bv_ref[...] = (_head_sum(r * kd * rk_ref[...], ones_bd) * v).astype(bf16)
    r_scr[...] = r
    v_scr[...] = v
    kk_scr[...] = kk
    kd_scr[...] = kd
    be_scr[...] = kk * a
    return p


def _rwkv_kernel(pf_ref, pfp_ref, pfn_ref, pb_ref, pbp_ref, pbn_ref, s0f_ref, s0b_ref,
                 mu_ref, wup_ref, w0_ref, aup_ref, a0_ref, kk_ref, ka_ref, rk_ref, ones_ref, gup_ref, nbr_ref,
                 yf_ref, bvf_ref, g_ref, yb_ref, bvb_ref, sff_ref, sfb_ref,
                 stf_scr, stb_scr, *scr, nb, bsz):
    L = RW_CHUNK
    nch = RW_BLOCK // L
    j = pl.program_id(0)

    @pl.when(j == 0)
    def _():
        stf_scr[...] = s0f_ref[...]
        stb_scr[...] = s0b_ref[...]

    par = (mu_ref, wup_ref, w0_ref, aup_ref, a0_ref, kk_ref, ka_ref, rk_ref, ones_ref, nbr_ref)
    scr_f, scr_b = scr[:6], scr[6:]
    o_gd = 3 * RW_W + 2 * RW_DECAY_LORA + 2 * RW_A_LORA
    streams = []
    for b in range(bsz):
        sf = tuple(s.at[b] for s in scr_f)
        sb_ = tuple(s.at[b] for s in scr_b)
        p_f = _rw_prepare(pf_ref.at[b], pfp_ref.at[b], pfn_ref.at[b], j, nb, 0, par, sf, bvf_ref.at[b])
        _rw_prepare(pb_ref.at[b], pbp_ref.at[b], pbn_ref.at[b], nb - 1 - j, nb, 1, par, sb_, bvb_ref.at[b])
        g_ref[b] = _mm(_sigmoid(p_f[:, o_gd:o_gd + RW_GATE_LORA]), gup_ref[...]).astype(bf16)
        streams.append((sf, stf_scr.at[b], yf_ref.at[b], False))
        streams.append((sb_, stb_scr.at[b], yb_ref.at[b], True))

    ii = lax.broadcasted_iota(jnp.int32, (L, L), 0)
    jj = lax.broadcasted_iota(jnp.int32, (L, L), 1)
    ii2 = lax.broadcasted_iota(jnp.int32, (L, 2 * L), 0)
    jj2 = lax.broadcasted_iota(jnp.int32, (L, 2 * L), 1) & (L - 1)
    masks = []
    for rev in (False, True):
        incl = (jj >= ii) if rev else (jj <= ii)
        strict = (jj > ii) if rev else (jj < ii)
        incl2 = (jj2 >= ii2) if rev else (jj2 <= ii2)
        masks.append((incl2, strict, incl.astype(bf16)))

    def chunk(i, carry):
        units = []
        rows_of = []
        for si, (sc, st_scr, y_ref, rev) in enumerate(streams):
            r_scr, v_scr, kk_scr, kd_scr, be_scr, ld_scr = sc
            incl2, strict, tri = masks[int(rev)]
            c = (nch - 1 - i) if rev else i
            rows = pl.ds(pl.multiple_of(c * L, L), L)
            rows_of.append(rows)
            ld = ld_scr[rows, :]
            b = _mm_exact_lhs(tri, ld)
            btot = b[0:1, :] if rev else b[L - 1:L, :]
            e_nb = jnp.exp(-b)
            e_end = jnp.exp(btot - b)
            kd_c = kd_scr[rows, :]
            be_c = be_scr[rows, :]
            v_c = v_scr[rows, :]
            aw = -kk_scr[rows, :] * jnp.exp(b - ld)
            rw = r_scr[rows, :] * jnp.exp(b)
            bi = be_c * e_nb
            ki = kd_c * e_nb
            bw = be_c * e_end
            kw = kd_c * e_end
            wend = jnp.exp(btot)
            for h in range(RW_HEADS):
                c_ = slice(h * RW_N, (h + 1) * RW_N)
                units.append(dict(si=si, h=h, incl2=incl2, strict=strict, st=st_scr, aw=aw[:, c_], rw=rw[:, c_],
                                  bi=bi[:, c_], ki=ki[:, c_], bw=bw[:, c_], kw=kw[:, c_], v=v_c[:, c_],
                                  wend=wend[:, c_]))
        us = range(len(units))
        m = [_mm(jnp.concatenate([u["aw"], u["rw"]], axis=0),
                 jnp.concatenate([u["bi"], u["ki"]], axis=0), _NT) for u in units]
        pw = [jnp.where(units[n]["strict"], m[n][:L, :L], 0.0) for n in us]
        mak = [jnp.where(units[n]["strict"], m[n][:L, L:], 0.0) for n in us]
        mr = [jnp.where(units[n]["incl2"], m[n][L:, :], 0.0) for n in us]
        mv = [_mm(mak[n], units[n]["v"]) for n in us]
        x = [jnp.concatenate([units[n]["aw"], mv[n]], axis=1) for n in us]
        for it in range(6):
            if it < 5:
                pr = [_mm(pw[n], jnp.concatenate([x[n], pw[n]], axis=1)) for n in us]
                pw = [pr[n][:, 2 * RW_N:] for n in us]
                x = [x[n] + pr[n][:, :2 * RW_N] for n in us]
            else:
                pr = [_mm(pw[n], x[n]) for n in us]
                x = [x[n] + pr[n] for n in us]
        zero = jnp.zeros((L, RW_N), f32)
        ray = [_mm(mr[n], jnp.concatenate(
            [x[n], jnp.concatenate([zero, units[n]["v"]], axis=1)], axis=0)) for n in us]
        gt = [_mm(x[n], units[n]["bw"], _TN) for n in us]
        vk = [_mm(units[n]["v"], units[n]["kw"], _TN) for n in us]
        s_old = [units[n]["st"][units[n]["h"]] for n in us]
        yy = [_mm(units[n]["rw"] + ray[n][:, :RW_N], s_old[n], _NT) for n in us]
        sg = [_mm(s_old[n], gt[n][:RW_N]) for n in us]
        for si, (sc, st_scr, y_ref, rev) in enumerate(streams):
            mine = [n for n in us if units[n]["si"] == si]
            y_ref[rows_of[si], :] = jnp.concatenate(
                [yy[n] + ray[n][:, RW_N:] for n in mine], axis=1).astype(bf16)
            for n in mine:
                st_scr[units[n]["h"]] = (s_old[n] * units[n]["wend"] + sg[n]) + (gt[n][RW_N:] + vk[n])
        return carry

    lax.fori_loop(0, nch, chunk, 0)
    sff_ref[...] = stf_scr[...]
    sfb_ref[...] = stb_scr[...]


def _rwkv(pfull, s0f, s0b, mu, wup, w0, aup, a0, k_k, k_a, r_k, ones_bd, gup, nbr):
    bsz, t, _ = pfull.shape
    tb_ = RW_BLOCK
    nb = t // tb_
    cblk = P_RW // P_RW_PAD
    hpb = tb_ // RW_HALO
    nh = t // RW_HALO
    fwd = lambda j: j
    bwd = lambda j: nb - 1 - j

    def p_specs(tb):
        return [
            pl.BlockSpec((bsz, tb_, P_RW_PAD), lambda j: (0, tb(j), cblk)),
            pl.BlockSpec((bsz, RW_HALO, P_RW_PAD), lambda j: (0, jnp.maximum(tb(j) * hpb - 1, 0), cblk)),
            pl.BlockSpec((bsz, RW_HALO, P_RW_PAD), lambda j: (0, jnp.minimum((tb(j) + 1) * hpb, nh - 1), cblk)),
        ]

    full = lambda shape: pl.BlockSpec(shape, lambda j: (0,) * len(shape))
    st_spec = full((bsz, RW_HEADS, RW_N, RW_N))
    tok = lambda tb: pl.BlockSpec((bsz, tb_, RW_W), lambda j: (0, tb(j), 0))
    tok_shape = jax.ShapeDtypeStruct((bsz, t, RW_W), bf16)
    st_shape = jax.ShapeDtypeStruct((bsz, RW_HEADS, RW_N, RW_N), f32)
    return pl.pallas_call(
        functools.partial(_rwkv_kernel, nb=nb, bsz=bsz),
        grid=(nb,),
        in_specs=p_specs(fwd) + p_specs(bwd) + [
            st_spec, st_spec,
            full((1, P_RW_PAD)),
            full((2, RW_DECAY_LORA, RW_W)), full((2, RW_W)),
            full((2, RW_A_LORA, RW_W)), full((2, RW_W)),
            full((1, RW_W)), full((1, RW_W)), full((1, RW_W)),
            full((RW_W, RW_W)),
            full((RW_GATE_LORA, RW_W)),
            full((tb_, tb_)),
        ],
        out_specs=[tok(fwd), tok(fwd), tok(fwd), tok(bwd), tok(bwd), st_spec, st_spec],
        out_shape=[tok_shape] * 5 + [st_shape] * 2,
        scratch_shapes=[pltpu.VMEM((bsz, RW_HEADS, RW_N, RW_N), f32) for _ in range(2)]
        + [pltpu.VMEM((bsz, tb_, RW_W), f32) for _ in range(12)],
        compiler_params=pltpu.CompilerParams(
            dimension_semantics=("arbitrary",), vmem_limit_bytes=VMEM_LIMIT),
        name="rwkv",
    )(pfull, pfull, pfull, pfull, pfull, pfull, s0f, s0b, mu, wup, w0, aup, a0, k_k, k_a, r_k, ones_bd, gup, nbr)


def _merge_kernel(x_ref, hf_ref, hb_ref, o_ref_, yf_ref, yb_ref, bvf_ref, bvb_ref, g_ref, ga_ref, gb_ref,
                  g1_ref, mlg_ref, lnw_ref, lnb_ref, ones_ref, pa_ref, pb_ref, wo_ref, out_ref):
    hsum = hf_ref[...].astype(f32) + hb_ref[...].astype(f32)
    parts = []
    for h in range(ML_HEADS):
        hh = hsum[:, h * ML_DV:(h + 1) * ML_DV]
        parts.append(hh * lax.rsqrt(jnp.mean(hh * hh, axis=-1, keepdims=True) + NORM_EPS))
    a_lat = jnp.concatenate(parts, axis=1) * mlg_ref[...] * _sigmoid(o_ref_[...].astype(f32))
    y = yf_ref[...].astype(f32) + yb_ref[...].astype(f32)
    ones_bd = ones_ref[...]
    inv_n = 1.0 / RW_N
    mu = _head_sum(y, ones_bd) * inv_n
    dev = y - mu
    var = _head_sum(dev * dev, ones_bd) * inv_n
    yn = dev * lax.rsqrt(var + RW_GN_EPS) * lnw_ref[...] + lnb_ref[...]
    b_lat = (yn + (bvf_ref[...].astype(f32) + bvb_ref[...].astype(f32))) * g_ref[...].astype(f32)
    m = (_sigmoid(ga_ref[...].astype(f32)) * _mm(a_lat, pa_ref[...])
         + _sigmoid(gb_ref[...].astype(f32)) * _mm(b_lat, pb_ref[...]))
    out_ref[...] = x_ref[...] + g1_ref[...] * _mm(m, wo_ref[...])


def _merge(x, hf, hb, yf, yb, bvf, bvb, g, pfull, g1, ml_norm_g, ln_w, ln_b, ones_bd, pa_bf, pb_bf, wo_bf):
    bsz, t, _ = x.shape
    tm = min(t, 512)
    full = lambda shape: pl.BlockSpec(shape, lambda b, i: (0,) * len(shape))
    tok = lambda w: pl.BlockSpec((None, tm, w), lambda b, i: (b, i, 0))
    col = lambda w, off: pl.BlockSpec((None, tm, w), lambda b, i: (b, i, off // w))
    return pl.pallas_call(
        _merge_kernel,
        grid=(bsz, t // tm),
        in_specs=[
            tok(D_MODEL), tok(ML_V), tok(ML_V), col(ML_V, P_O),
            tok(RW_W), tok(RW_W), tok(RW_W), tok(RW_W), tok(RW_W),
            col(D_MODEL, P_GA), col(D_MODEL, P_GB),
            pl.BlockSpec((None, 1, D_MODEL), lambda b, i: (b, 0, 0)),
            full((1, ML_V)), full((1, RW_W)), full((1, RW_W)), full((RW_W, RW_W)),
            full((ML_V, D_MODEL)), full((RW_W, D_MODEL)), full((D_MODEL, D_MODEL)),
        ],
        out_specs=tok(D_MODEL),
        out_shape=jax.ShapeDtypeStruct((bsz, t, D_MODEL), f32),
        compiler_params=pltpu.CompilerParams(
            dimension_semantics=("parallel", "parallel"), vmem_limit_bytes=VMEM_LIMIT),
        name="merge",
    )(x, hf, hb, pfull, yf, yb, bvf, bvb, g, pfull, pfull, g1, ml_norm_g, ln_w, ln_b, ones_bd,
      pa_bf, pb_bf, wo_bf)


def _top_max(x):
    return jnp.max(x, axis=0, keepdims=True)


def _first_at(x, val, eid_f):
    return jnp.min(jnp.where(x == val, eid_f, float(ROUTE_W)), axis=0, keepdims=True)


def _split3(x):
    hi = x.astype(bf16)
    r1 = x - hi.astype(f32)
    mid = r1.astype(bf16)
    return hi, mid, (r1 - mid.astype(f32)).astype(bf16)


def _route_kernel(x_ref, g_ref, sc_ref, sh_ref, rw_ref, rb_ref, rt_ref):
    h = _norm_mod(x_ref[...], g_ref[...], sc_ref[...], sh_ref[...])
    logits = _mm3(h, rw_ref[...])
    e_i = lax.broadcasted_iota(jnp.int32, (ROUTE_W, ROUTE_W), 0)
    e_j = lax.broadcasted_iota(jnp.int32, (ROUTE_W, ROUTE_W), 1)
    eye = (e_i == e_j).astype(bf16)
    scores = _sigmoid(_mm_exact_lhs(eye, logits, _NT))
    sel = scores + rb_ref[...]
    eid = lax.broadcasted_iota(jnp.int32, sel.shape, 0)
    eid_f = eid.astype(f32)
    grp = eid // EXPERTS_PER_GROUP
    neg = -jnp.inf
    gr = range(N_GROUPS)
    mg = [jnp.where(grp == gi, sel, neg) for gi in gr]
    m1 = [_top_max(mg[gi]) for gi in gr]
    i1 = [_first_at(mg[gi], m1[gi], eid_f) for gi in gr]
    m2 = [_top_max(jnp.where(eid_f == i1[gi], neg, mg[gi])) for gi in gr]
    best_g = jnp.zeros((1,) + sel.shape[1:], jnp.int32)
    best_v = m1[0] + m2[0]
    for gi in range(1, N_GROUPS):
        gs = m1[gi] + m2[gi]
        upd = gs > best_v
        best_g = jnp.where(upd, gi, best_g)
        best_v = jnp.where(upd, gs, best_v)
    cand = jnp.where(grp == best_g, sel, neg)
    v1 = _top_max(cand)
    p1 = _first_at(cand, v1, eid_f)
    cand2 = jnp.where(eid_f == p1, neg, cand)
    v2 = _top_max(cand2)
    p2 = _first_at(cand2, v2, eid_f)
    picked = (eid_f == p1) | (eid_f == p2)
    w = jnp.where(picked, scores, 0.0)
    comb = w / jnp.sum(w, axis=0, keepdims=True)
    out_t = jnp.where(eid == ROUTE_GID, best_g.astype(f32), comb)
    rt_ref[...] = sum(_dot(part, eye, _TN) for part in _split3(out_t))


def _route(x, g, sc, sh, router_w, router_b):
    bsz, t, _ = x.shape
    tm = min(t, 512)
    pad = ROUTE_W - N_EXPERTS
    rw = jnp.pad(router_w, ((0, 0), (0, pad)))
    rb = jnp.pad(router_b.reshape(N_EXPERTS, 1), ((0, pad), (0, 0)))
    return pl.pallas_call(
        _route_kernel,
        grid=(bsz, t // tm),
        in_specs=[
            pl.BlockSpec((None, tm, D_MODEL), lambda b, i: (b, i, 0)),
            pl.BlockSpec((1, D_MODEL), lambda b, i: (0, 0)),
            pl.BlockSpec((None, 1, D_MODEL), lambda b, i: (b, 0, 0)),
            pl.BlockSpec((None, 1, D_MODEL), lambda b, i: (b, 0, 0)),
            pl.BlockSpec((D_MODEL, ROUTE_W), lambda b, i: (0, 0)),
            pl.BlockSpec((ROUTE_W, 1), lambda b, i: (0, 0)),
        ],
        out_specs=pl.BlockSpec((None, tm, ROUTE_W), lambda b, i: (b, i, 0)),
        out_shape=jax.ShapeDtypeStruct((bsz, t, ROUTE_W), f32),
        compiler_params=pltpu.CompilerParams(
            dimension_semantics=("parallel", "parallel"), vmem_limit_bytes=VMEM_LIMIT),
        name="route",
    )(x, g, sc, sh, rw, rb)


def _moe_kernel(meta_ref, x_ref, g_ref, sc_ref, sh_ref, g2_ref, rt_ref, fg_ref, wg_ref, wu_ref, wd_ref, o_ref,
                xs_scr, ys_scr, cs_scr, pos_scr, *, final_norm, tm, nt):
    sb = MOE_SB
    n_rows = tm + 2 * sb
    grp = pl.program_id(2)
    base = (pl.program_id(0) * nt + pl.program_id(1)) * (2 * N_GROUPS)

    @pl.when(grp == 0)
    def _():
        h = _norm_mod(x_ref[...], g_ref[...], sc_ref[...], sh_ref[...]).astype(bf16)
        rt = rt_ref[...]
        lane = lax.broadcasted_iota(jnp.int32, rt.shape, 1)
        gid = jnp.sum(jnp.where(lane == ROUTE_GID, rt, 0.0), axis=-1, keepdims=True)
        g8 = lax.broadcasted_iota(jnp.int32, (tm, 8), 1)
        onehot = gid == g8.astype(f32)
        ii = lax.broadcasted_iota(jnp.int32, (tm, tm), 0)
        jj = lax.broadcasted_iota(jnp.int32, (tm, tm), 1)
        rank = _dot((jj <= ii).astype(bf16), onehot.astype(bf16))
        start = jnp.zeros((1, 8), f32)
        g8r = lax.broadcasted_iota(jnp.int32, (1, 8), 1)
        for gi in range(N_GROUPS):
            start = jnp.where(g8r == gi, meta_ref[base + gi].astype(f32), start)
        pos = jnp.sum(jnp.where(onehot, (start + rank) - 1.0, 0.0), axis=-1, keepdims=True)
        posmat = jnp.broadcast_to(pos, (tm, 8))
        pos_scr[...] = posmat
        e_i = lax.broadcasted_iota(jnp.int32, (8, 8), 0)
        e_j = lax.broadcasted_iota(jnp.int32, (8, 8), 1)
        pos_row = _mm_exact_lhs((e_i == e_j).astype(bf16), posmat, _NT)[0:1, :]
        hi = rt.astype(bf16)
        r1 = rt - hi.astype(f32)
        mid = r1.astype(bf16)
        lo = (r1 - mid.astype(f32)).astype(bf16)
        rt3 = jnp.concatenate([hi, mid, lo], axis=1)
        for c in range(n_rows // sb):
            rid = (lax.broadcasted_iota(jnp.int32, (sb, 1), 0) + c * sb).astype(f32)
            perm = (rid == pos_row).astype(bf16)
            xs_scr[c * sb:(c + 1) * sb, :] = _dot(perm, h).astype(bf16)
            cc = _dot(perm, rt3)
            cs_scr[c * sb:(c + 1) * sb, :] = cc[:, :ROUTE_W] + (cc[:, ROUTE_W:2 * ROUTE_W] + cc[:, 2 * ROUTE_W:])
        ys_scr[...] = jnp.zeros_like(ys_scr)

    seg_start = meta_ref[base + grp]
    seg_blocks = meta_ref[base + N_GROUPS + grp]
    lane_sb = lax.broadcasted_iota(jnp.int32, (sb, ROUTE_W), 1)

    def block(kb, carry):
        rows = pl.ds(pl.multiple_of(seg_start + kb * sb, MOE_ALIGN), sb)
        xs = xs_scr[rows, :]
        cs = cs_scr[rows, :]
        acc = None
        for k in range(EXPERTS_PER_GROUP):
            gate = _dot(xs, wg_ref[k])
            hid = (gate * _sigmoid(gate)) * _dot(xs, wu_ref[k])
            ce = jnp.sum(jnp.where(lane_sb == grp * EXPERTS_PER_GROUP + k, cs, 0.0), axis=-1, keepdims=True)
            part = _dot((ce * hid).astype(bf16), wd_ref[k])
            acc = part if acc is None else acc + part
        ys_scr[rows, :] = acc.astype(bf16)
        return carry

    lax.fori_loop(0, seg_blocks, block, 0)

    @pl.when(grp == N_GROUPS - 1)
    def _():
        cid = lax.broadcasted_iota(jnp.int32, (1, n_rows), 1).astype(f32)
        chunks = [slice(c * sb, (c + 1) * sb) for c in range(tm // sb)]
        ys = [_dot((pos_scr[rows, 0:1] == cid).astype(bf16), ys_scr[...]) for rows in chunks]
        ys = [x_ref[rows, :] + g2_ref[...] * ys[c] for c, rows in enumerate(chunks)]
        if final_norm:
            ms = [jnp.mean(y * y, axis=-1, keepdims=True) for y in ys]
            ys = [y * lax.rsqrt(ms[c] + NORM_EPS) * fg_ref[...] for c, y in enumerate(ys)]
        for c, rows in enumerate(chunks):
            o_ref[rows, :] = ys[c]


def _moe(x, g, sc, sh, g2, rt, final_g, final_norm, w_gate, w_up, w_down):
    bsz, t, _ = x.shape
    tm = min(t, 1024)
    nt = t // tm
    sb = MOE_SB
    n_rows = tm + 2 * sb
    gid = rt[..., ROUTE_GID].astype(jnp.int32).reshape(bsz, nt, tm)
    cnt = jnp.sum(gid[..., None] == jnp.arange(N_GROUPS), axis=2).astype(jnp.int32)
    nblk = (cnt + (sb - 1)) // sb
    seg = ((cnt + (MOE_ALIGN - 1)) // MOE_ALIGN) * MOE_ALIGN
    start = jnp.cumsum(seg, axis=-1) - seg
    meta = jnp.concatenate([start, nblk], axis=-1).reshape(-1).astype(jnp.int32)
    wspec = lambda shape: pl.BlockSpec((EXPERTS_PER_GROUP,) + shape, lambda b, i, e, m: (e, 0, 0))
    grid_spec = pltpu.PrefetchScalarGridSpec(
        num_scalar_prefetch=1,
        grid=(bsz, nt, N_GROUPS),
        in_specs=[
            pl.BlockSpec((None, tm, D_MODEL), lambda b, i, e, m: (b, i, 0)),
            pl.BlockSpec((1, D_MODEL), lambda b, i, e, m: (0, 0)),
            pl.BlockSpec((None, 1, D_MODEL), lambda b, i, e, m: (b, 0, 0)),
            pl.BlockSpec((None, 1, D_MODEL), lambda b, i, e, m: (b, 0, 0)),
            pl.BlockSpec((None, 1, D_MODEL), lambda b, i, e, m: (b, 0, 0)),
            pl.BlockSpec((None, tm, ROUTE_W), lambda b, i, e, m: (b, i, 0)),
            pl.BlockSpec((1, D_MODEL), lambda b, i, e, m: (0, 0)),
            wspec((D_MODEL, D_EXPERT)), wspec((D_MODEL, D_EXPERT)), wspec((D_EXPERT, D_MODEL)),
        ],
        out_specs=pl.BlockSpec((None, tm, D_MODEL), lambda b, i, e, m: (b, i, 0)),
        scratch_shapes=[pltpu.VMEM((n_rows, D_MODEL), bf16), pltpu.VMEM((n_rows, D_MODEL), bf16),
                        pltpu.VMEM((n_rows, ROUTE_W), f32), pltpu.VMEM((tm, 8), f32)],
    )
    return pl.pallas_call(
        functools.partial(_moe_kernel, final_norm=final_norm, tm=tm, nt=nt),
        grid_spec=grid_spec,
        out_shape=jax.ShapeDtypeStruct((bsz, t, D_MODEL), f32),
        compiler_params=pltpu.CompilerParams(
            dimension_semantics=("parallel", "parallel", "arbitrary"), vmem_limit_bytes=MOE_VMEM_LIMIT),
        name="moe",
    )(meta, x, g, sc, sh, g2, rt, final_g, w_gate, w_up, w_down)


def _cast_kernel(a_ref, b_ref, c_ref, ao_ref, bo_ref, co_ref):
    ao_ref[...] = a_ref[...].astype(bf16)
    bo_ref[...] = b_ref[...].astype(bf16)
    co_ref[...] = c_ref[...].astype(bf16)


def _cast_experts(w_gate, w_up, w_down, l):
    per = 2
    shapes = [(D_MODEL, D_EXPERT), (D_MODEL, D_EXPERT), (D_EXPERT, D_MODEL)]
    return pl.pallas_call(
        _cast_kernel,
        grid=(N_EXPERTS // per,),
        in_specs=[pl.BlockSpec((None, per) + s, lambda i: (l, i, 0, 0)) for s in shapes],
        out_specs=[pl.BlockSpec((per,) + s, lambda i: (i, 0, 0)) for s in shapes],
        out_shape=[jax.ShapeDtypeStruct((N_EXPERTS,) + s, bf16) for s in shapes],
        compiler_params=pltpu.CompilerParams(
            dimension_semantics=("arbitrary",), vmem_limit_bytes=VMEM_LIMIT),
        name="cast_experts",
    )(w_gate, w_up, w_down)


def _pack_w_in(w_in):
    ml, rw, gt = w_in[:, :ML_COLS], w_in[:, ML_COLS:ML_COLS + RW_COLS], w_in[:, ML_COLS + RW_COLS:]
    qkvo, mlg = ml[:, :2 * ML_QK + 2 * ML_V], ml[:, 2 * ML_QK + 2 * ML_V:]
    z = lambda n: jnp.zeros((D_MODEL, n), w_in.dtype)
    return jnp.concatenate(
        [gt, rw, mlg, z(P_RW_PAD - RW_COLS - 4 * ML_HEADS), qkvo], axis=1).astype(bf16)


def _mixer(pfull, is_ctx, lp, ml_state, rw_state):
    conv_args = (pfull, lp["taps"], lp["conv_b"], lp["gate_b"])
    h_f, c_f, qk_conv = _mlstm(*conv_args, ml_state[0], rev=False, d=0, grid_conv=not is_ctx)
    h_b, c_b = _mlstm(*conv_args, ml_state[1], rev=True, d=1, grid_conv=not is_ctx, qk_conv=qk_conv)
    hs, ml_fin = [h_f, h_b], [c_f, c_b]
    yf, bvf, g, yb, bvb, sff, sfb = _rwkv(pfull, rw_state[0], rw_state[1], lp["rw_mu"], lp["rw_w_up"],
                                          lp["rw_w0"], lp["rw_a_up"], lp["rw_a0"], lp["rw_k_k"], lp["rw_k_a"],
                                          lp["rw_r_k"], lp["ones_bd"], lp["gup"], lp["nbr"])
    return (hs[0], hs[1], yf, yb, bvf, bvb, g), ml_fin, [sff, sfb]


def kernel(x, c, ctx, c_ctx, w_ada, b_ada, norm1_g, norm2_g, w_in, ml_conv_k, ml_conv_b, ml_gate_b, ml_norm_g, rw_mu, rw_w_up, rw_w0, rw_a_up, rw_a0, rw_g_up, rw_k_k, rw_k_a, rw_r_k, rw_ln_w, rw_ln_b, merge_pa, merge_pb, w_out, router_w, router_b, exp_w_gate, exp_w_up, exp_w_down, final_g):
    bsz = x.shape[0]
    s_rows = jnp.zeros((8, D_MODEL), f32).at[:bsz].set(c).at[bsz].set(c_ctx)
    mod = _ada(s_rows, w_ada, b_ada)
    head_id = jnp.arange(RW_W) // RW_N
    ones_bd = (head_id[:, None] == head_id[None, :]).astype(bf16)
    tok_id = jnp.arange(RW_BLOCK)
    nbr = (0.5 * (jnp.abs(tok_id[:, None] - tok_id[None, :]) == 1)).astype(bf16)
    row = lambda v: v.reshape(1, -1)

    x_lat, x_ctx = x, ctx
    for l in range(DEPTH):
        last = l == DEPTH - 1
        mod_lat = mod[l, :bsz].reshape(bsz, 1, N_MOD, D_MODEL)
        mod_ctx = jnp.broadcast_to(mod[l, bsz].reshape(1, 1, N_MOD, D_MODEL), (bsz, 1, N_MOD, D_MODEL))
        lp = dict(taps=ml_conv_k[l].reshape(9, 2 * ML_QK), conv_b=row(ml_conv_b[l]), gate_b=row(ml_gate_b[l]),
                  rw_mu=jnp.pad(row(rw_mu[l]), ((0, 0), (0, P_RW_PAD - RW_COLS))),
                  rw_w_up=rw_w_up[l].astype(bf16), rw_w0=rw_w0[l], rw_a_up=rw_a_up[l].astype(bf16),
                  rw_a0=rw_a0[l], rw_k_k=row(rw_k_k[l]), rw_k_a=row(rw_k_a[l]), rw_r_k=row(rw_r_k[l]),
                  ones_bd=ones_bd, gup=rw_g_up[l].astype(bf16), nbr=nbr)
        w_in_bf = _pack_w_in(w_in[l])
        pa_bf, pb_bf, wo_bf = merge_pa[l].astype(bf16), merge_pb[l].astype(bf16), w_out[l].astype(bf16)
        experts = _cast_experts(exp_w_gate, exp_w_up, exp_w_down, l)
        g1n, g2n = row(norm1_g[l]), row(norm2_g[l])
        readout = (row(ml_norm_g[l]), row(rw_ln_w[l]), row(rw_ln_b[l]), ones_bd, pa_bf, pb_bf, wo_bf)

        def m(modv, i):
            return modv[:, :, i]

        p_ctx = _proj(x_ctx, g1n, m(mod_ctx, 1), m(mod_ctx, 0), w_in_bf)
        ml0 = [jnp.zeros((bsz, ML_HEADS, ML_DQK, 2 * ML_DV), f32)] * 2
        rw0 = [jnp.zeros((bsz, RW_HEADS, RW_N, RW_N), f32)] * 2
        mix_c, ml_st, rw_st = _mixer(p_ctx, True, lp, ml0, rw0)

        p_lat = _proj(x_lat, g1n, m(mod_lat, 1), m(mod_lat, 0), w_in_bf)
        mix_l, _, _ = _mixer(p_lat, False, lp, ml_st, rw_st)
        x_lat = _merge(x_lat, *mix_l, p_lat, m(mod_lat, 2), *readout)
        comb = _route(x_lat, g2n, m(mod_lat, 4), m(mod_lat, 3), router_w, router_b)
        x_lat = _moe(x_lat, g2n, m(mod_lat, 4), m(mod_lat, 3), m(mod_lat, 5), comb, row(final_g), last, *experts)
        if not last:
            x_ctx = _merge(x_ctx, *mix_c, p_ctx, m(mod_ctx, 2), *readout)
            comb_c = _route(x_ctx, g2n, m(mod_ctx, 4), m(mod_ctx, 3), router_w, router_b)
            x_ctx = _moe(x_ctx, g2n, m(mod_ctx, 4), m(mod_ctx, 3), m(mod_ctx, 5), comb_c, row(final_g), False,
                         *experts)
    return x_lat
```

```python
import functools

import jax
import jax.numpy as jnp
import numpy as np
from jax import lax
from jax.experimental import pallas as pl
from jax.experimental.pallas import tpu as pltpu

f32 = jnp.float32
bf16 = jnp.bfloat16

D_MODEL = 1024
DEPTH = 2
GRID_W = 64
N_MOD = 6
NORM_EPS = 1e-6

ML_HEADS = 4
ML_DQK = 64
ML_DV = 128
ML_QK = ML_HEADS * ML_DQK
ML_V = ML_HEADS * ML_DV
GATE_CAP = 15.0
ML_COLS = 2 * ML_QK + 2 * ML_V + 4 * ML_HEADS
ML_CHUNK = 256

RW_HEADS = 8
RW_N = 64
RW_W = RW_HEADS * RW_N
RW_DECAY_LORA = 64
RW_A_LORA = 64
RW_GATE_LORA = 128
RW_GN_EPS = 6.4e-4
RW_COLS = 3 * RW_W + 2 * RW_DECAY_LORA + 2 * RW_A_LORA + RW_GATE_LORA
RW_CHUNK = 64
RW_BLOCK = 256
RW_HALO = 16

N_EXPERTS = 16
N_GROUPS = 4
EXPERTS_PER_GROUP = N_EXPERTS // N_GROUPS
D_EXPERT = 512

P_GA = 0
P_GB = D_MODEL
P_RW = 2 * D_MODEL
P_RW_PAD = 2048
P_MLG = P_RW + RW_COLS
P_MLG_BLK = 128
P_QK = P_RW + P_RW_PAD
P_V = P_QK + 2 * ML_QK
P_O = P_V + ML_V
P_COLS = P_O + ML_V
PROJ_TN = P_COLS // 2

VMEM_LIMIT = 48 * 1024 * 1024
MOE_VMEM_LIMIT = 58 * 1024 * 1024
ROUTE_W = 32
ROUTE_GID = N_EXPERTS
MOE_SB = 320
MOE_OUT_ROWS = 256
MOE_ALIGN = 16
EXP_NEG_HALF = float(np.exp(-0.5))
NEG_BIG = -1e30


_NN = ((1,), (0,))
_NT = ((1,), (1,))
_TN = ((0,), (0,))


def _dot(a, b, dims=_NN):
    return lax.dot_general(a, b, (dims, ((), ())), preferred_element_type=f32)


def _mm(a, b, dims=_NN):
    return _dot(a.astype(bf16), b.astype(bf16), dims)


def _hi_lo(x):
    hi = x.astype(bf16)
    lo = (x - hi.astype(f32)).astype(bf16)
    return hi, lo


def _mm3(a, b, dims=_NN):
    ah, al = _hi_lo(a)
    bh, bl = _hi_lo(b)
    return _dot(ah, bh, dims) + (_dot(ah, bl, dims) + _dot(al, bh, dims))


def _mm_exact_lhs(a_bf, b, dims=_NN):
    hi = b.astype(bf16)
    r1 = b - hi.astype(f32)
    mid = r1.astype(bf16)
    lo = (r1 - mid.astype(f32)).astype(bf16)
    return _dot(a_bf, hi, dims) + (_dot(a_bf, mid, dims) + _dot(a_bf, lo, dims))


def _head_sum(a, ones_bd):
    return _dot(a.astype(bf16), ones_bd)


def _sigmoid(x):
    return 1.0 / (1.0 + jnp.exp(-x))


def _norm_mod(x, g, sc, sh):
    y = x * lax.rsqrt(jnp.mean(x * x, axis=-1, keepdims=True) + NORM_EPS)
    return (y * g) * (1.0 + sc) + sh


def _ada_kernel(s_ref, w_ref, b_ref, o_ref):
    s = s_ref[...]
    s = s * _sigmoid(s)
    o_ref[...] = _mm3(s, w_ref[...]) + b_ref[...]


def _ada(s_rows, w_ada, b_ada):
    tn = 1536
    n = N_MOD * D_MODEL
    return pl.pallas_call(
        _ada_kernel,
        grid=(DEPTH, n // tn),
        in_specs=[
            pl.BlockSpec((8, D_MODEL), lambda l, j: (0, 0)),
            pl.BlockSpec((None, D_MODEL, tn), lambda l, j: (l, 0, j)),
            pl.BlockSpec((None, 1, tn), lambda l, j: (l, 0, j)),
        ],
        out_specs=pl.BlockSpec((None, 8, tn), lambda l, j: (l, 0, j)),
        out_shape=jax.ShapeDtypeStruct((DEPTH, 8, n), f32),
        compiler_params=pltpu.CompilerParams(
            dimension_semantics=("arbitrary", "arbitrary"), vmem_limit_bytes=VMEM_LIMIT),
        name="ada",
    )(s_rows, w_ada, b_ada.reshape(DEPTH, 1, n))


def _proj_kernel(x_ref, g_ref, sc_ref, sh_ref, w_ref, o_ref, h_scr):
    @pl.when(pl.program_id(2) == 0)
    def _():
        h_scr[...] = _norm_mod(x_ref[...], g_ref[...], sc_ref[...], sh_ref[...]).astype(bf16)

    o_ref[...] = _dot(h_scr[...], w_ref[...]).astype(bf16)


def _proj(x, g, sc, sh, w_bf):
    bsz, t, _ = x.shape
    tm = min(t, 1024)
    return pl.pallas_call(
        _proj_kernel,
        grid=(bsz, t // tm, P_COLS // PROJ_TN),
        in_specs=[
            pl.BlockSpec((None, tm, D_MODEL), lambda b, i, j: (b, i, 0)),
            pl.BlockSpec((1, D_MODEL), lambda b, i, j: (0, 0)),
            pl.BlockSpec((None, 1, D_MODEL), lambda b, i, j: (b, 0, 0)),
            pl.BlockSpec((None, 1, D_MODEL), lambda b, i, j: (b, 0, 0)),
            pl.BlockSpec((D_MODEL, PROJ_TN), lambda b, i, j: (0, j)),
        ],
        out_specs=pl.BlockSpec((None, tm, PROJ_TN), lambda b, i, j: (b, i, j)),
        out_shape=jax.ShapeDtypeStruct((bsz, t, P_COLS), bf16),
        scratch_shapes=[pltpu.VMEM((tm, D_MODEL), bf16)],
        compiler_params=pltpu.CompilerParams(
            dimension_semantics=("parallel", "parallel", "arbitrary"), vmem_limit_bytes=VMEM_LIMIT),
        name="proj",
    )(x, g, sc, sh, w_bf)


def _shift_rows(u, up_row, dn_row):
    n = u.shape[0]
    rid = lax.broadcasted_iota(jnp.int32, (n, 1), 0)
    up = jnp.where(rid == 0, up_row, pltpu.roll(u, 1, axis=0))
    dn = jnp.where(rid == n - 1, dn_row, pltpu.roll(u, n - 1, axis=0))
    return up, dn


def _log_sigmoid(x):
    return jnp.minimum(x, 0.0) - jnp.log1p(jnp.exp(-jnp.abs(x)))


def _mlstm_kernel(*refs, rev, d, grid_conv, nb, bsz, conv_done):
    if conv_done:
        qk_ref, v_ref, mlg_ref, gb_ref, c0_ref, h_ref, cfin_ref, c_scr = refs
    else:
        (qk_ref, qkp_ref, qkn_ref, v_ref, mlg_ref, taps_ref, cb_ref, gb_ref, c0_ref,
         h_ref, cfin_ref, qkc_ref, c_scr) = refs
    L = ML_CHUNK
    j = pl.program_id(0)
    jblk = (nb - 1 - j) if rev else j

    @pl.when(j == 0)
    def _():
        c_scr[...] = c0_ref[...]

    zero_row = jnp.zeros((1, 2 * ML_QK), f32)
    rid = lax.broadcasted_iota(jnp.int32, (L, 1), 0)
    if grid_conv:
        first_col = (rid % GRID_W) == 0
        last_col = (rid % GRID_W) == GRID_W - 1
    else:
        first_col = rid == 0
        last_col = rid == L - 1
    ii = lax.broadcasted_iota(jnp.int32, (L, L), 0)
    jj = lax.broadcasted_iota(jnp.int32, (L, L), 1)
    incl = (jj >= ii) if rev else (jj <= ii)
    tri = incl.astype(bf16)
    e_i = lax.broadcasted_iota(jnp.int32, (2 * ML_HEADS, 2 * ML_HEADS), 0)
    e_j = lax.broadcasted_iota(jnp.int32, (2 * ML_HEADS, 2 * ML_HEADS), 1)
    eye = (e_i == e_j).astype(bf16)
    one_col = (lax.broadcasted_iota(jnp.int32, (L, ML_DV), 1) == 0).astype(bf16)

    units = []
    for b in range(bsz):
        qk = qk_ref[b].astype(f32)
        if conv_done:
            q, k = qk[:, :ML_QK], qk[:, ML_QK:]
        else:
            taps = taps_ref[...]
            if grid_conv:
                above = jnp.where(jblk > 0, qkp_ref[b].astype(f32), 0.0)
                below = jnp.where(jblk < nb - 1, qkn_ref[b].astype(f32), 0.0)
                ext = jnp.concatenate([above, qk, below], axis=0)
                bases = [(dr, ext[dr * GRID_W:dr * GRID_W + L]) for dr in range(3)]
            else:
                bases = [(1, qk)]
            conv = cb_ref[...]
            for dr, base in bases:
                up, dn = _shift_rows(base, zero_row, zero_row)
                conv = conv + (taps[3 * dr:3 * dr + 1] * jnp.where(first_col, 0.0, up)
                               + taps[3 * dr + 1:3 * dr + 2] * base
                               + taps[3 * dr + 2:3 * dr + 3] * jnp.where(last_col, 0.0, dn))
            q = conv[:, :ML_QK]
            k = conv[:, ML_QK:] * (ML_DQK ** -0.5)
            qkc_ref[b] = jnp.concatenate([q, k], axis=1).astype(bf16)
        pre = mlg_ref[b][:, :4 * ML_HEADS].astype(f32) + gb_ref[...]
        pre = GATE_CAP * jnp.tanh(pre * (1.0 / GATE_CAP))
        ig = pre[:, d * ML_HEADS:(d + 1) * ML_HEADS]
        lf = _log_sigmoid(pre[:, (2 + d) * ML_HEADS:(3 + d) * ML_HEADS])
        bc = _mm_exact_lhs(tri, lf)
        rows_ = _mm_exact_lhs(eye, jnp.concatenate([ig, bc], axis=1), _NT)
        v = v_ref[b]
        for h in range(ML_HEADS):
            b_col = bc[:, h:h + 1]
            units.append(dict(
                b=b, h=h, q=q[:, h * ML_DQK:(h + 1) * ML_DQK], k=k[:, h * ML_DQK:(h + 1) * ML_DQK],
                vp=jnp.concatenate([v[:, h * ML_DV:(h + 1) * ML_DV], one_col], axis=1),
                b_col=b_col, ig_col=ig[:, h:h + 1], btot=b_col[0:1, :] if rev else b_col[L - 1:L, :],
                b_row=rows_[ML_HEADS + h:ML_HEADS + h + 1, :], ig_row=rows_[h:h + 1, :]))
    decay = [jnp.exp(jnp.where(incl, (u["b_col"] - u["b_row"]) + u["ig_row"], NEG_BIG)) for u in units]
    qk_s = [_mm(u["q"], u["k"], _NT) for u in units]
    qc = [_mm(u["q"], c_scr[u["b"], u["h"]]) for u in units]
    kw = [u["k"] * jnp.exp((u["btot"] - u["b_col"]) + u["ig_col"]) for u in units]
    kv = [_mm(kw[n], u["vp"], _TN) for n, u in enumerate(units)]
    sv = [_mm(qk_s[n] * decay[n], u["vp"]) for n, u in enumerate(units)]
    for n, u in enumerate(units):
        b, h = u["b"], u["h"]
        nd = sv[n] + jnp.exp(u["b_col"]) * qc[n]
        den = nd[:, ML_DV:ML_DV + 1]
        h_ref[b, :, h * ML_DV:(h + 1) * ML_DV] = (nd[:, :ML_DV] / jnp.maximum(jnp.abs(den), 1.0)).astype(bf16)
        c_scr[b, h] = jnp.exp(u["btot"]) * c_scr[b, h] + kv[n]
    cfin_ref[...] = c_scr[...]


def _mlstm(pfull, taps, conv_b, gate_b, c0, rev, d, grid_conv, qk_conv=None):
    bsz, t, _ = pfull.shape
    L = ML_CHUNK
    nb = t // L
    tb = (lambda j: nb - 1 - j) if rev else (lambda j: j)
    vblk = P_V // ML_V
    qkblk = P_QK // (2 * ML_QK)
    rpb = L // GRID_W
    n_rows = t // GRID_W
    conv_done = qk_conv is not None
    tok = lambda w, blk: pl.BlockSpec((bsz, L, w), lambda j: (0, tb(j), blk))
    full = lambda shape: pl.BlockSpec(shape, lambda j: (0,) * len(shape))
    st_spec = full((bsz, ML_HEADS, ML_DQK, 2 * ML_DV))
    common = [tok(ML_V, vblk), tok(P_MLG_BLK, P_MLG // P_MLG_BLK)]
    if conv_done:
        in_specs = [tok(2 * ML_QK, 0)] + common + [full((1, 4 * ML_HEADS)), st_spec]
        args = (qk_conv, pfull, pfull, gate_b, c0)
    else:
        in_specs = [
            tok(2 * ML_QK, qkblk),
            pl.BlockSpec((bsz, GRID_W, 2 * ML_QK), lambda j: (0, jnp.maximum(tb(j) * rpb - 1, 0), qkblk)),
            pl.BlockSpec((bsz, GRID_W, 2 * ML_QK),
                         lambda j: (0, jnp.minimum((tb(j) + 1) * rpb, n_rows - 1), qkblk)),
        ] + common + [full((9, 2 * ML_QK)), full((1, 2 * ML_QK)), full((1, 4 * ML_HEADS)), st_spec]
        args = (pfull, pfull, pfull, pfull, pfull, taps, conv_b, gate_b, c0)
    out_specs = [tok(ML_V, 0), st_spec]
    out_shape = [jax.ShapeDtypeStruct((bsz, t, ML_V), bf16),
                 jax.ShapeDtypeStruct((bsz, ML_HEADS, ML_DQK, 2 * ML_DV), f32)]
    if not conv_done:
        out_specs.append(tok(2 * ML_QK, 0))
        out_shape.append(jax.ShapeDtypeStruct((bsz, t, 2 * ML_QK), bf16))
    return pl.pallas_call(
        functools.partial(_mlstm_kernel, rev=rev, d=d, grid_conv=grid_conv, nb=nb, bsz=bsz,
                          conv_done=conv_done),
        grid=(nb,),
        in_specs=in_specs,
        out_specs=out_specs,
        out_shape=out_shape,
        scratch_shapes=[pltpu.VMEM((bsz, ML_HEADS, ML_DQK, 2 * ML_DV), f32)],
        compiler_params=pltpu.CompilerParams(
            dimension_semantics=("arbitrary",), vmem_limit_bytes=VMEM_LIMIT),
        name="mlstm_bwd" if rev else "mlstm_fwd",
    )(*args)


def _rw_prepare(p_ref, pp_ref, pn_ref, jblk, nb, d, par, scr, bv_ref):
    mu_ref, wup_ref, w0_ref, aup_ref, a0_ref, kk_ref, ka_ref, rk_ref, ones_ref, nbr_ref = par
    r_scr, v_scr, kk_scr, kd_scr, be_scr, ld_scr = scr
    tb_ = RW_BLOCK
    p_bf = p_ref[...]
    p = p_bf.astype(f32)
    nb_avg = _dot(nbr_ref[...], p_bf)
    rid8 = lax.broadcasted_iota(jnp.int32, (8, 1), 0)
    prev_row = jnp.where(jblk > 0, pp_ref[RW_HALO - 1:RW_HALO, :].astype(f32), 0.0)
    next_row = jnp.where(jblk < nb - 1, pn_ref[0:1, :].astype(f32), 0.0)
    nb_avg = jnp.concatenate([nb_avg[0:8] + jnp.where(rid8 == 0, 0.5 * prev_row, 0.0),
                              nb_avg[8:tb_ - 8],
                              nb_avg[tb_ - 8:] + jnp.where(rid8 == 7, 0.5 * next_row, 0.0)], axis=0)
    p = p + mu_ref[...] * (nb_avg - p)
    r = p[:, 0:RW_W]
    k = p[:, RW_W:2 * RW_W]
    v = p[:, 2 * RW_W:3 * RW_W]
    o_wd = 3 * RW_W + d * RW_DECAY_LORA
    o_ad = 3 * RW_W + 2 * RW_DECAY_LORA + d * RW_A_LORA
    lw = w0_ref[d:d + 1, :] + _mm(jnp.tanh(p[:, o_wd:o_wd + RW_DECAY_LORA]), wup_ref[d])
    ld_scr[...] = -EXP_NEG_HALF * _sigmoid(lw)
    a = _sigmoid(a0_ref[d:d + 1, :] + _mm(p[:, o_ad:o_ad + RW_A_LORA], aup_ref[d]))
    ones_bd = ones_ref[...]
    kkr = k * kk_ref[...]
    kk = kkr * lax.rsqrt(_head_sum(kkr * kkr, ones_bd) + 1e-12)
    kd = k * (1.0 + (a - 1.0) * ka_ref[...])
    bv_ref[...] = (_head_sum(r * kd * rk_ref[...], ones_bd) * v).astype(bf16)
    r_scr[...] = r
    v_scr[...] = v
    kk_scr[...] = kk
    kd_scr[...] = kd
    be_scr[...] = kk * a
    return p


def _rwkv_kernel(pf_ref, pfp_ref, pfn_ref, pb_ref, pbp_ref, pbn_ref, s0f_ref, s0b_ref,
                 mu_ref, wup_ref, w0_ref, aup_ref, a0_ref, kk_ref, ka_ref, rk_ref, ones_ref, gup_ref, nbr_ref,
                 yf_ref, bvf_ref, g_ref, yb_ref, bvb_ref, sff_ref, sfb_ref,
                 stf_scr, stb_scr, *scr, nb, bsz):
    L = RW_CHUNK
    nch = RW_BLOCK // L
    j = pl.program_id(0)

    @pl.when(j == 0)
    def _():
        stf_scr[...] = s0f_ref[...]
        stb_scr[...] = s0b_ref[...]

    par = (mu_ref, wup_ref, w0_ref, aup_ref, a0_ref, kk_ref, ka_ref, rk_ref, ones_ref, nbr_ref)
    scr_f, scr_b = scr[:6], scr[6:]
    o_gd = 3 * RW_W + 2 * RW_DECAY_LORA + 2 * RW_A_LORA
    streams = []
    for b in range(bsz):
        sf = tuple(s.at[b] for s in scr_f)
        sb_ = tuple(s.at[b] for s in scr_b)
        p_f = _rw_prepare(pf_ref.at[b], pfp_ref.at[b], pfn_ref.at[b], j, nb, 0, par, sf, bvf_ref.at[b])
        _rw_prepare(pb_ref.at[b], pbp_ref.at[b], pbn_ref.at[b], nb - 1 - j, nb, 1, par, sb_, bvb_ref.at[b])
        g_ref[b] = _mm(_sigmoid(p_f[:, o_gd:o_gd + RW_GATE_LORA]), gup_ref[...]).astype(bf16)
        streams.append((sf, stf_scr.at[b], yf_ref.at[b], False))
        streams.append((sb_, stb_scr.at[b], yb_ref.at[b], True))

    ii = lax.broadcasted_iota(jnp.int32, (L, L), 0)
    jj = lax.broadcasted_iota(jnp.int32, (L, L), 1)
    ii2 = lax.broadcasted_iota(jnp.int32, (L, 2 * L), 0)
    jj2 = lax.broadcasted_iota(jnp.int32, (L, 2 * L), 1) & (L - 1)
    masks = []
    for rev in (False, True):
        incl = (jj >= ii) if rev else (jj <= ii)
        strict = (jj > ii) if rev else (jj < ii)
        incl2 = (jj2 >= ii2) if rev else (jj2 <= ii2)
        masks.append((incl2, strict, incl.astype(bf16)))

    def chunk(i, carry):
        units = []
        rows_of = []
        for si, (sc, st_scr, y_ref, rev) in enumerate(streams):
            r_scr, v_scr, kk_scr, kd_scr, be_scr, ld_scr = sc
            incl2, strict, tri = masks[int(rev)]
            c = (nch - 1 - i) if rev else i
            rows = pl.ds(pl.multiple_of(c * L, L), L)
            rows_of.append(rows)
            ld = ld_scr[rows, :]
            b = _mm_exact_lhs(tri, ld)
            btot = b[0:1, :] if rev else b[L - 1:L, :]
            e_nb = jnp.exp(-b)
            e_end = jnp.exp(btot - b)
            kd_c = kd_scr[rows, :]
            be_c = be_scr[rows, :]
            v_c = v_scr[rows, :]
            aw = -kk_scr[rows, :] * jnp.exp(b - ld)
            rw = r_scr[rows, :] * jnp.exp(b)
            bi = be_c * e_nb
            ki = kd_c * e_nb
            bw = be_c * e_end
            kw = kd_c * e_end
            wend = jnp.exp(btot)
            for h in range(RW_HEADS):
                c_ = slice(h * RW_N, (h + 1) * RW_N)
                units.append(dict(si=si, h=h, incl2=incl2, strict=strict, st=st_scr, aw=aw[:, c_], rw=rw[:, c_],
                                  bi=bi[:, c_], ki=ki[:, c_], bw=bw[:, c_], kw=kw[:, c_], v=v_c[:, c_],
                                  wend=wend[:, c_]))
        us = range(len(units))
        m = [_mm(jnp.concatenate([u["aw"], u["rw"]], axis=0),
                 jnp.concatenate([u["bi"], u["ki"]], axis=0), _NT) for u in units]
        pw = [jnp.where(units[n]["strict"], m[n][:L, :L], 0.0) for n in us]
        mak = [jnp.where(units[n]["strict"], m[n][:L, L:], 0.0) for n in us]
        mr = [jnp.where(units[n]["incl2"], m[n][L:, :], 0.0) for n in us]
        mv = [_mm(mak[n], units[n]["v"]) for n in us]
        x = [jnp.concatenate([units[n]["aw"], mv[n]], axis=1) for n in us]
        for it in range(6):
            if it < 5:
                pr = [_mm(pw[n], jnp.concatenate([x[n], pw[n]], axis=1)) for n in us]
                pw = [pr[n][:, 2 * RW_N:] for n in us]
                x = [x[n] + pr[n][:, :2 * RW_N] for n in us]
            else:
                pr = [_mm(pw[n], x[n]) for n in us]
                x = [x[n] + pr[n] for n in us]
        zero = jnp.zeros((L, RW_N), f32)
        ray = [_mm(mr[n], jnp.concatenate(
            [x[n], jnp.concatenate([zero, units[n]["v"]], axis=1)], axis=0)) for n in us]
        gt = [_mm(x[n], units[n]["bw"], _TN) for n in us]
        vk = [_mm(units[n]["v"], units[n]["kw"], _TN) for n in us]
        s_old = [units[n]["st"][units[n]["h"]] for n in us]
        yy = [_mm(units[n]["rw"] + ray[n][:, :RW_N], s_old[n], _NT) for n in us]
        sg = [_mm(s_old[n], gt[n][:RW_N]) for n in us]
        for si, (sc, st_scr, y_ref, rev) in enumerate(streams):
            mine = [n for n in us if units[n]["si"] == si]
            y_ref[rows_of[si], :] = jnp.concatenate(
                [yy[n] + ray[n][:, RW_N:] for n in mine], axis=1).astype(bf16)
            for n in mine:
                st_scr[units[n]["h"]] = (s_old[n] * units[n]["wend"] + sg[n]) + (gt[n][RW_N:] + vk[n])
        return carry

    lax.fori_loop(0, nch, chunk, 0)
    sff_ref[...] = stf_scr[...]
    sfb_ref[...] = stb_scr[...]


def _rwkv(pfull, s0f, s0b, mu, wup, w0, aup, a0, k_k, k_a, r_k, ones_bd, gup, nbr):
    bsz, t, _ = pfull.shape
    tb_ = RW_BLOCK
    nb = t // tb_
    cblk = P_RW // P_RW_PAD
    hpb = tb_ // RW_HALO
    nh = t // RW_HALO
    fwd = lambda j: j
    bwd = lambda j: nb - 1 - j

    def p_specs(tb):
        return [
            pl.BlockSpec((bsz, tb_, P_RW_PAD), lambda j: (0, tb(j), cblk)),
            pl.BlockSpec((bsz, RW_HALO, P_RW_PAD), lambda j: (0, jnp.maximum(tb(j) * hpb - 1, 0), cblk)),
            pl.BlockSpec((bsz, RW_HALO, P_RW_PAD), lambda j: (0, jnp.minimum((tb(j) + 1) * hpb, nh - 1), cblk)),
        ]

    full = lambda shape: pl.BlockSpec(shape, lambda j: (0,) * len(shape))
    st_spec = full((bsz, RW_HEADS, RW_N, RW_N))
    tok = lambda tb: pl.BlockSpec((bsz, tb_, RW_W), lambda j: (0, tb(j), 0))
    tok_shape = jax.ShapeDtypeStruct((bsz, t, RW_W), bf16)
    st_shape = jax.ShapeDtypeStruct((bsz, RW_HEADS, RW_N, RW_N), f32)
    return pl.pallas_call(
        functools.partial(_rwkv_kernel, nb=nb, bsz=bsz),
        grid=(nb,),
        in_specs=p_specs(fwd) + p_specs(bwd) + [
            st_spec, st_spec,
            full((1, P_RW_PAD)),
            full((2, RW_DECAY_LORA, RW_W)), full((2, RW_W)),
            full((2, RW_A_LORA, RW_W)), full((2, RW_W)),
            full((1, RW_W)), full((1, RW_W)), full((1, RW_W)),
            full((RW_W, RW_W)),
            full((RW_GATE_LORA, RW_W)),
            full((tb_, tb_)),
        ],
        out_specs=[tok(fwd), tok(fwd), tok(fwd), tok(bwd), tok(bwd), st_spec, st_spec],
        out_shape=[tok_shape] * 5 + [st_shape] * 2,
        scratch_shapes=[pltpu.VMEM((bsz, RW_HEADS, RW_N, RW_N), f32) for _ in range(2)]
        + [pltpu.VMEM((bsz, tb_, RW_W), f32) for _ in range(12)],
        compiler_params=pltpu.CompilerParams(
            dimension_semantics=("arbitrary",), vmem_limit_bytes=VMEM_LIMIT),
        name="rwkv",
    )(pfull, pfull, pfull, pfull, pfull, pfull, s0f, s0b, mu, wup, w0, aup, a0, k_k, k_a, r_k, ones_bd, gup, nbr)


def _merge_kernel(x_ref, hf_ref, hb_ref, o_ref_, yf_ref, yb_ref, bvf_ref, bvb_ref, g_ref, ga_ref, gb_ref,
                  g1_ref, mlg_ref, lnw_ref, lnb_ref, ones_ref, pa_ref, pb_ref, wo_ref, out_ref):
    hsum = hf_ref[...].astype(f32) + hb_ref[...].astype(f32)
    parts = []
    for h in range(ML_HEADS):
        hh = hsum[:, h * ML_DV:(h + 1) * ML_DV]
        parts.append(hh * lax.rsqrt(jnp.mean(hh * hh, axis=-1, keepdims=True) + NORM_EPS))
    a_lat = jnp.concatenate(parts, axis=1) * mlg_ref[...] * _sigmoid(o_ref_[...].astype(f32))
    y = yf_ref[...].astype(f32) + yb_ref[...].astype(f32)
    ones_bd = ones_ref[...]
    inv_n = 1.0 / RW_N
    mu = _head_sum(y, ones_bd) * inv_n
    dev = y - mu
    var = _head_sum(dev * dev, ones_bd) * inv_n
    yn = dev * lax.rsqrt(var + RW_GN_EPS) * lnw_ref[...] + lnb_ref[...]
    b_lat = (yn + (bvf_ref[...].astype(f32) + bvb_ref[...].astype(f32))) * g_ref[...].astype(f32)
    m = (_sigmoid(ga_ref[...].astype(f32)) * _mm(a_lat, pa_ref[...])
         + _sigmoid(gb_ref[...].astype(f32)) * _mm(b_lat, pb_ref[...]))
    out_ref[...] = x_ref[...] + g1_ref[...] * _mm(m, wo_ref[...])


def _merge(x, hf, hb, yf, yb, bvf, bvb, g, pfull, g1, ml_norm_g, ln_w, ln_b, ones_bd, pa_bf, pb_bf, wo_bf):
    bsz, t, _ = x.shape
    tm = min(t, 512)
    full = lambda shape: pl.BlockSpec(shape, lambda b, i: (0,) * len(shape))
    tok = lambda w: pl.BlockSpec((None, tm, w), lambda b, i: (b, i, 0))
    col = lambda w, off: pl.BlockSpec((None, tm, w), lambda b, i: (b, i, off // w))
    return pl.pallas_call(
        _merge_kernel,
        grid=(bsz, t // tm),
        in_specs=[
            tok(D_MODEL), tok(ML_V), tok(ML_V), col(ML_V, P_O),
            tok(RW_W), tok(RW_W), tok(RW_W), tok(RW_W), tok(RW_W),
            col(D_MODEL, P_GA), col(D_MODEL, P_GB),
            pl.BlockSpec((None, 1, D_MODEL), lambda b, i: (b, 0, 0)),
            full((1, ML_V)), full((1, RW_W)), full((1, RW_W)), full((RW_W, RW_W)),
            full((ML_V, D_MODEL)), full((RW_W, D_MODEL)), full((D_MODEL, D_MODEL)),
        ],
        out_specs=tok(D_MODEL),
        out_shape=jax.ShapeDtypeStruct((bsz, t, D_MODEL), f32),
        compiler_params=pltpu.CompilerParams(
            dimension_semantics=("parallel", "parallel"), vmem_limit_bytes=VMEM_LIMIT),
        name="merge",
    )(x, hf, hb, pfull, yf, yb, bvf, bvb, g, pfull, pfull, g1, ml_norm_g, ln_w, ln_b, ones_bd,
      pa_bf, pb_bf, wo_bf)


def _top_max(x):
    return jnp.max(x, axis=0, keepdims=True)


def _first_at(x, val, eid_f):
    return jnp.min(jnp.where(x == val, eid_f, float(ROUTE_W)), axis=0, keepdims=True)


def _split3(x):
    hi = x.astype(bf16)
    r1 = x - hi.astype(f32)
    mid = r1.astype(bf16)
    return hi, mid, (r1 - mid.astype(f32)).astype(bf16)


def _route_kernel(x_ref, g_ref, sc_ref, sh_ref, rw_ref, rb_ref, rt_ref):
    h = _norm_mod(x_ref[...], g_ref[...], sc_ref[...], sh_ref[...])
    logits = _mm3(h, rw_ref[...])
    e_i = lax.broadcasted_iota(jnp.int32, (ROUTE_W, ROUTE_W), 0)
    e_j = lax.broadcasted_iota(jnp.int32, (ROUTE_W, ROUTE_W), 1)
    eye = (e_i == e_j).astype(bf16)
    scores = _sigmoid(_mm_exact_lhs(eye, logits, _NT))
    sel = scores + rb_ref[...]
    eid = lax.broadcasted_iota(jnp.int32, sel.shape, 0)
    eid_f = eid.astype(f32)
    grp = eid // EXPERTS_PER_GROUP
    neg = -jnp.inf
    gr = range(N_GROUPS)
    mg = [jnp.where(grp == gi, sel, neg) for gi in gr]
    m1 = [_top_max(mg[gi]) for gi in gr]
    i1 = [_first_at(mg[gi], m1[gi], eid_f) for gi in gr]
    m2 = [_top_max(jnp.where(eid_f == i1[gi], neg, mg[gi])) for gi in gr]
    best_g = jnp.zeros((1,) + sel.shape[1:], jnp.int32)
    best_v = m1[0] + m2[0]
    for gi in range(1, N_GROUPS):
        gs = m1[gi] + m2[gi]
        upd = gs > best_v
        best_g = jnp.where(upd, gi, best_g)
        best_v = jnp.where(upd, gs, best_v)
    cand = jnp.where(grp == best_g, sel, neg)
    v1 = _top_max(cand)
    p1 = _first_at(cand, v1, eid_f)
    cand2 = jnp.where(eid_f == p1, neg, cand)
    v2 = _top_max(cand2)
    p2 = _first_at(cand2, v2, eid_f)
    picked = (eid_f == p1) | (eid_f == p2)
    w = jnp.where(picked, scores, 0.0)
    comb = w / jnp.sum(w, axis=0, keepdims=True)
    out_t = jnp.where(eid == ROUTE_GID, best_g.astype(f32), comb)
    rt_ref[...] = sum(_dot(part, eye, _TN) for part in _split3(out_t))


def _route(x, g, sc, sh, router_w, router_b):
    bsz, t, _ = x.shape
    tm = min(t, 512)
    pad = ROUTE_W - N_EXPERTS
    rw = jnp.pad(router_w, ((0, 0), (0, pad)))
    rb = jnp.pad(router_b.reshape(N_EXPERTS, 1), ((0, pad), (0, 0)))
    return pl.pallas_call(
        _route_kernel,
        grid=(bsz, t // tm),
        in_specs=[
            pl.BlockSpec((None, tm, D_MODEL), lambda b, i: (b, i, 0)),
            pl.BlockSpec((1, D_MODEL), lambda b, i: (0, 0)),
            pl.BlockSpec((None, 1, D_MODEL), lambda b, i: (b, 0, 0)),
            pl.BlockSpec((None, 1, D_MODEL), lambda b, i: (b, 0, 0)),
            pl.BlockSpec((D_MODEL, ROUTE_W), lambda b, i: (0, 0)),
            pl.BlockSpec((ROUTE_W, 1), lambda b, i: (0, 0)),
        ],
        out_specs=pl.BlockSpec((None, tm, ROUTE_W), lambda b, i: (b, i, 0)),
        out_shape=jax.ShapeDtypeStruct((bsz, t, ROUTE_W), f32),
        compiler_params=pltpu.CompilerParams(
            dimension_semantics=("parallel", "parallel"), vmem_limit_bytes=VMEM_LIMIT),
        name="route",
    )(x, g, sc, sh, rw, rb)


def _moe_rows(tm):
    need = tm + (N_GROUPS - 1) * (MOE_ALIGN - 1) + MOE_SB
    return -(-need // MOE_SB) * MOE_SB


def _moe_kernel(meta_ref, x_ref, g_ref, sc_ref, sh_ref, g2_ref, rt_ref, fg_ref, wg_ref, wu_ref, wd_ref, o_ref,
                xs_scr, ys_scr, cs_scr, pos_scr, *, final_norm, tm, nt):
    sb = MOE_SB
    n_rows = _moe_rows(tm)
    grp = pl.program_id(2)
    base = (pl.program_id(0) * nt + pl.program_id(1)) * (2 * N_GROUPS)

    @pl.when(grp == 0)
    def _():
        h = _norm_mod(x_ref[...], g_ref[...], sc_ref[...], sh_ref[...]).astype(bf16)
        rt = rt_ref[...]
        lane = lax.broadcasted_iota(jnp.int32, rt.shape, 1)
        gid = jnp.sum(jnp.where(lane == ROUTE_GID, rt, 0.0), axis=-1, keepdims=True)
        g8 = lax.broadcasted_iota(jnp.int32, (tm, 8), 1)
        onehot = gid == g8.astype(f32)
        ii = lax.broadcasted_iota(jnp.int32, (tm, tm), 0)
        jj = lax.broadcasted_iota(jnp.int32, (tm, tm), 1)
        rank = _dot((jj <= ii).astype(bf16), onehot.astype(bf16))
        start = jnp.zeros((1, 8), f32)
        g8r = lax.broadcasted_iota(jnp.int32, (1, 8), 1)
        for gi in range(N_GROUPS):
            start = jnp.where(g8r == gi, meta_ref[base + gi].astype(f32), start)
        pos = jnp.sum(jnp.where(onehot, (start + rank) - 1.0, 0.0), axis=-1, keepdims=True)
        posmat = jnp.broadcast_to(pos, (tm, 8))
        pos_scr[...] = posmat
        e_i = lax.broadcasted_iota(jnp.int32, (8, 8), 0)
        e_j = lax.broadcasted_iota(jnp.int32, (8, 8), 1)
        pos_row = _mm_exact_lhs((e_i == e_j).astype(bf16), posmat, _NT)[0:1, :]
        hi = rt.astype(bf16)
        r1 = rt - hi.astype(f32)
        mid = r1.astype(bf16)
        lo = (r1 - mid.astype(f32)).astype(bf16)
        rt3 = jnp.concatenate([hi, mid, lo], axis=1)
        for c in range(n_rows // sb):
            rid = (lax.broadcasted_iota(jnp.int32, (sb, 1), 0) + c * sb).astype(f32)
            perm = (rid == pos_row).astype(bf16)
            xs_scr[c * sb:(c + 1) * sb, :] = _dot(perm, h).astype(bf16)
            cc = _dot(perm, rt3)
            cs_scr[c * sb:(c + 1) * sb, :] = cc[:, :ROUTE_W] + (cc[:, ROUTE_W:2 * ROUTE_W] + cc[:, 2 * ROUTE_W:])
        ys_scr[...] = jnp.zeros_like(ys_scr)

    seg_start = meta_ref[base + grp]
    seg_blocks = meta_ref[base + N_GROUPS + grp]
    lane_sb = lax.broadcasted_iota(jnp.int32, (sb, ROUTE_W), 1)

    def block(kb, carry):
        rows = pl.ds(pl.multiple_of(seg_start + kb * sb, MOE_ALIGN), sb)
        xs = xs_scr[rows, :]
        cs = cs_scr[rows, :]
        acc = None
        for k in range(EXPERTS_PER_GROUP):
            gate = _dot(xs, wg_ref[k])
            hid = (gate * _sigmoid(gate)) * _dot(xs, wu_ref[k])
            ce = jnp.sum(jnp.where(lane_sb == grp * EXPERTS_PER_GROUP + k, cs, 0.0), axis=-1, keepdims=True)
            part = _dot((ce * hid).astype(bf16), wd_ref[k])
            acc = part if acc is None else acc + part
        ys_scr[rows, :] = acc.astype(bf16)
        return carry

    lax.fori_loop(0, seg_blocks, block, 0)

    @pl.when(grp == N_GROUPS - 1)
    def _():
        cid = lax.broadcasted_iota(jnp.int32, (1, n_rows), 1).astype(f32)
        chunks = [slice(c * MOE_OUT_ROWS, (c + 1) * MOE_OUT_ROWS) for c in range(tm // MOE_OUT_ROWS)]
        ys = [_dot((pos_scr[rows, 0:1] == cid).astype(bf16), ys_scr[...]) for rows in chunks]
        ys = [x_ref[rows, :] + g2_ref[...] * ys[c] for c, rows in enumerate(chunks)]
        if final_norm:
            ms = [jnp.mean(y * y, axis=-1, keepdims=True) for y in ys]
            ys = [y * lax.rsqrt(ms[c] + NORM_EPS) * fg_ref[...] for c, y in enumerate(ys)]
        for c, rows in enumerate(chunks):
            o_ref[rows, :] = ys[c]


def _moe(x, g, sc, sh, g2, rt, final_g, final_norm, w_gate, w_up, w_down):
    bsz, t, _ = x.shape
    tm = min(t, 1024)
    nt = t // tm
    sb = MOE_SB
    n_rows = _moe_rows(tm)
    gid = rt[..., ROUTE_GID].astype(jnp.int32).reshape(bsz, nt, tm)
    cnt = jnp.sum(gid[..., None] == jnp.arange(N_GROUPS), axis=2).astype(jnp.int32)
    nblk = (cnt + (sb - 1)) // sb
    seg = ((cnt + (MOE_ALIGN - 1)) // MOE_ALIGN) * MOE_ALIGN
    start = jnp.cumsum(seg, axis=-1) - seg
    meta = jnp.concatenate([start, nblk], axis=-1).reshape(-1).astype(jnp.int32)
    wspec = lambda shape: pl.BlockSpec((EXPERTS_PER_GROUP,) + shape, lambda b, i, e, m: (e, 0, 0))
    grid_spec = pltpu.PrefetchScalarGridSpec(
        num_scalar_prefetch=1,
        grid=(bsz, nt, N_GROUPS),
        in_specs=[
            pl.BlockSpec((None, tm, D_MODEL), lambda b, i, e, m: (b, i, 0)),
            pl.BlockSpec((1, D_MODEL), lambda b, i, e, m: (0, 0)),
            pl.BlockSpec((None, 1, D_MODEL), lambda b, i, e, m: (b, 0, 0)),
            pl.BlockSpec((None, 1, D_MODEL), lambda b, i, e, m: (b, 0, 0)),
            pl.BlockSpec((None, 1, D_MODEL), lambda b, i, e, m: (b, 0, 0)),
            pl.BlockSpec((None, tm, ROUTE_W), lambda b, i, e, m: (b, i, 0)),
            pl.BlockSpec((1, D_MODEL), lambda b, i, e, m: (0, 0)),
            wspec((D_MODEL, D_EXPERT)), wspec((D_MODEL, D_EXPERT)), wspec((D_EXPERT, D_MODEL)),
        ],
        out_specs=pl.BlockSpec((None, tm, D_MODEL), lambda b, i, e, m: (b, i, 0)),
        scratch_shapes=[pltpu.VMEM((n_rows, D_MODEL), bf16), pltpu.VMEM((n_rows, D_MODEL), bf16),
                        pltpu.VMEM((n_rows, ROUTE_W), f32), pltpu.VMEM((tm, 8), f32)],
    )
    return pl.pallas_call(
        functools.partial(_moe_kernel, final_norm=final_norm, tm=tm, nt=nt),
        grid_spec=grid_spec,
        out_shape=jax.ShapeDtypeStruct((bsz, t, D_MODEL), f32),
        compiler_params=pltpu.CompilerParams(
            dimension_semantics=("parallel", "parallel", "arbitrary"), vmem_limit_bytes=MOE_VMEM_LIMIT),
        name="moe",
    )(meta, x, g, sc, sh, g2, rt, final_g, w_gate, w_up, w_down)


def _cast_kernel(a_ref, b_ref, c_ref, ao_ref, bo_ref, co_ref):
    ao_ref[...] = a_ref[...].astype(bf16)
    bo_ref[...] = b_ref[...].astype(bf16)
    co_ref[...] = c_ref[...].astype(bf16)


def _cast_experts(w_gate, w_up, w_down, l):
    per = 2
    shapes = [(D_MODEL, D_EXPERT), (D_MODEL, D_EXPERT), (D_EXPERT, D_MODEL)]
    return pl.pallas_call(
        _cast_kernel,
        grid=(N_EXPERTS // per,),
        in_specs=[pl.BlockSpec((None, per) + s, lambda i: (l, i, 0, 0)) for s in shapes],
        out_specs=[pl.BlockSpec((per,) + s, lambda i: (i, 0, 0)) for s in shapes],
        out_shape=[jax.ShapeDtypeStruct((N_EXPERTS,) + s, bf16) for s in shapes],
        compiler_params=pltpu.CompilerParams(
            dimension_semantics=("arbitrary",), vmem_limit_bytes=VMEM_LIMIT),
        name="cast_experts",
    )(w_gate, w_up, w_down)


def _pack_w_in(w_in):
    ml, rw, gt = w_in[:, :ML_COLS], w_in[:, ML_COLS:ML_COLS + RW_COLS], w_in[:, ML_COLS + RW_COLS:]
    qkvo, mlg = ml[:, :2 * ML_QK + 2 * ML_V], ml[:, 2 * ML_QK + 2 * ML_V:]
    z = lambda n: jnp.zeros((D_MODEL, n), w_in.dtype)
    return jnp.concatenate(
        [gt, rw, mlg, z(P_RW_PAD - RW_COLS - 4 * ML_HEADS), qkvo], axis=1).astype(bf16)


def _mixer(pfull, is_ctx, lp, ml_state, rw_state):
    conv_args = (pfull, lp["taps"], lp["conv_b"], lp["gate_b"])
    h_f, c_f, qk_conv = _mlstm(*conv_args, ml_state[0], rev=False, d=0, grid_conv=not is_ctx)
    h_b, c_b = _mlstm(*conv_args, ml_state[1], rev=True, d=1, grid_conv=not is_ctx, qk_conv=qk_conv)
    hs, ml_fin = [h_f, h_b], [c_f, c_b]
    yf, bvf, g, yb, bvb, sff, sfb = _rwkv(pfull, rw_state[0], rw_state[1], lp["rw_mu"], lp["rw_w_up"],
                                          lp["rw_w0"], lp["rw_a_up"], lp["rw_a0"], lp["rw_k_k"], lp["rw_k_a"],
                                          lp["rw_r_k"], lp["ones_bd"], lp["gup"], lp["nbr"])
    return (hs[0], hs[1], yf, yb, bvf, bvb, g), ml_fin, [sff, sfb]


def kernel(x, c, ctx, c_ctx, w_ada, b_ada, norm1_g, norm2_g, w_in, ml_conv_k, ml_conv_b, ml_gate_b, ml_norm_g, rw_mu, rw_w_up, rw_w0, rw_a_up, rw_a0, rw_g_up, rw_k_k, rw_k_a, rw_r_k, rw_ln_w, rw_ln_b, merge_pa, merge_pb, w_out, router_w, router_b, exp_w_gate, exp_w_up, exp_w_down, final_g):
    bsz = x.shape[0]
    s_rows = jnp.zeros((8, D_MODEL), f32).at[:bsz].set(c).at[bsz].set(c_ctx)
    mod = _ada(s_rows, w_ada, b_ada)
    head_id = jnp.arange(RW_W) // RW_N
    ones_bd = (head_id[:, None] == head_id[None, :]).astype(bf16)
    tok_id = jnp.arange(RW_BLOCK)
    nbr = (0.5 * (jnp.abs(tok_id[:, None] - tok_id[None, :]) == 1)).astype(bf16)
    row = lambda v: v.reshape(1, -1)

    x_lat, x_ctx = x, ctx
    for l in range(DEPTH):
        last = l == DEPTH - 1
        mod_lat = mod[l, :bsz].reshape(bsz, 1, N_MOD, D_MODEL)
        mod_ctx = jnp.broadcast_to(mod[l, bsz].reshape(1, 1, N_MOD, D_MODEL), (bsz, 1, N_MOD, D_MODEL))
        lp = dict(taps=ml_conv_k[l].reshape(9, 2 * ML_QK), conv_b=row(ml_conv_b[l]), gate_b=row(ml_gate_b[l]),
                  rw_mu=jnp.pad(row(rw_mu[l]), ((0, 0), (0, P_RW_PAD - RW_COLS))),
                  rw_w_up=rw_w_up[l].astype(bf16), rw_w0=rw_w0[l], rw_a_up=rw_a_up[l].astype(bf16),
                  rw_a0=rw_a0[l], rw_k_k=row(rw_k_k[l]), rw_k_a=row(rw_k_a[l]), rw_r_k=row(rw_r_k[l]),
                  ones_bd=ones_bd, gup=rw_g_up[l].astype(bf16), nbr=nbr)
        w_in_bf = _pack_w_in(w_in[l])
        pa_bf, pb_bf, wo_bf = merge_pa[l].astype(bf16), merge_pb[l].astype(bf16), w_out[l].astype(bf16)
        experts = _cast_experts(exp_w_gate, exp_w_up, exp_w_down, l)
        g1n, g2n = row(norm1_g[l]), row(norm2_g[l])
        readout = (row(ml_norm_g[l]), row(rw_ln_w[l]), row(rw_ln_b[l]), ones_bd, pa_bf, pb_bf, wo_bf)

        def m(modv, i):
            return modv[:, :, i]

        p_ctx = _proj(x_ctx, g1n, m(mod_ctx, 1), m(mod_ctx, 0), w_in_bf)
        ml0 = [jnp.zeros((bsz, ML_HEADS, ML_DQK, 2 * ML_DV), f32)] * 2
        rw0 = [jnp.zeros((bsz, RW_HEADS, RW_N, RW_N), f32)] * 2
        mix_c, ml_st, rw_st = _mixer(p_ctx, True, lp, ml0, rw0)

        p_lat = _proj(x_lat, g1n, m(mod_lat, 1), m(mod_lat, 0), w_in_bf)
        mix_l, _, _ = _mixer(p_lat, False, lp, ml_st, rw_st)
        x_lat = _merge(x_lat, *mix_l, p_lat, m(mod_lat, 2), *readout)
        comb = _route(x_lat, g2n, m(mod_lat, 4), m(mod_lat, 3), router_w, router_b)
        x_lat = _moe(x_lat, g2n, m(mod_lat, 4), m(mod_lat, 3), m(mod_lat, 5), comb, row(final_g), last, *experts)
        if not last:
            x_ctx = _merge(x_ctx, *mix_c, p_ctx, m(mod_ctx, 2), *readout)
            comb_c = _route(x_ctx, g2n, m(mod_ctx, 4), m(mod_ctx, 3), router_w, router_b)
            x_ctx = _moe(x_ctx, g2n, m(mod_ctx, 4), m(mod_ctx, 3), m(mod_ctx, 5), comb_c, row(final_g), False,
                         *experts)
    return x_lat
```

```python
import functools

import jax
import jax.numpy as jnp
import numpy as np
from jax import lax
from jax.experimental import pallas as pl
from jax.experimental.pallas import tpu as pltpu

f32 = jnp.float32
bf16 = jnp.bfloat16

D_MODEL = 1024
DEPTH = 2
GRID_W = 64
N_MOD = 6
NORM_EPS = 1e-6

ML_HEADS = 4
ML_DQK = 64
ML_DV = 128
ML_QK = ML_HEADS * ML_DQK
ML_V = ML_HEADS * ML_DV
GATE_CAP = 15.0
ML_COLS = 2 * ML_QK + 2 * ML_V + 4 * ML_HEADS
ML_CHUNK = 256

RW_HEADS = 8
RW_N = 64
RW_W = RW_HEADS * RW_N
RW_DECAY_LORA = 64
RW_A_LORA = 64
RW_GATE_LORA = 128
RW_GN_EPS = 6.4e-4
RW_COLS = 3 * RW_W + 2 * RW_DECAY_LORA + 2 * RW_A_LORA + RW_GATE_LORA
RW_CHUNK = 64
RW_BLOCK = 256
RW_HALO = 16

N_EXPERTS = 16
N_GROUPS = 4
EXPERTS_PER_GROUP = N_EXPERTS // N_GROUPS
D_EXPERT = 512

P_GA = 0
P_GB = D_MODEL
P_RW = 2 * D_MODEL
P_RW_PAD = 2048
P_MLG = P_RW + RW_COLS
P_MLG_BLK = 128
P_QK = P_RW + P_RW_PAD
P_V = P_QK + 2 * ML_QK
P_O = P_V + ML_V
P_COLS = P_O + ML_V
PROJ_TN = P_COLS // 2

VMEM_LIMIT = 48 * 1024 * 1024
MOE_VMEM_LIMIT = 58 * 1024 * 1024
ROUTE_W = 32
ROUTE_GID = N_EXPERTS
MOE_SB = 128
MOE_OUT_ROWS = 256
MOE_ALIGN = 16
EXP_NEG_HALF = float(np.exp(-0.5))
NEG_BIG = -1e30


_NN = ((1,), (0,))
_NT = ((1,), (1,))
_TN = ((0,), (0,))


def _dot(a, b, dims=_NN):
    return lax.dot_general(a, b, (dims, ((), ())), preferred_element_type=f32)


def _mm(a, b, dims=_NN):
    return _dot(a.astype(bf16), b.astype(bf16), dims)


def _hi_lo(x):
    hi = x.astype(bf16)
    lo = (x - hi.astype(f32)).astype(bf16)
    return hi, lo


def _mm3(a, b, dims=_NN):
    ah, al = _hi_lo(a)
    bh, bl = _hi_lo(b)
    return _dot(ah, bh, dims) + (_dot(ah, bl, dims) + _dot(al, bh, dims))


def _mm_exact_lhs(a_bf, b, dims=_NN):
    hi = b.astype(bf16)
    r1 = b - hi.astype(f32)
    mid = r1.astype(bf16)
    lo = (r1 - mid.astype(f32)).astype(bf16)
    return _dot(a_bf, hi, dims) + (_dot(a_bf, mid, dims) + _dot(a_bf, lo, dims))


def _head_sum(a, ones_bd):
    return _dot(a.astype(bf16), ones_bd)


def _sigmoid(x):
    return 1.0 / (1.0 + jnp.exp(-x))


def _norm_mod(x, g, sc, sh):
    y = x * lax.rsqrt(jnp.mean(x * x, axis=-1, keepdims=True) + NORM_EPS)
    return (y * g) * (1.0 + sc) + sh


def _ada_kernel(s_ref, w_ref, b_ref, o_ref):
    s = s_ref[...]
    s = s * _sigmoid(s)
    o_ref[...] = _mm3(s, w_ref[...]) + b_ref[...]


def _ada(s_rows, w_ada, b_ada):
    tn = 1536
    n = N_MOD * D_MODEL
    return pl.pallas_call(
        _ada_kernel,
        grid=(DEPTH, n // tn),
        in_specs=[
            pl.BlockSpec((8, D_MODEL), lambda l, j: (0, 0)),
            pl.BlockSpec((None, D_MODEL, tn), lambda l, j: (l, 0, j)),
            pl.BlockSpec((None, 1, tn), lambda l, j: (l, 0, j)),
        ],
        out_specs=pl.BlockSpec((None, 8, tn), lambda l, j: (l, 0, j)),
        out_shape=jax.ShapeDtypeStruct((DEPTH, 8, n), f32),
        compiler_params=pltpu.CompilerParams(
            dimension_semantics=("arbitrary", "arbitrary"), vmem_limit_bytes=VMEM_LIMIT),
        name="ada",
    )(s_rows, w_ada, b_ada.reshape(DEPTH, 1, n))


def _proj_kernel(x_ref, g_ref, sc_ref, sh_ref, w_ref, o_ref, h_scr):
    @pl.when(pl.program_id(2) == 0)
    def _():
        h_scr[...] = _norm_mod(x_ref[...], g_ref[...], sc_ref[...], sh_ref[...]).astype(bf16)

    o_ref[...] = _dot(h_scr[...], w_ref[...]).astype(bf16)


def _proj(x, g, sc, sh, w_bf):
    bsz, t, _ = x.shape
    tm = min(t, 1024)
    return pl.pallas_call(
        _proj_kernel,
        grid=(bsz, t // tm, P_COLS // PROJ_TN),
        in_specs=[
            pl.BlockSpec((None, tm, D_MODEL), lambda b, i, j: (b, i, 0)),
            pl.BlockSpec((1, D_MODEL), lambda b, i, j: (0, 0)),
            pl.BlockSpec((None, 1, D_MODEL), lambda b, i, j: (b, 0, 0)),
            pl.BlockSpec((None, 1, D_MODEL), lambda b, i, j: (b, 0, 0)),
            pl.BlockSpec((D_MODEL, PROJ_TN), lambda b, i, j: (0, j)),
        ],
        out_specs=pl.BlockSpec((None, tm, PROJ_TN), lambda b, i, j: (b, i, j)),
        out_shape=jax.ShapeDtypeStruct((bsz, t, P_COLS), bf16),
        scratch_shapes=[pltpu.VMEM((tm, D_MODEL), bf16)],
        compiler_params=pltpu.CompilerParams(
            dimension_semantics=("parallel", "parallel", "arbitrary"), vmem_limit_bytes=VMEM_LIMIT),
        name="proj",
    )(x, g, sc, sh, w_bf)


def _shift_rows(u, up_row, dn_row):
    n = u.shape[0]
    rid = lax.broadcasted_iota(jnp.int32, (n, 1), 0)
    up = jnp.where(rid == 0, up_row, pltpu.roll(u, 1, axis=0))
    dn = jnp.where(rid == n - 1, dn_row, pltpu.roll(u, n - 1, axis=0))
    return up, dn


def _log_sigmoid(x):
    return jnp.minimum(x, 0.0) - jnp.log1p(jnp.exp(-jnp.abs(x)))


def _mlstm_kernel(*refs, rev, d, grid_conv, nb, bsz, conv_done):
    if conv_done:
        qk_ref, v_ref, mlg_ref, gb_ref, c0_ref, h_ref, cfin_ref, c_scr = refs
    else:
        (qk_ref, qkp_ref, qkn_ref, v_ref, mlg_ref, taps_ref, cb_ref, gb_ref, c0_ref,
         h_ref, cfin_ref, qkc_ref, c_scr) = refs
    L = ML_CHUNK
    j = pl.program_id(0)
    jblk = (nb - 1 - j) if rev else j

    @pl.when(j == 0)
    def _():
        c_scr[...] = c0_ref[...]

    zero_row = jnp.zeros((1, 2 * ML_QK), f32)
    rid = lax.broadcasted_iota(jnp.int32, (L, 1), 0)
    if grid_conv:
        first_col = (rid % GRID_W) == 0
        last_col = (rid % GRID_W) == GRID_W - 1
    else:
        first_col = rid == 0
        last_col = rid == L - 1
    ii = lax.broadcasted_iota(jnp.int32, (L, L), 0)
    jj = lax.broadcasted_iota(jnp.int32, (L, L), 1)
    incl = (jj >= ii) if rev else (jj <= ii)
    tri = incl.astype(bf16)
    e_i = lax.broadcasted_iota(jnp.int32, (2 * ML_HEADS, 2 * ML_HEADS), 0)
    e_j = lax.broadcasted_iota(jnp.int32, (2 * ML_HEADS, 2 * ML_HEADS), 1)
    eye = (e_i == e_j).astype(bf16)
    one_col = (lax.broadcasted_iota(jnp.int32, (L, ML_DV), 1) == 0).astype(bf16)

    units = []
    for b in range(bsz):
        qk = qk_ref[b].astype(f32)
        if conv_done:
            q, k = qk[:, :ML_QK], qk[:, ML_QK:]
        else:
            taps = taps_ref[...]
            if grid_conv:
                above = jnp.where(jblk > 0, qkp_ref[b].astype(f32), 0.0)
                below = jnp.where(jblk < nb - 1, qkn_ref[b].astype(f32), 0.0)
                ext = jnp.concatenate([above, qk, below], axis=0)
                bases = [(dr, ext[dr * GRID_W:dr * GRID_W + L]) for dr in range(3)]
            else:
                bases = [(1, qk)]
            conv = cb_ref[...]
            for dr, base in bases:
                up, dn = _shift_rows(base, zero_row, zero_row)
                conv = conv + (taps[3 * dr:3 * dr + 1] * jnp.where(first_col, 0.0, up)
                               + taps[3 * dr + 1:3 * dr + 2] * base
                               + taps[3 * dr + 2:3 * dr + 3] * jnp.where(last_col, 0.0, dn))
            q = conv[:, :ML_QK]
            k = conv[:, ML_QK:] * (ML_DQK ** -0.5)
            qkc_ref[b] = jnp.concatenate([q, k], axis=1).astype(bf16)
        pre = mlg_ref[b][:, :4 * ML_HEADS].astype(f32) + gb_ref[...]
        pre = GATE_CAP * jnp.tanh(pre * (1.0 / GATE_CAP))
        ig = pre[:, d * ML_HEADS:(d + 1) * ML_HEADS]
        lf = _log_sigmoid(pre[:, (2 + d) * ML_HEADS:(3 + d) * ML_HEADS])
        bc = _mm_exact_lhs(tri, lf)
        rows_ = _mm_exact_lhs(eye, jnp.concatenate([ig, bc], axis=1), _NT)
        v = v_ref[b]
        for h in range(ML_HEADS):
            b_col = bc[:, h:h + 1]
            units.append(dict(
                b=b, h=h, q=q[:, h * ML_DQK:(h + 1) * ML_DQK], k=k[:, h * ML_DQK:(h + 1) * ML_DQK],
                vp=jnp.concatenate([v[:, h * ML_DV:(h + 1) * ML_DV], one_col], axis=1),
                b_col=b_col, ig_col=ig[:, h:h + 1], btot=b_col[0:1, :] if rev else b_col[L - 1:L, :],
                b_row=rows_[ML_HEADS + h:ML_HEADS + h + 1, :], ig_row=rows_[h:h + 1, :]))
    decay = [jnp.exp(jnp.where(incl, (u["b_col"] - u["b_row"]) + u["ig_row"], NEG_BIG)) for u in units]
    qk_s = [_mm(u["q"], u["k"], _NT) for u in units]
    qc = [_mm(u["q"], c_scr[u["b"], u["h"]]) for u in units]
    kw = [u["k"] * jnp.exp((u["btot"] - u["b_col"]) + u["ig_col"]) for u in units]
    kv = [_mm(kw[n], u["vp"], _TN) for n, u in enumerate(units)]
    sv = [_mm(qk_s[n] * decay[n], u["vp"]) for n, u in enumerate(units)]
    for n, u in enumerate(units):
        b, h = u["b"], u["h"]
        nd = sv[n] + jnp.exp(u["b_col"]) * qc[n]
        den = nd[:, ML_DV:ML_DV + 1]
        h_ref[b, :, h * ML_DV:(h + 1) * ML_DV] = (nd[:, :ML_DV] / jnp.maximum(jnp.abs(den), 1.0)).astype(bf16)
        c_scr[b, h] = jnp.exp(u["btot"]) * c_scr[b, h] + kv[n]
    cfin_ref[...] = c_scr[...]


def _mlstm(pfull, taps, conv_b, gate_b, c0, rev, d, grid_conv, qk_conv=None):
    bsz, t, _ = pfull.shape
    L = ML_CHUNK
    nb = t // L
    tb = (lambda j: nb - 1 - j) if rev else (lambda j: j)
    vblk = P_V // ML_V
    qkblk = P_QK // (2 * ML_QK)
    rpb = L // GRID_W
    n_rows = t // GRID_W
    conv_done = qk_conv is not None
    tok = lambda w, blk: pl.BlockSpec((bsz, L, w), lambda j: (0, tb(j), blk))
    full = lambda shape: pl.BlockSpec(shape, lambda j: (0,) * len(shape))
    st_spec = full((bsz, ML_HEADS, ML_DQK, 2 * ML_DV))
    common = [tok(ML_V, vblk), tok(P_MLG_BLK, P_MLG // P_MLG_BLK)]
    if conv_done:
        in_specs = [tok(2 * ML_QK, 0)] + common + [full((1, 4 * ML_HEADS)), st_spec]
        args = (qk_conv, pfull, pfull, gate_b, c0)
    else:
        in_specs = [
            tok(2 * ML_QK, qkblk),
            pl.BlockSpec((bsz, GRID_W, 2 * ML_QK), lambda j: (0, jnp.maximum(tb(j) * rpb - 1, 0), qkblk)),
            pl.BlockSpec((bsz, GRID_W, 2 * ML_QK),
                         lambda j: (0, jnp.minimum((tb(j) + 1) * rpb, n_rows - 1), qkblk)),
        ] + common + [full((9, 2 * ML_QK)), full((1, 2 * ML_QK)), full((1, 4 * ML_HEADS)), st_spec]
        args = (pfull, pfull, pfull, pfull, pfull, taps, conv_b, gate_b, c0)
    out_specs = [tok(ML_V, 0), st_spec]
    out_shape = [jax.ShapeDtypeStruct((bsz, t, ML_V), bf16),
                 jax.ShapeDtypeStruct((bsz, ML_HEADS, ML_DQK, 2 * ML_DV), f32)]
    if not conv_done:
        out_specs.append(tok(2 * ML_QK, 0))
        out_shape.append(jax.ShapeDtypeStruct((bsz, t, 2 * ML_QK), bf16))
    return pl.pallas_call(
        functools.partial(_mlstm_kernel, rev=rev, d=d, grid_conv=grid_conv, nb=nb, bsz=bsz,
                          conv_done=conv_done),
        grid=(nb,),
        in_specs=in_specs,
        out_specs=out_specs,
        out_shape=out_shape,
        scratch_shapes=[pltpu.VMEM((bsz, ML_HEADS, ML_DQK, 2 * ML_DV), f32)],
        compiler_params=pltpu.CompilerParams(
            dimension_semantics=("arbitrary",), vmem_limit_bytes=VMEM_LIMIT),
        name="mlstm_bwd" if rev else "mlstm_fwd",
    )(*args)


def _rw_prepare(p_ref, pp_ref, pn_ref, jblk, nb, d, par, scr, bv_ref):
    mu_ref, wup_ref, w0_ref, aup_ref, a0_ref, kk_ref, ka_ref, rk_ref, ones_ref, nbr_ref = par
    r_scr, v_scr, kk_scr, kd_scr, be_scr, ld_scr = scr
    tb_ = RW_BLOCK
    p_bf = p_ref[...]
    p = p_bf.astype(f32)
    nb_avg = _dot(nbr_ref[...], p_bf)
    rid8 = lax.broadcasted_iota(jnp.int32, (8, 1), 0)
    prev_row = jnp.where(jblk > 0, pp_ref[RW_HALO - 1:RW_HALO, :].astype(f32), 0.0)
    next_row = jnp.where(jblk < nb - 1, pn_ref[0:1, :].astype(f32), 0.0)
    nb_avg = jnp.concatenate([nb_avg[0:8] + jnp.where(rid8 == 0, 0.5 * prev_row, 0.0),
                              nb_avg[8:tb_ - 8],
                              nb_avg[tb_ - 8:] + jnp.where(rid8 == 7, 0.5 * next_row, 0.0)], axis=0)
    p = p + mu_ref[...] * (nb_avg - p)
    r = p[:, 0:RW_W]
    k = p[:, RW_W:2 * RW_W]
    v = p[:, 2 * RW_W:3 * RW_W]
    o_wd = 3 * RW_W + d * RW_DECAY_LORA
    o_ad = 3 * RW_W + 2 * RW_DECAY_LORA + d * RW_A_LORA
    lw = w0_ref[d:d + 1, :] + _mm(jnp.tanh(p[:, o_wd:o_wd + RW_DECAY_LORA]), wup_ref[d])
    ld_scr[...] = -EXP_NEG_HALF * _sigmoid(lw)
    a = _sigmoid(a0_ref[d:d + 1, :] + _mm(p[:, o_ad:o_ad + RW_A_LORA], aup_ref[d]))
    ones_bd = ones_ref[...]
    kkr = k * kk_ref[...]
    kk = kkr * lax.rsqrt(_head_sum(kkr * kkr, ones_bd) + 1e-12)
    kd = k * (1.0 + (a - 1.0) * ka_ref[...])
    bv_ref[...] = (_head_sum(r * kd * rk_ref[...], ones_bd) * v).astype(bf16)
    r_scr[...] = r
    v_scr[...] = v
    kk_scr[...] = kk
    kd_scr[...] = kd
    be_scr[...] = kk * a
    return p


def _rwkv_kernel(pf_ref, pfp_ref, pfn_ref, pb_ref, pbp_ref, pbn_ref, s0f_ref, s0b_ref,
                 mu_ref, wup_ref, w0_ref, aup_ref, a0_ref, kk_ref, ka_ref, rk_ref, ones_ref, gup_ref, nbr_ref,
                 yf_ref, bvf_ref, g_ref, yb_ref, bvb_ref, sff_ref, sfb_ref,
                 stf_scr, stb_scr, *scr, nb, bsz):
    L = RW_CHUNK
    nch = RW_BLOCK // L
    j = pl.program_id(0)

    @pl.when(j == 0)
    def _():
        stf_scr[...] = s0f_ref[...]
        stb_scr[...] = s0b_ref[...]

    par = (mu_ref, wup_ref, w0_ref, aup_ref, a0_ref, kk_ref, ka_ref, rk_ref, ones_ref, nbr_ref)
    scr_f, scr_b = scr[:6], scr[6:]
    o_gd = 3 * RW_W + 2 * RW_DECAY_LORA + 2 * RW_A_LORA
    streams = []
    for b in range(bsz):
        sf = tuple(s.at[b] for s in scr_f)
        sb_ = tuple(s.at[b] for s in scr_b)
        p_f = _rw_prepare(pf_ref.at[b], pfp_ref.at[b], pfn_ref.at[b], j, nb, 0, par, sf, bvf_ref.at[b])
        _rw_prepare(pb_ref.at[b], pbp_ref.at[b], pbn_ref.at[b], nb - 1 - j, nb, 1, par, sb_, bvb_ref.at[b])
        g_ref[b] = _mm(_sigmoid(p_f[:, o_gd:o_gd + RW_GATE_LORA]), gup_ref[...]).astype(bf16)
        streams.append((sf, stf_scr.at[b], yf_ref.at[b], False))
        streams.append((sb_, stb_scr.at[b], yb_ref.at[b], True))

    ii = lax.broadcasted_iota(jnp.int32, (L, L), 0)
    jj = lax.broadcasted_iota(jnp.int32, (L, L), 1)
    ii2 = lax.broadcasted_iota(jnp.int32, (L, 2 * L), 0)
    jj2 = lax.broadcasted_iota(jnp.int32, (L, 2 * L), 1) & (L - 1)
    masks = []
    for rev in (False, True):
        incl = (jj >= ii) if rev else (jj <= ii)
        strict = (jj > ii) if rev else (jj < ii)
        incl2 = (jj2 >= ii2) if rev else (jj2 <= ii2)
        masks.append((incl2, strict, incl.astype(bf16)))

    def chunk(i, carry):
        units = []
        rows_of = []
        for si, (sc, st_scr, y_ref, rev) in enumerate(streams):
            r_scr, v_scr, kk_scr, kd_scr, be_scr, ld_scr = sc
            incl2, strict, tri = masks[int(rev)]
            c = (nch - 1 - i) if rev else i
            rows = pl.ds(pl.multiple_of(c * L, L), L)
            rows_of.append(rows)
            ld = ld_scr[rows, :]
            b = _mm_exact_lhs(tri, ld)
            btot = b[0:1, :] if rev else b[L - 1:L, :]
            e_nb = jnp.exp(-b)
            e_end = jnp.exp(btot - b)
            kd_c = kd_scr[rows, :]
            be_c = be_scr[rows, :]
            v_c = v_scr[rows, :]
            aw = -kk_scr[rows, :] * jnp.exp(b - ld)
            rw = r_scr[rows, :] * jnp.exp(b)
            bi = be_c * e_nb
            ki = kd_c * e_nb
            bw = be_c * e_end
            kw = kd_c * e_end
            wend = jnp.exp(btot)
            for h in range(RW_HEADS):
                c_ = slice(h * RW_N, (h + 1) * RW_N)
                units.append(dict(si=si, h=h, incl2=incl2, strict=strict, st=st_scr, aw=aw[:, c_], rw=rw[:, c_],
                                  bi=bi[:, c_], ki=ki[:, c_], bw=bw[:, c_], kw=kw[:, c_], v=v_c[:, c_],
                                  wend=wend[:, c_]))
        us = range(len(units))
        m = [_mm(jnp.concatenate([u["aw"], u["rw"]], axis=0),
                 jnp.concatenate([u["bi"], u["ki"]], axis=0), _NT) for u in units]
        pw = [jnp.where(units[n]["strict"], m[n][:L, :L], 0.0) for n in us]
        mak = [jnp.where(units[n]["strict"], m[n][:L, L:], 0.0) for n in us]
        mr = [jnp.where(units[n]["incl2"], m[n][L:, :], 0.0) for n in us]
        mv = [_mm(mak[n], units[n]["v"]) for n in us]
        x = [jnp.concatenate([units[n]["aw"], mv[n]], axis=1) for n in us]
        for it in range(6):
            if it < 5:
                pr = [_mm(pw[n], jnp.concatenate([x[n], pw[n]], axis=1)) for n in us]
                pw = [pr[n][:, 2 * RW_N:] for n in us]
                x = [x[n] + pr[n][:, :2 * RW_N] for n in us]
            else:
                pr = [_mm(pw[n], x[n]) for n in us]
                x = [x[n] + pr[n] for n in us]
        zero = jnp.zeros((L, RW_N), f32)
        ray = [_mm(mr[n], jnp.concatenate(
            [x[n], jnp.concatenate([zero, units[n]["v"]], axis=1)], axis=0)) for n in us]
        gt = [_mm(x[n], units[n]["bw"], _TN) for n in us]
        vk = [_mm(units[n]["v"], units[n]["kw"], _TN) for n in us]
        s_old = [units[n]["st"][units[n]["h"]] for n in us]
        yy = [_mm(units[n]["rw"] + ray[n][:, :RW_N], s_old[n], _NT) for n in us]
        sg = [_mm(s_old[n], gt[n][:RW_N]) for n in us]
        for si, (sc, st_scr, y_ref, rev) in enumerate(streams):
            mine = [n for n in us if units[n]["si"] == si]
            y_ref[rows_of[si], :] = jnp.concatenate(
                [yy[n] + ray[n][:, RW_N:] for n in mine], axis=1).astype(bf16)
            for n in mine:
                st_scr[units[n]["h"]] = (s_old[n] * units[n]["wend"] + sg[n]) + (gt[n][RW_N:] + vk[n])
        return carry

    lax.fori_loop(0, nch, chunk, 0)
    sff_ref[...] = stf_scr[...]
    sfb_ref[...] = stb_scr[...]


def _rwkv(pfull, s0f, s0b, mu, wup, w0, aup, a0, k_k, k_a, r_k, ones_bd, gup, nbr):
    bsz, t, _ = pfull.shape
    tb_ = RW_BLOCK
    nb = t // tb_
    cblk = P_RW // P_RW_PAD
    hpb = tb_ // RW_HALO
    nh = t // RW_HALO
    fwd = lambda j: j
    bwd = lambda j: nb - 1 - j

    def p_specs(tb):
        return [
            pl.BlockSpec((bsz, tb_, P_RW_PAD), lambda j: (0, tb(j), cblk)),
            pl.BlockSpec((bsz, RW_HALO, P_RW_PAD), lambda j: (0, jnp.maximum(tb(j) * hpb - 1, 0), cblk)),
            pl.BlockSpec((bsz, RW_HALO, P_RW_PAD), lambda j: (0, jnp.minimum((tb(j) + 1) * hpb, nh - 1), cblk)),
        ]

    full = lambda shape: pl.BlockSpec(shape, lambda j: (0,) * len(shape))
    st_spec = full((bsz, RW_HEADS, RW_N, RW_N))
    tok = lambda tb: pl.BlockSpec((bsz, tb_, RW_W), lambda j: (0, tb(j), 0))
    tok_shape = jax.ShapeDtypeStruct((bsz, t, RW_W), bf16)
    st_shape = jax.ShapeDtypeStruct((bsz, RW_HEADS, RW_N, RW_N), f32)
    return pl.pallas_call(
        functools.partial(_rwkv_kernel, nb=nb, bsz=bsz),
        grid=(nb,),
        in_specs=p_specs(fwd) + p_specs(bwd) + [
            st_spec, st_spec,
            full((1, P_RW_PAD)),
            full((2, RW_DECAY_LORA, RW_W)), full((2, RW_W)),
            full((2, RW_A_LORA, RW_W)), full((2, RW_W)),
            full((1, RW_W)), full((1, RW_W)), full((1, RW_W)),
            full((RW_W, RW_W)),
            full((RW_GATE_LORA, RW_W)),
            full((tb_, tb_)),
        ],
        out_specs=[tok(fwd), tok(fwd), tok(fwd), tok(bwd), tok(bwd), st_spec, st_spec],
        out_shape=[tok_shape] * 5 + [st_shape] * 2,
        scratch_shapes=[pltpu.VMEM((bsz, RW_HEADS, RW_N, RW_N), f32) for _ in range(2)]
        + [pltpu.VMEM((bsz, tb_, RW_W), f32) for _ in range(12)],
        compiler_params=pltpu.CompilerParams(
            dimension_semantics=("arbitrary",), vmem_limit_bytes=VMEM_LIMIT),
        name="rwkv",
    )(pfull, pfull, pfull, pfull, pfull, pfull, s0f, s0b, mu, wup, w0, aup, a0, k_k, k_a, r_k, ones_bd, gup, nbr)


def _merge_kernel(x_ref, hf_ref, hb_ref, o_ref_, yf_ref, yb_ref, bvf_ref, bvb_ref, g_ref, ga_ref, gb_ref,
                  g1_ref, mlg_ref, lnw_ref, lnb_ref, ones_ref, pa_ref, pb_ref, wo_ref, out_ref):
    hsum = hf_ref[...].astype(f32) + hb_ref[...].astype(f32)
    parts = []
    for h in range(ML_HEADS):
        hh = hsum[:, h * ML_DV:(h + 1) * ML_DV]
        parts.append(hh * lax.rsqrt(jnp.mean(hh * hh, axis=-1, keepdims=True) + NORM_EPS))
    a_lat = jnp.concatenate(parts, axis=1) * mlg_ref[...] * _sigmoid(o_ref_[...].astype(f32))
    y = yf_ref[...].astype(f32) + yb_ref[...].astype(f32)
    ones_bd = ones_ref[...]
    inv_n = 1.0 / RW_N
    mu = _head_sum(y, ones_bd) * inv_n
    dev = y - mu
    var = _head_sum(dev * dev, ones_bd) * inv_n
    yn = dev * lax.rsqrt(var + RW_GN_EPS) * lnw_ref[...] + lnb_ref[...]
    b_lat = (yn + (bvf_ref[...].astype(f32) + bvb_ref[...].astype(f32))) * g_ref[...].astype(f32)
    m = (_sigmoid(ga_ref[...].astype(f32)) * _mm(a_lat, pa_ref[...])
         + _sigmoid(gb_ref[...].astype(f32)) * _mm(b_lat, pb_ref[...]))
    out_ref[...] = x_ref[...] + g1_ref[...] * _mm(m, wo_ref[...])


def _merge(x, hf, hb, yf, yb, bvf, bvb, g, pfull, g1, ml_norm_g, ln_w, ln_b, ones_bd, pa_bf, pb_bf, wo_bf):
    bsz, t, _ = x.shape
    tm = min(t, 512)
    full = lambda shape: pl.BlockSpec(shape, lambda b, i: (0,) * len(shape))
    tok = lambda w: pl.BlockSpec((None, tm, w), lambda b, i: (b, i, 0))
    col = lambda w, off: pl.BlockSpec((None, tm, w), lambda b, i: (b, i, off // w))
    return pl.pallas_call(
        _merge_kernel,
        grid=(bsz, t // tm),
        in_specs=[
            tok(D_MODEL), tok(ML_V), tok(ML_V), col(ML_V, P_O),
            tok(RW_W), tok(RW_W), tok(RW_W), tok(RW_W), tok(RW_W),
            col(D_MODEL, P_GA), col(D_MODEL, P_GB),
            pl.BlockSpec((None, 1, D_MODEL), lambda b, i: (b, 0, 0)),
            full((1, ML_V)), full((1, RW_W)), full((1, RW_W)), full((RW_W, RW_W)),
            full((ML_V, D_MODEL)), full((RW_W, D_MODEL)), full((D_MODEL, D_MODEL)),
        ],
        out_specs=tok(D_MODEL),
        out_shape=jax.ShapeDtypeStruct((bsz, t, D_MODEL), f32),
        compiler_params=pltpu.CompilerParams(
            dimension_semantics=("parallel", "parallel"), vmem_limit_bytes=VMEM_LIMIT),
        name="merge",
    )(x, hf, hb, pfull, yf, yb, bvf, bvb, g, pfull, pfull, g1, ml_norm_g, ln_w, ln_b, ones_bd,
      pa_bf, pb_bf, wo_bf)


def _top_max(x):
    return jnp.max(x, axis=0, keepdims=True)


def _first_at(x, val, eid_f):
    return jnp.min(jnp.where(x == val, eid_f, float(ROUTE_W)), axis=0, keepdims=True)


def _split3(x):
    hi = x.astype(bf16)
    r1 = x - hi.astype(f32)
    mid = r1.astype(bf16)
    return hi, mid, (r1 - mid.astype(f32)).astype(bf16)


def _route_kernel(x_ref, g_ref, sc_ref, sh_ref, rw_ref, rb_ref, rt_ref):
    h = _norm_mod(x_ref[...], g_ref[...], sc_ref[...], sh_ref[...])
    logits = _mm3(h, rw_ref[...])
    e_i = lax.broadcasted_iota(jnp.int32, (ROUTE_W, ROUTE_W), 0)
    e_j = lax.broadcasted_iota(jnp.int32, (ROUTE_W, ROUTE_W), 1)
    eye = (e_i == e_j).astype(bf16)
    scores = _sigmoid(_mm_exact_lhs(eye, logits, _NT))
    sel = scores + rb_ref[...]
    eid = lax.broadcasted_iota(jnp.int32, sel.shape, 0)
    eid_f = eid.astype(f32)
    grp = eid // EXPERTS_PER_GROUP
    neg = -jnp.inf
    gr = range(N_GROUPS)
    mg = [jnp.where(grp == gi, sel, neg) for gi in gr]
    m1 = [_top_max(mg[gi]) for gi in gr]
    i1 = [_first_at(mg[gi], m1[gi], eid_f) for gi in gr]
    m2 = [_top_max(jnp.where(eid_f == i1[gi], neg, mg[gi])) for gi in gr]
    best_g = jnp.zeros((1,) + sel.shape[1:], jnp.int32)
    best_v = m1[0] + m2[0]
    for gi in range(1, N_GROUPS):
        gs = m1[gi] + m2[gi]
        upd = gs > best_v
        best_g = jnp.where(upd, gi, best_g)
        best_v = jnp.where(upd, gs, best_v)
    cand = jnp.where(grp == best_g, sel, neg)
    v1 = _top_max(cand)
    p1 = _first_at(cand, v1, eid_f)
    cand2 = jnp.where(eid_f == p1, neg, cand)
    v2 = _top_max(cand2)
    p2 = _first_at(cand2, v2, eid_f)
    picked = (eid_f == p1) | (eid_f == p2)
    w = jnp.where(picked, scores, 0.0)
    comb = w / jnp.sum(w, axis=0, keepdims=True)
    out_t = jnp.where(eid == ROUTE_GID, best_g.astype(f32), comb)
    rt_ref[...] = sum(_dot(part, eye, _TN) for part in _split3(out_t))


def _route(x, g, sc, sh, router_w, router_b):
    bsz, t, _ = x.shape
    tm = min(t, 512)
    pad = ROUTE_W - N_EXPERTS
    rw = jnp.pad(router_w, ((0, 0), (0, pad)))
    rb = jnp.pad(router_b.reshape(N_EXPERTS, 1), ((0, pad), (0, 0)))
    return pl.pallas_call(
        _route_kernel,
        grid=(bsz, t // tm),
        in_specs=[
            pl.BlockSpec((None, tm, D_MODEL), lambda b, i: (b, i, 0)),
            pl.BlockSpec((1, D_MODEL), lambda b, i: (0, 0)),
            pl.BlockSpec((None, 1, D_MODEL), lambda b, i: (b, 0, 0)),
            pl.BlockSpec((None, 1, D_MODEL), lambda b, i: (b, 0, 0)),
            pl.BlockSpec((D_MODEL, ROUTE_W), lambda b, i: (0, 0)),
            pl.BlockSpec((ROUTE_W, 1), lambda b, i: (0, 0)),
        ],
        out_specs=pl.BlockSpec((None, tm, ROUTE_W), lambda b, i: (b, i, 0)),
        out_shape=jax.ShapeDtypeStruct((bsz, t, ROUTE_W), f32),
        compiler_params=pltpu.CompilerParams(
            dimension_semantics=("parallel", "parallel"), vmem_limit_bytes=VMEM_LIMIT),
        name="route",
    )(x, g, sc, sh, rw, rb)


def _moe_rows(tm):
    need = tm + (N_GROUPS - 1) * (MOE_ALIGN - 1) + MOE_SB
    return -(-need // MOE_SB) * MOE_SB


def _moe_kernel(meta_ref, x_ref, g_ref, sc_ref, sh_ref, g2_ref, rt_ref, fg_ref, wg_ref, wu_ref, wd_ref, o_ref,
                xs_scr, ys_scr, cs_scr, pos_scr, *, final_norm, tm, nt):
    sb = MOE_SB
    n_rows = _moe_rows(tm)
    grp = pl.program_id(2)
    base = (pl.program_id(0) * nt + pl.program_id(1)) * (2 * N_GROUPS)

    @pl.when(grp == 0)
    def _():
        h = _norm_mod(x_ref[...], g_ref[...], sc_ref[...], sh_ref[...]).astype(bf16)
        rt = rt_ref[...]
        lane = lax.broadcasted_iota(jnp.int32, rt.shape, 1)
        gid = jnp.sum(jnp.where(lane == ROUTE_GID, rt, 0.0), axis=-1, keepdims=True)
        g8 = lax.broadcasted_iota(jnp.int32, (tm, 8), 1)
        onehot = gid == g8.astype(f32)
        ii = lax.broadcasted_iota(jnp.int32, (tm, tm), 0)
        jj = lax.broadcasted_iota(jnp.int32, (tm, tm), 1)
        rank = _dot((jj <= ii).astype(bf16), onehot.astype(bf16))
        start = jnp.zeros((1, 8), f32)
        g8r = lax.broadcasted_iota(jnp.int32, (1, 8), 1)
        for gi in range(N_GROUPS):
            start = jnp.where(g8r == gi, meta_ref[base + gi].astype(f32), start)
        pos = jnp.sum(jnp.where(onehot, (start + rank) - 1.0, 0.0), axis=-1, keepdims=True)
        posmat = jnp.broadcast_to(pos, (tm, 8))
        pos_scr[...] = posmat
        e_i = lax.broadcasted_iota(jnp.int32, (8, 8), 0)
        e_j = lax.broadcasted_iota(jnp.int32, (8, 8), 1)
        pos_row = _mm_exact_lhs((e_i == e_j).astype(bf16), posmat, _NT)[0:1, :]
        hi = rt.astype(bf16)
        r1 = rt - hi.astype(f32)
        mid = r1.astype(bf16)
        lo = (r1 - mid.astype(f32)).astype(bf16)
        rt3 = jnp.concatenate([hi, mid, lo], axis=1)
        for c in range(n_rows // sb):
            rid = (lax.broadcasted_iota(jnp.int32, (sb, 1), 0) + c * sb).astype(f32)
            perm = (rid == pos_row).astype(bf16)
            xs_scr[c * sb:(c + 1) * sb, :] = _dot(perm, h).astype(bf16)
            cc = _dot(perm, rt3)
            cs_scr[c * sb:(c + 1) * sb, :] = cc[:, :ROUTE_W] + (cc[:, ROUTE_W:2 * ROUTE_W] + cc[:, 2 * ROUTE_W:])
        ys_scr[...] = jnp.zeros_like(ys_scr)

    seg_start = meta_ref[base + grp]
    seg_blocks = meta_ref[base + N_GROUPS + grp]
    lane_sb = lax.broadcasted_iota(jnp.int32, (sb, ROUTE_W), 1)

    def block(kb, carry):
        rows = pl.ds(pl.multiple_of(seg_start + kb * sb, MOE_ALIGN), sb)
        xs = xs_scr[rows, :]
        cs = cs_scr[rows, :]
        acc = None
        for k in range(EXPERTS_PER_GROUP):
            gate = _dot(xs, wg_ref[k])
            hid = (gate * _sigmoid(gate)) * _dot(xs, wu_ref[k])
            ce = jnp.sum(jnp.where(lane_sb == grp * EXPERTS_PER_GROUP + k, cs, 0.0), axis=-1, keepdims=True)
            part = _dot((ce * hid).astype(bf16), wd_ref[k])
            acc = part if acc is None else acc + part
        ys_scr[rows, :] = acc.astype(bf16)
        return carry

    lax.fori_loop(0, seg_blocks, block, 0)

    @pl.when(grp == N_GROUPS - 1)
    def _():
        cid = lax.broadcasted_iota(jnp.int32, (1, n_rows), 1).astype(f32)
        chunks = [slice(c * MOE_OUT_ROWS, (c + 1) * MOE_OUT_ROWS) for c in range(tm // MOE_OUT_ROWS)]
        ys = [_dot((pos_scr[rows, 0:1] == cid).astype(bf16), ys_scr[...]) for rows in chunks]
        ys = [x_ref[rows, :] + g2_ref[...] * ys[c] for c, rows in enumerate(chunks)]
        if final_norm:
            ms = [jnp.mean(y * y, axis=-1, keepdims=True) for y in ys]
            ys = [y * lax.rsqrt(ms[c] + NORM_EPS) * fg_ref[...] for c, y in enumerate(ys)]
        for c, rows in enumerate(chunks):
            o_ref[rows, :] = ys[c]


def _moe(x, g, sc, sh, g2, rt, final_g, final_norm, w_gate, w_up, w_down):
    bsz, t, _ = x.shape
    tm = min(t, 1024)
    nt = t // tm
    sb = MOE_SB
    n_rows = _moe_rows(tm)
    gid = rt[..., ROUTE_GID].astype(jnp.int32).reshape(bsz, nt, tm)
    cnt = jnp.sum(gid[..., None] == jnp.arange(N_GROUPS), axis=2).astype(jnp.int32)
    nblk = (cnt + (sb - 1)) // sb
    seg = ((cnt + (MOE_ALIGN - 1)) // MOE_ALIGN) * MOE_ALIGN
    start = jnp.cumsum(seg, axis=-1) - seg
    meta = jnp.concatenate([start, nblk], axis=-1).reshape(-1).astype(jnp.int32)
    wspec = lambda shape: pl.BlockSpec((EXPERTS_PER_GROUP,) + shape, lambda b, i, e, m: (e, 0, 0))
    grid_spec = pltpu.PrefetchScalarGridSpec(
        num_scalar_prefetch=1,
        grid=(bsz, nt, N_GROUPS),
        in_specs=[
            pl.BlockSpec((None, tm, D_MODEL), lambda b, i, e, m: (b, i, 0)),
            pl.BlockSpec((1, D_MODEL), lambda b, i, e, m: (0, 0)),
            pl.BlockSpec((None, 1, D_MODEL), lambda b, i, e, m: (b, 0, 0)),
            pl.BlockSpec((None, 1, D_MODEL), lambda b, i, e, m: (b, 0, 0)),
            pl.BlockSpec((None, 1, D_MODEL), lambda b, i, e, m: (b, 0, 0)),
            pl.BlockSpec((None, tm, ROUTE_W), lambda b, i, e, m: (b, i, 0)),
            pl.BlockSpec((1, D_MODEL), lambda b, i, e, m: (0, 0)),
            wspec((D_MODEL, D_EXPERT)), wspec((D_MODEL, D_EXPERT)), wspec((D_EXPERT, D_MODEL)),
        ],
        out_specs=pl.BlockSpec((None, tm, D_MODEL), lambda b, i, e, m: (b, i, 0)),
        scratch_shapes=[pltpu.VMEM((n_rows, D_MODEL), bf16), pltpu.VMEM((n_rows, D_MODEL), bf16),
                        pltpu.VMEM((n_rows, ROUTE_W), f32), pltpu.VMEM((tm, 8), f32)],
    )
    return pl.pallas_call(
        functools.partial(_moe_kernel, final_norm=final_norm, tm=tm, nt=nt),
        grid_spec=grid_spec,
        out_shape=jax.ShapeDtypeStruct((bsz, t, D_MODEL), f32),
        compiler_params=pltpu.CompilerParams(
            dimension_semantics=("parallel", "parallel", "arbitrary"), vmem_limit_bytes=MOE_VMEM_LIMIT),
        name="moe",
    )(meta, x, g, sc, sh, g2, rt, final_g, w_gate, w_up, w_down)


def _cast_kernel(a_ref, b_ref, c_ref, ao_ref, bo_ref, co_ref):
    ao_ref[...] = a_ref[...].astype(bf16)
    bo_ref[...] = b_ref[...].astype(bf16)
    co_ref[...] = c_ref[...].astype(bf16)


def _cast_experts(w_gate, w_up, w_down, l):
    per = 2
    shapes = [(D_MODEL, D_EXPERT), (D_MODEL, D_EXPERT), (D_EXPERT, D_MODEL)]
    return pl.pallas_call(
        _cast_kernel,
        grid=(N_EXPERTS // per,),
        in_specs=[pl.BlockSpec((None, per) + s, lambda i: (l, i, 0, 0)) for s in shapes],
        out_specs=[pl.BlockSpec((per,) + s, lambda i: (i, 0, 0)) for s in shapes],
        out_shape=[jax.ShapeDtypeStruct((N_EXPERTS,) + s, bf16) for s in shapes],
        compiler_params=pltpu.CompilerParams(
            dimension_semantics=("arbitrary",), vmem_limit_bytes=VMEM_LIMIT),
        name="cast_experts",
    )(w_gate, w_up, w_down)


def _pack_w_in(w_in):
    ml, rw, gt = w_in[:, :ML_COLS], w_in[:, ML_COLS:ML_COLS + RW_COLS], w_in[:, ML_COLS + RW_COLS:]
    qkvo, mlg = ml[:, :2 * ML_QK + 2 * ML_V], ml[:, 2 * ML_QK + 2 * ML_V:]
    z = lambda n: jnp.zeros((D_MODEL, n), w_in.dtype)
    return jnp.concatenate(
        [gt, rw, mlg, z(P_RW_PAD - RW_COLS - 4 * ML_HEADS), qkvo], axis=1).astype(bf16)


def _mixer(pfull, is_ctx, lp, ml_state, rw_state):
    conv_args = (pfull, lp["taps"], lp["conv_b"], lp["gate_b"])
    h_f, c_f, qk_conv = _mlstm(*conv_args, ml_state[0], rev=False, d=0, grid_conv=not is_ctx)
    h_b, c_b = _mlstm(*conv_args, ml_state[1], rev=True, d=1, grid_conv=not is_ctx, qk_conv=qk_conv)
    hs, ml_fin = [h_f, h_b], [c_f, c_b]
    yf, bvf, g, yb, bvb, sff, sfb = _rwkv(pfull, rw_state[0], rw_state[1], lp["rw_mu"], lp["rw_w_up"],
                                          lp["rw_w0"], lp["rw_a_up"], lp["rw_a0"], lp["rw_k_k"], lp["rw_k_a"],
                                          lp["rw_r_k"], lp["ones_bd"], lp["gup"], lp["nbr"])
    return (hs[0], hs[1], yf, yb, bvf, bvb, g), ml_fin, [sff, sfb]


def kernel(x, c, ctx, c_ctx, w_ada, b_ada, norm1_g, norm2_g, w_in, ml_conv_k, ml_conv_b, ml_gate_b, ml_norm_g, rw_mu, rw_w_up, rw_w0, rw_a_up, rw_a0, rw_g_up, rw_k_k, rw_k_a, rw_r_k, rw_ln_w, rw_ln_b, merge_pa, merge_pb, w_out, router_w, router_b, exp_w_gate, exp_w_up, exp_w_down, final_g):
    bsz = x.shape[0]
    s_rows = jnp.zeros((8, D_MODEL), f32).at[:bsz].set(c).at[bsz].set(c_ctx)
    mod = _ada(s_rows, w_ada, b_ada)
    head_id = jnp.arange(RW_W) // RW_N
    ones_bd = (head_id[:, None] == head_id[None, :]).astype(bf16)
    tok_id = jnp.arange(RW_BLOCK)
    nbr = (0.5 * (jnp.abs(tok_id[:, None] - tok_id[None, :]) == 1)).astype(bf16)
    row = lambda v: v.reshape(1, -1)

    x_lat, x_ctx = x, ctx
    for l in range(DEPTH):
        last = l == DEPTH - 1
        mod_lat = mod[l, :bsz].reshape(bsz, 1, N_MOD, D_MODEL)
        mod_ctx = jnp.broadcast_to(mod[l, bsz].reshape(1, 1, N_MOD, D_MODEL), (bsz, 1, N_MOD, D_MODEL))
        lp = dict(taps=ml_conv_k[l].reshape(9, 2 * ML_QK), conv_b=row(ml_conv_b[l]), gate_b=row(ml_gate_b[l]),
                  rw_mu=jnp.pad(row(rw_mu[l]), ((0, 0), (0, P_RW_PAD - RW_COLS))),
                  rw_w_up=rw_w_up[l].astype(bf16), rw_w0=rw_w0[l], rw_a_up=rw_a_up[l].astype(bf16),
                  rw_a0=rw_a0[l], rw_k_k=row(rw_k_k[l]), rw_k_a=row(rw_k_a[l]), rw_r_k=row(rw_r_k[l]),
                  ones_bd=ones_bd, gup=rw_g_up[l].astype(bf16), nbr=nbr)
        w_in_bf = _pack_w_in(w_in[l])
        pa_bf, pb_bf, wo_bf = merge_pa[l].astype(bf16), merge_pb[l].astype(bf16), w_out[l].astype(bf16)
        experts = _cast_experts(exp_w_gate, exp_w_up, exp_w_down, l)
        g1n, g2n = row(norm1_g[l]), row(norm2_g[l])
        readout = (row(ml_norm_g[l]), row(rw_ln_w[l]), row(rw_ln_b[l]), ones_bd, pa_bf, pb_bf, wo_bf)

        def m(modv, i):
            return modv[:, :, i]

        p_ctx = _proj(x_ctx, g1n, m(mod_ctx, 1), m(mod_ctx, 0), w_in_bf)
        ml0 = [jnp.zeros((bsz, ML_HEADS, ML_DQK, 2 * ML_DV), f32)] * 2
        rw0 = [jnp.zeros((bsz, RW_HEADS, RW_N, RW_N), f32)] * 2
        mix_c, ml_st, rw_st = _mixer(p_ctx, True, lp, ml0, rw0)

        p_lat = _proj(x_lat, g1n, m(mod_lat, 1), m(mod_lat, 0), w_in_bf)
        mix_l, _, _ = _mixer(p_lat, False, lp, ml_st, rw_st)
        x_lat = _merge(x_lat, *mix_l, p_lat, m(mod_lat, 2), *readout)
        comb = _route(x_lat, g2n, m(mod_lat, 4), m(mod_lat, 3), router_w, router_b)
        x_lat = _moe(x_lat, g2n, m(mod_lat, 4), m(mod_lat, 3), m(mod_lat, 5), comb, row(final_g), last, *experts)
        if not last:
            x_ctx = _merge(x_ctx, *mix_c, p_ctx, m(mod_ctx, 2), *readout)
            comb_c = _route(x_ctx, g2n, m(mod_ctx, 4), m(mod_ctx, 3), router_w, router_b)
            x_ctx = _moe(x_ctx, g2n, m(mod_ctx, 4), m(mod_ctx, 3), m(mod_ctx, 5), comb_c, row(final_g), False,
                         *experts)
    return x_lat
```

```python
import functools

import jax
import jax.numpy as jnp
import numpy as np
from jax import lax
from jax.experimental import pallas as pl
from jax.experimental.pallas import tpu as pltpu

f32 = jnp.float32
bf16 = jnp.bfloat16

D_MODEL = 1024
DEPTH = 2
GRID_W = 64
N_MOD = 6
NORM_EPS = 1e-6

ML_HEADS = 4
ML_DQK = 64
ML_DV = 128
ML_QK = ML_HEADS * ML_DQK
ML_V = ML_HEADS * ML_DV
GATE_CAP = 15.0
ML_COLS = 2 * ML_QK + 2 * ML_V + 4 * ML_HEADS
ML_CHUNK = 256

RW_HEADS = 8
RW_N = 64
RW_W = RW_HEADS * RW_N
RW_DECAY_LORA = 64
RW_A_LORA = 64
RW_GATE_LORA = 128
RW_GN_EPS = 6.4e-4
RW_COLS = 3 * RW_W + 2 * RW_DECAY_LORA + 2 * RW_A_LORA + RW_GATE_LORA
RW_CHUNK = 64
RW_BLOCK = 256
RW_HALO = 16

N_EXPERTS = 16
N_GROUPS = 4
EXPERTS_PER_GROUP = N_EXPERTS // N_GROUPS
D_EXPERT = 512

P_GA = 0
P_GB = D_MODEL
P_RW = 2 * D_MODEL
P_RW_PAD = 2048
P_MLG = P_RW + RW_COLS
P_MLG_BLK = 128
P_QK = P_RW + P_RW_PAD
P_V = P_QK + 2 * ML_QK
P_O = P_V + ML_V
P_COLS = P_O + ML_V
PROJ_TN = P_COLS // 2

VMEM_LIMIT = 48 * 1024 * 1024
MOE_VMEM_LIMIT = 58 * 1024 * 1024
ROUTE_W = 32
ROUTE_GID = N_EXPERTS
MOE_SB = 256
MOE_ALIGN = 16
EXP_NEG_HALF = float(np.exp(-0.5))
NEG_BIG = -1e30


_NN = ((1,), (0,))
_NT = ((1,), (1,))
_TN = ((0,), (0,))


def _dot(a, b, dims=_NN):
    return lax.dot_general(a, b, (dims, ((), ())), preferred_element_type=f32)


def _mm(a, b, dims=_NN):
    return _dot(a.astype(bf16), b.astype(bf16), dims)


def _hi_lo(x):
    hi = x.astype(bf16)
    lo = (x - hi.astype(f32)).astype(bf16)
    return hi, lo


def _mm3(a, b, dims=_NN):
    ah, al = _hi_lo(a)
    bh, bl = _hi_lo(b)
    return _dot(ah, bh, dims) + (_dot(ah, bl, dims) + _dot(al, bh, dims))


def _mm_exact_lhs(a_bf, b, dims=_NN):
    hi = b.astype(bf16)
    r1 = b - hi.astype(f32)
    mid = r1.astype(bf16)
    lo = (r1 - mid.astype(f32)).astype(bf16)
    return _dot(a_bf, hi, dims) + (_dot(a_bf, mid, dims) + _dot(a_bf, lo, dims))


def _head_sum(a, ones_bd):
    return _dot(a.astype(bf16), ones_bd)


def _sigmoid(x):
    return 1.0 / (1.0 + jnp.exp(-x))


def _norm_mod(x, g, sc, sh):
    y = x * lax.rsqrt(jnp.mean(x * x, axis=-1, keepdims=True) + NORM_EPS)
    return (y * g) * (1.0 + sc) + sh


def _ada_kernel(s_ref, w_ref, b_ref, o_ref):
    s = s_ref[...]
    s = s * _sigmoid(s)
    o_ref[...] = _mm3(s, w_ref[...]) + b_ref[...]


def _ada(s_rows, w_ada, b_ada):
    tn = 1536
    n = N_MOD * D_MODEL
    return pl.pallas_call(
        _ada_kernel,
        grid=(DEPTH, n // tn),
        in_specs=[
            pl.BlockSpec((8, D_MODEL), lambda l, j: (0, 0)),
            pl.BlockSpec((None, D_MODEL, tn), lambda l, j: (l, 0, j)),
            pl.BlockSpec((None, 1, tn), lambda l, j: (l, 0, j)),
        ],
        out_specs=pl.BlockSpec((None, 8, tn), lambda l, j: (l, 0, j)),
        out_shape=jax.ShapeDtypeStruct((DEPTH, 8, n), f32),
        compiler_params=pltpu.CompilerParams(
            dimension_semantics=("arbitrary", "arbitrary"), vmem_limit_bytes=VMEM_LIMIT),
        name="ada",
    )(s_rows, w_ada, b_ada.reshape(DEPTH, 1, n))


def _proj_kernel(x_ref, g_ref, sc_ref, sh_ref, w_ref, o_ref, h_scr):
    @pl.when(pl.program_id(2) == 0)
    def _():
        h_scr[...] = _norm_mod(x_ref[...], g_ref[...], sc_ref[...], sh_ref[...]).astype(bf16)

    o_ref[...] = _dot(h_scr[...], w_ref[...]).astype(bf16)


def _proj(x, g, sc, sh, w_bf):
    bsz, t, _ = x.shape
    tm = min(t, 1024)
    return pl.pallas_call(
        _proj_kernel,
        grid=(bsz, t // tm, P_COLS // PROJ_TN),
        in_specs=[
            pl.BlockSpec((None, tm, D_MODEL), lambda b, i, j: (b, i, 0)),
            pl.BlockSpec((1, D_MODEL), lambda b, i, j: (0, 0)),
            pl.BlockSpec((None, 1, D_MODEL), lambda b, i, j: (b, 0, 0)),
            pl.BlockSpec((None, 1, D_MODEL), lambda b, i, j: (b, 0, 0)),
            pl.BlockSpec((D_MODEL, PROJ_TN), lambda b, i, j: (0, j)),
        ],
        out_specs=pl.BlockSpec((None, tm, PROJ_TN), lambda b, i, j: (b, i, j)),
        out_shape=jax.ShapeDtypeStruct((bsz, t, P_COLS), bf16),
        scratch_shapes=[pltpu.VMEM((tm, D_MODEL), bf16)],
        compiler_params=pltpu.CompilerParams(
            dimension_semantics=("parallel", "parallel", "arbitrary"), vmem_limit_bytes=VMEM_LIMIT),
        name="proj",
    )(x, g, sc, sh, w_bf)


def _shift_rows(u, up_row, dn_row):
    n = u.shape[0]
    rid = lax.broadcasted_iota(jnp.int32, (n, 1), 0)
    up = jnp.where(rid == 0, up_row, pltpu.roll(u, 1, axis=0))
    dn = jnp.where(rid == n - 1, dn_row, pltpu.roll(u, n - 1, axis=0))
    return up, dn


def _log_sigmoid(x):
    return jnp.minimum(x, 0.0) - jnp.log1p(jnp.exp(-jnp.abs(x)))


def _mlstm_kernel(*refs, rev, d, grid_conv, nb, bsz, conv_done):
    if conv_done:
        qk_ref, v_ref, mlg_ref, gb_ref, c0_ref, h_ref, cfin_ref, c_scr = refs
    else:
        (qk_ref, qkp_ref, qkn_ref, v_ref, mlg_ref, taps_ref, cb_ref, gb_ref, c0_ref,
         h_ref, cfin_ref, qkc_ref, c_scr) = refs
    L = ML_CHUNK
    j = pl.program_id(0)
    jblk = (nb - 1 - j) if rev else j

    @pl.when(j == 0)
    def _():
        c_scr[...] = c0_ref[...]

    zero_row = jnp.zeros((1, 2 * ML_QK), f32)
    rid = lax.broadcasted_iota(jnp.int32, (L, 1), 0)
    if grid_conv:
        first_col = (rid % GRID_W) == 0
        last_col = (rid % GRID_W) == GRID_W - 1
    else:
        first_col = rid == 0
        last_col = rid == L - 1
    ii = lax.broadcasted_iota(jnp.int32, (L, L), 0)
    jj = lax.broadcasted_iota(jnp.int32, (L, L), 1)
    incl = (jj >= ii) if rev else (jj <= ii)
    tri = incl.astype(bf16)
    e_i = lax.broadcasted_iota(jnp.int32, (2 * ML_HEADS, 2 * ML_HEADS), 0)
    e_j = lax.broadcasted_iota(jnp.int32, (2 * ML_HEADS, 2 * ML_HEADS), 1)
    eye = (e_i == e_j).astype(bf16)
    one_col = (lax.broadcasted_iota(jnp.int32, (L, ML_DV), 1) == 0).astype(bf16)

    units = []
    for b in range(bsz):
        qk = qk_ref[b].astype(f32)
        if conv_done:
            q, k = qk[:, :ML_QK], qk[:, ML_QK:]
        else:
            taps = taps_ref[...]
            if grid_conv:
                above = jnp.where(jblk > 0, qkp_ref[b].astype(f32), 0.0)
                below = jnp.where(jblk < nb - 1, qkn_ref[b].astype(f32), 0.0)
                ext = jnp.concatenate([above, qk, below], axis=0)
                bases = [(dr, ext[dr * GRID_W:dr * GRID_W + L]) for dr in range(3)]
            else:
                bases = [(1, qk)]
            conv = cb_ref[...]
            for dr, base in bases:
                up, dn = _shift_rows(base, zero_row, zero_row)
                conv = conv + (taps[3 * dr:3 * dr + 1] * jnp.where(first_col, 0.0, up)
                               + taps[3 * dr + 1:3 * dr + 2] * base
                               + taps[3 * dr + 2:3 * dr + 3] * jnp.where(last_col, 0.0, dn))
            q = conv[:, :ML_QK]
            k = conv[:, ML_QK:] * (ML_DQK ** -0.5)
            qkc_ref[b] = jnp.concatenate([q, k], axis=1).astype(bf16)
        pre = mlg_ref[b][:, :4 * ML_HEADS].astype(f32) + gb_ref[...]
        pre = GATE_CAP * jnp.tanh(pre * (1.0 / GATE_CAP))
        ig = pre[:, d * ML_HEADS:(d + 1) * ML_HEADS]
        lf = _log_sigmoid(pre[:, (2 + d) * ML_HEADS:(3 + d) * ML_HEADS])
        bc = _mm_exact_lhs(tri, lf)
        rows_ = _mm_exact_lhs(eye, jnp.concatenate([ig, bc], axis=1), _NT)
        v = v_ref[b]
        for h in range(ML_HEADS):
            b_col = bc[:, h:h + 1]
            units.append(dict(
                b=b, h=h, q=q[:, h * ML_DQK:(h + 1) * ML_DQK], k=k[:, h * ML_DQK:(h + 1) * ML_DQK],
                vp=jnp.concatenate([v[:, h * ML_DV:(h + 1) * ML_DV], one_col], axis=1),
                b_col=b_col, ig_col=ig[:, h:h + 1], btot=b_col[0:1, :] if rev else b_col[L - 1:L, :],
                b_row=rows_[ML_HEADS + h:ML_HEADS + h + 1, :], ig_row=rows_[h:h + 1, :]))
    decay = [jnp.exp(jnp.where(incl, (u["b_col"] - u["b_row"]) + u["ig_row"], NEG_BIG)) for u in units]
    qk_s = [_mm(u["q"], u["k"], _NT) for u in units]
    qc = [_mm(u["q"], c_scr[u["b"], u["h"]]) for u in units]
    kw = [u["k"] * jnp.exp((u["btot"] - u["b_col"]) + u["ig_col"]) for u in units]
    kv = [_mm(kw[n], u["vp"], _TN) for n, u in enumerate(units)]
    sv = [_mm(qk_s[n] * decay[n], u["vp"]) for n, u in enumerate(units)]
    for n, u in enumerate(units):
        b, h = u["b"], u["h"]
        nd = sv[n] + jnp.exp(u["b_col"]) * qc[n]
        den = nd[:, ML_DV:ML_DV + 1]
        h_ref[b, :, h * ML_DV:(h + 1) * ML_DV] = (nd[:, :ML_DV] / jnp.maximum(jnp.abs(den), 1.0)).astype(bf16)
        c_scr[b, h] = jnp.exp(u["btot"]) * c_scr[b, h] + kv[n]
    cfin_ref[...] = c_scr[...]


def _mlstm(pfull, taps, conv_b, gate_b, c0, rev, d, grid_conv, qk_conv=None):
    bsz, t, _ = pfull.shape
    L = ML_CHUNK
    nb = t // L
    tb = (lambda j: nb - 1 - j) if rev else (lambda j: j)
    vblk = P_V // ML_V
    qkblk = P_QK // (2 * ML_QK)
    rpb = L // GRID_W
    n_rows = t // GRID_W
    conv_done = qk_conv is not None
    tok = lambda w, blk: pl.BlockSpec((bsz, L, w), lambda j: (0, tb(j), blk))
    full = lambda shape: pl.BlockSpec(shape, lambda j: (0,) * len(shape))
    st_spec = full((bsz, ML_HEADS, ML_DQK, 2 * ML_DV))
    common = [tok(ML_V, vblk), tok(P_MLG_BLK, P_MLG // P_MLG_BLK)]
    if conv_done:
        in_specs = [tok(2 * ML_QK, 0)] + common + [full((1, 4 * ML_HEADS)), st_spec]
        args = (qk_conv, pfull, pfull, gate_b, c0)
    else:
        in_specs = [
            tok(2 * ML_QK, qkblk),
            pl.BlockSpec((bsz, GRID_W, 2 * ML_QK), lambda j: (0, jnp.maximum(tb(j) * rpb - 1, 0), qkblk)),
            pl.BlockSpec((bsz, GRID_W, 2 * ML_QK),
                         lambda j: (0, jnp.minimum((tb(j) + 1) * rpb, n_rows - 1), qkblk)),
        ] + common + [full((9, 2 * ML_QK)), full((1, 2 * ML_QK)), full((1, 4 * ML_HEADS)), st_spec]
        args = (pfull, pfull, pfull, pfull, pfull, taps, conv_b, gate_b, c0)
    out_specs = [tok(ML_V, 0), st_spec]
    out_shape = [jax.ShapeDtypeStruct((bsz, t, ML_V), bf16),
                 jax.ShapeDtypeStruct((bsz, ML_HEADS, ML_DQK, 2 * ML_DV), f32)]
    if not conv_done:
        out_specs.append(tok(2 * ML_QK, 0))
        out_shape.append(jax.ShapeDtypeStruct((bsz, t, 2 * ML_QK), bf16))
    return pl.pallas_call(
        functools.partial(_mlstm_kernel, rev=rev, d=d, grid_conv=grid_conv, nb=nb, bsz=bsz,
                          conv_done=conv_done),
        grid=(nb,),
        in_specs=in_specs,
        out_specs=out_specs,
        out_shape=out_shape,
        scratch_shapes=[pltpu.VMEM((bsz, ML_HEADS, ML_DQK, 2 * ML_DV), f32)],
        compiler_params=pltpu.CompilerParams(
            dimension_semantics=("arbitrary",), vmem_limit_bytes=VMEM_LIMIT),
        name="mlstm_bwd" if rev else "mlstm_fwd",
    )(*args)


def _rw_prepare(p_ref, pp_ref, pn_ref, jblk, nb, d, par, scr, bv_ref):
    mu_ref, wup_ref, w0_ref, aup_ref, a0_ref, kk_ref, ka_ref, rk_ref, ones_ref, nbr_ref = par
    r_scr, v_scr, kk_scr, kd_scr, be_scr, ld_scr = scr
    tb_ = RW_BLOCK
    p_bf = p_ref[...]
    p = p_bf.astype(f32)
    nb_avg = _dot(nbr_ref[...], p_bf)
    rid8 = lax.broadcasted_iota(jnp.int32, (8, 1), 0)
    prev_row = jnp.where(jblk > 0, pp_ref[RW_HALO - 1:RW_HALO, :].astype(f32), 0.0)
    next_row = jnp.where(jblk < nb - 1, pn_ref[0:1, :].astype(f32), 0.0)
    nb_avg = jnp.concatenate([nb_avg[0:8] + jnp.where(rid8 == 0, 0.5 * prev_row, 0.0),
                              nb_avg[8:tb_ - 8],
                              nb_avg[tb_ - 8:] + jnp.where(rid8 == 7, 0.5 * next_row, 0.0)], axis=0)
    p = p + mu_ref[...] * (nb_avg - p)
    r = p[:, 0:RW_W]
    k = p[:, RW_W:2 * RW_W]
    v = p[:, 2 * RW_W:3 * RW_W]
    o_wd = 3 * RW_W + d * RW_DECAY_LORA
    o_ad = 3 * RW_W + 2 * RW_DECAY_LORA + d * RW_A_LORA
    lw = w0_ref[d:d + 1, :] + _mm(jnp.tanh(p[:, o_wd:o_wd + RW_DECAY_LORA]), wup_ref[d])
    ld_scr[...] = -EXP_NEG_HALF * _sigmoid(lw)
    a = _sigmoid(a0_ref[d:d + 1, :] + _mm(p[:, o_ad:o_ad + RW_A_LORA], aup_ref[d]))
    ones_bd = ones_ref[...]
    kkr = k * kk_ref[...]
    kk = kkr * lax.rsqrt(_head_sum(kkr * kkr, ones_bd) + 1e-12)
    kd = k * (1.0 + (a - 1.0) * ka_ref[...])
    bv_ref[...] = (_head_sum(r * kd * rk_ref[...], ones_bd) * v).astype(bf16)
    r_scr[...] = r
    v_scr[...] = v
    kk_scr[...] = kk
    kd_scr[...] = kd
    be_scr[...] = kk * a
    return p


def _rwkv_kernel(pf_ref, pfp_ref, pfn_ref, pb_ref, pbp_ref, pbn_ref, s0f_ref, s0b_ref,
                 mu_ref, wup_ref, w0_ref, aup_ref, a0_ref, kk_ref, ka_ref, rk_ref, ones_ref, gup_ref, nbr_ref,
                 yf_ref, bvf_ref, g_ref, yb_ref, bvb_ref, sff_ref, sfb_ref,
                 stf_scr, stb_scr, *scr, nb, bsz):
    L = RW_CHUNK
    nch = RW_BLOCK // L
    j = pl.program_id(0)

    @pl.when(j == 0)
    def _():
        stf_scr[...] = s0f_ref[...]
        stb_scr[...] = s0b_ref[...]

    par = (mu_ref, wup_ref, w0_ref, aup_ref, a0_ref, kk_ref, ka_ref, rk_ref, ones_ref, nbr_ref)
    scr_f, scr_b = scr[:6], scr[6:]
    o_gd = 3 * RW_W + 2 * RW_DECAY_LORA + 2 * RW_A_LORA
    streams = []
    for b in range(bsz):
        sf = tuple(s.at[b] for s in scr_f)
        sb_ = tuple(s.at[b] for s in scr_b)
        p_f = _rw_prepare(pf_ref.at[b], pfp_ref.at[b], pfn_ref.at[b], j, nb, 0, par, sf, bvf_ref.at[b])
        _rw_prepare(pb_ref.at[b], pbp_ref.at[b], pbn_ref.at[b], nb - 1 - j, nb, 1, par, sb_, bvb_ref.at[b])
        g_ref[b] = _mm(_sigmoid(p_f[:, o_gd:o_gd + RW_GATE_LORA]), gup_ref[...]).astype(bf16)
        streams.append((sf, stf_scr.at[b], yf_ref.at[b], False))
        streams.append((sb_, stb_scr.at[b], yb_ref.at[b], True))

    ii = lax.broadcasted_iota(jnp.int32, (L, L), 0)
    jj = lax.broadcasted_iota(jnp.int32, (L, L), 1)
    ii2 = lax.broadcasted_iota(jnp.int32, (L, 2 * L), 0)
    jj2 = lax.broadcasted_iota(jnp.int32, (L, 2 * L), 1) & (L - 1)
    masks = []
    for rev in (False, True):
        incl = (jj >= ii) if rev else (jj <= ii)
        strict = (jj > ii) if rev else (jj < ii)
        incl2 = (jj2 >= ii2) if rev else (jj2 <= ii2)
        masks.append((incl2, strict, incl.astype(bf16)))

    def chunk(i, carry):
        units = []
        rows_of = []
        for si, (sc, st_scr, y_ref, rev) in enumerate(streams):
            r_scr, v_scr, kk_scr, kd_scr, be_scr, ld_scr = sc
            incl2, strict, tri = masks[int(rev)]
            c = (nch - 1 - i) if rev else i
            rows = pl.ds(pl.multiple_of(c * L, L), L)
            rows_of.append(rows)
            ld = ld_scr[rows, :]
            b = _mm_exact_lhs(tri, ld)
            btot = b[0:1, :] if rev else b[L - 1:L, :]
            e_nb = jnp.exp(-b)
            e_end = jnp.exp(btot - b)
            kd_c = kd_scr[rows, :]
            be_c = be_scr[rows, :]
            v_c = v_scr[rows, :]
            aw = -kk_scr[rows, :] * jnp.exp(b - ld)
            rw = r_scr[rows, :] * jnp.exp(b)
            bi = be_c * e_nb
            ki = kd_c * e_nb
            bw = be_c * e_end
            kw = kd_c * e_end
            wend = jnp.exp(btot)
            for h in range(RW_HEADS):
                c_ = slice(h * RW_N, (h + 1) * RW_N)
                units.append(dict(si=si, h=h, incl2=incl2, strict=strict, st=st_scr, aw=aw[:, c_], rw=rw[:, c_],
                                  bi=bi[:, c_], ki=ki[:, c_], bw=bw[:, c_], kw=kw[:, c_], v=v_c[:, c_],
                                  wend=wend[:, c_]))
        us = range(len(units))
        m = [_mm(jnp.concatenate([u["aw"], u["rw"]], axis=0),
                 jnp.concatenate([u["bi"], u["ki"]], axis=0), _NT) for u in units]
        pw = [jnp.where(units[n]["strict"], m[n][:L, :L], 0.0) for n in us]
        mak = [jnp.where(units[n]["strict"], m[n][:L, L:], 0.0) for n in us]
        mr = [jnp.where(units[n]["incl2"], m[n][L:, :], 0.0) for n in us]
        mv = [_mm(mak[n], units[n]["v"]) for n in us]
        x = [jnp.concatenate([units[n]["aw"], mv[n]], axis=1) for n in us]
        for it in range(6):
            if it < 5:
                pr = [_mm(pw[n], jnp.concatenate([x[n], pw[n]], axis=1)) for n in us]
                pw = [pr[n][:, 2 * RW_N:] for n in us]
                x = [x[n] + pr[n][:, :2 * RW_N] for n in us]
            else:
                pr = [_mm(pw[n], x[n]) for n in us]
                x = [x[n] + pr[n] for n in us]
        zero = jnp.zeros((L, RW_N), f32)
        ray = [_mm(mr[n], jnp.concatenate(
            [x[n], jnp.concatenate([zero, units[n]["v"]], axis=1)], axis=0)) for n in us]
        gt = [_mm(x[n], units[n]["bw"], _TN) for n in us]
        vk = [_mm(units[n]["v"], units[n]["kw"], _TN) for n in us]
        s_old = [units[n]["st"][units[n]["h"]] for n in us]
        yy = [_mm(units[n]["rw"] + ray[n][:, :RW_N], s_old[n], _NT) for n in us]
        sg = [_mm(s_old[n], gt[n][:RW_N]) for n in us]
        for si, (sc, st_scr, y_ref, rev) in enumerate(streams):
            mine = [n for n in us if units[n]["si"] == si]
            y_ref[rows_of[si], :] = jnp.concatenate(
                [yy[n] + ray[n][:, RW_N:] for n in mine], axis=1).astype(bf16)
            for n in mine:
                st_scr[units[n]["h"]] = (s_old[n] * units[n]["wend"] + sg[n]) + (gt[n][RW_N:] + vk[n])
        return carry

    lax.fori_loop(0, nch, chunk, 0)
    sff_ref[...] = stf_scr[...]
    sfb_ref[...] = stb_scr[...]


def _rwkv(pfull, s0f, s0b, mu, wup, w0, aup, a0, k_k, k_a, r_k, ones_bd, gup, nbr):
    bsz, t, _ = pfull.shape
    tb_ = RW_BLOCK
    nb = t // tb_
    cblk = P_RW // P_RW_PAD
    hpb = tb_ // RW_HALO
    nh = t // RW_HALO
    fwd = lambda j: j
    bwd = lambda j: nb - 1 - j

    def p_specs(tb):
        return [
            pl.BlockSpec((bsz, tb_, P_RW_PAD), lambda j: (0, tb(j), cblk)),
            pl.BlockSpec((bsz, RW_HALO, P_RW_PAD), lambda j: (0, jnp.maximum(tb(j) * hpb - 1, 0), cblk)),
            pl.BlockSpec((bsz, RW_HALO, P_RW_PAD), lambda j: (0, jnp.minimum((tb(j) + 1) * hpb, nh - 1), cblk)),
        ]

    full = lambda shape: pl.BlockSpec(shape, lambda j: (0,) * len(shape))
    st_spec = full((bsz, RW_HEADS, RW_N, RW_N))
    tok = lambda tb: pl.BlockSpec((bsz, tb_, RW_W), lambda j: (0, tb(j), 0))
    tok_shape = jax.ShapeDtypeStruct((bsz, t, RW_W), bf16)
    st_shape = jax.ShapeDtypeStruct((bsz, RW_HEADS, RW_N, RW_N), f32)
    return pl.pallas_call(
        functools.partial(_rwkv_kernel, nb=nb, bsz=bsz),
        grid=(nb,),
        in_specs=p_specs(fwd) + p_specs(bwd) + [
            st_spec, st_spec,
            full((1, P_RW_PAD)),
            full((2, RW_DECAY_LORA, RW_W)), full((2, RW_W)),
            full((2, RW_A_LORA, RW_W)), full((2, RW_W)),
            full((1, RW_W)), full((1, RW_W)), full((1, RW_W)),
            full((RW_W, RW_W)),
            full((RW_GATE_LORA, RW_W)),
            full((tb_, tb_)),
        ],
        out_specs=[tok(fwd), tok(fwd), tok(fwd), tok(bwd), tok(bwd), st_spec, st_spec],
        out_shape=[tok_shape] * 5 + [st_shape] * 2,
        scratch_shapes=[pltpu.VMEM((bsz, RW_HEADS, RW_N, RW_N), f32) for _ in range(2)]
        + [pltpu.VMEM((bsz, tb_, RW_W), f32) for _ in range(12)],
        compiler_params=pltpu.CompilerParams(
            dimension_semantics=("arbitrary",), vmem_limit_bytes=VMEM_LIMIT),
        name="rwkv",
    )(pfull, pfull, pfull, pfull, pfull, pfull, s0f, s0b, mu, wup, w0, aup, a0, k_k, k_a, r_k, ones_bd, gup, nbr)


def _merge_kernel(x_ref, hf_ref, hb_ref, o_ref_, yf_ref, yb_ref, bvf_ref, bvb_ref, g_ref, ga_ref, gb_ref,
                  g1_ref, mlg_ref, lnw_ref, lnb_ref, ones_ref, pa_ref, pb_ref, wo_ref, out_ref):
    hsum = hf_ref[...].astype(f32) + hb_ref[...].astype(f32)
    parts = []
    for h in range(ML_HEADS):
        hh = hsum[:, h * ML_DV:(h + 1) * ML_DV]
        parts.append(hh * lax.rsqrt(jnp.mean(hh * hh, axis=-1, keepdims=True) + NORM_EPS))
    a_lat = jnp.concatenate(parts, axis=1) * mlg_ref[...] * _sigmoid(o_ref_[...].astype(f32))
    y = yf_ref[...].astype(f32) + yb_ref[...].astype(f32)
    ones_bd = ones_ref[...]
    inv_n = 1.0 / RW_N
    mu = _head_sum(y, ones_bd) * inv_n
    dev = y - mu
    var = _head_sum(dev * dev, ones_bd) * inv_n
    yn = dev * lax.rsqrt(var + RW_GN_EPS) * lnw_ref[...] + lnb_ref[...]
    b_lat = (yn + (bvf_ref[...].astype(f32) + bvb_ref[...].astype(f32))) * g_ref[...].astype(f32)
    m = (_sigmoid(ga_ref[...].astype(f32)) * _mm(a_lat, pa_ref[...])
         + _sigmoid(gb_ref[...].astype(f32)) * _mm(b_lat, pb_ref[...]))
    out_ref[...] = x_ref[...] + g1_ref[...] * _mm(m, wo_ref[...])


def _merge(x, hf, hb, yf, yb, bvf, bvb, g, pfull, g1, ml_norm_g, ln_w, ln_b, ones_bd, pa_bf, pb_bf, wo_bf):
    bsz, t, _ = x.shape
    tm = min(t, 1024)
    full = lambda shape: pl.BlockSpec(shape, lambda b, i: (0,) * len(shape), pipeline_mode=pl.Buffered(1))
    tok = lambda w: pl.BlockSpec((None, tm, w), lambda b, i: (b, i, 0))
    col = lambda w, off: pl.BlockSpec((None, tm, w), lambda b, i: (b, i, off // w))
    return pl.pallas_call(
        _merge_kernel,
        grid=(bsz, t // tm),
        in_specs=[
            tok(D_MODEL), tok(ML_V), tok(ML_V), col(ML_V, P_O),
            tok(RW_W), tok(RW_W), tok(RW_W), tok(RW_W), tok(RW_W),
            col(D_MODEL, P_GA), col(D_MODEL, P_GB),
            pl.BlockSpec((None, 1, D_MODEL), lambda b, i: (b, 0, 0)),
            full((1, ML_V)), full((1, RW_W)), full((1, RW_W)), full((RW_W, RW_W)),
            full((ML_V, D_MODEL)), full((RW_W, D_MODEL)), full((D_MODEL, D_MODEL)),
        ],
        out_specs=tok(D_MODEL),
        out_shape=jax.ShapeDtypeStruct((bsz, t, D_MODEL), f32),
        compiler_params=pltpu.CompilerParams(
            dimension_semantics=("parallel", "parallel"), vmem_limit_bytes=MOE_VMEM_LIMIT),
        name="merge",
    )(x, hf, hb, pfull, yf, yb, bvf, bvb, g, pfull, pfull, g1, ml_norm_g, ln_w, ln_b, ones_bd,
      pa_bf, pb_bf, wo_bf)


def _top_max(x):
    return jnp.max(x, axis=0, keepdims=True)


def _first_at(x, val, eid_f):
    return jnp.min(jnp.where(x == val, eid_f, float(ROUTE_W)), axis=0, keepdims=True)


def _split3(x):
    hi = x.astype(bf16)
    r1 = x - hi.astype(f32)
    mid = r1.astype(bf16)
    return hi, mid, (r1 - mid.astype(f32)).astype(bf16)


def _route_kernel(x_ref, g_ref, sc_ref, sh_ref, rw_ref, rb_ref, rt_ref):
    h = _norm_mod(x_ref[...], g_ref[...], sc_ref[...], sh_ref[...])
    logits = _mm3(h, rw_ref[...])
    e_i = lax.broadcasted_iota(jnp.int32, (ROUTE_W, ROUTE_W), 0)
    e_j = lax.broadcasted_iota(jnp.int32, (ROUTE_W, ROUTE_W), 1)
    eye = (e_i == e_j).astype(bf16)
    scores = _sigmoid(_mm_exact_lhs(eye, logits, _NT))
    sel = scores + rb_ref[...]
    eid = lax.broadcasted_iota(jnp.int32, sel.shape, 0)
    eid_f = eid.astype(f32)
    grp = eid // EXPERTS_PER_GROUP
    neg = -jnp.inf
    gr = range(N_GROUPS)
    mg = [jnp.where(grp == gi, sel, neg) for gi in gr]
    m1 = [_top_max(mg[gi]) for gi in gr]
    i1 = [_first_at(mg[gi], m1[gi], eid_f) for gi in gr]
    m2 = [_top_max(jnp.where(eid_f == i1[gi], neg, mg[gi])) for gi in gr]
    best_g = jnp.zeros((1,) + sel.shape[1:], jnp.int32)
    best_v = m1[0] + m2[0]
    for gi in range(1, N_GROUPS):
        gs = m1[gi] + m2[gi]
        upd = gs > best_v
        best_g = jnp.where(upd, gi, best_g)
        best_v = jnp.where(upd, gs, best_v)
    cand = jnp.where(grp == best_g, sel, neg)
    v1 = _top_max(cand)
    p1 = _first_at(cand, v1, eid_f)
    cand2 = jnp.where(eid_f == p1, neg, cand)
    v2 = _top_max(cand2)
    p2 = _first_at(cand2, v2, eid_f)
    picked = (eid_f == p1) | (eid_f == p2)
    w = jnp.where(picked, scores, 0.0)
    comb = w / jnp.sum(w, axis=0, keepdims=True)
    out_t = jnp.where(eid == ROUTE_GID, best_g.astype(f32), comb)
    rt_ref[...] = sum(_dot(part, eye, _TN) for part in _split3(out_t))


def _route(x, g, sc, sh, router_w, router_b):
    bsz, t, _ = x.shape
    tm = min(t, 512)
    pad = ROUTE_W - N_EXPERTS
    rw = jnp.pad(router_w, ((0, 0), (0, pad)))
    rb = jnp.pad(router_b.reshape(N_EXPERTS, 1), ((0, pad), (0, 0)))
    return pl.pallas_call(
        _route_kernel,
        grid=(bsz, t // tm),
        in_specs=[
            pl.BlockSpec((None, tm, D_MODEL), lambda b, i: (b, i, 0)),
            pl.BlockSpec((1, D_MODEL), lambda b, i: (0, 0)),
            pl.BlockSpec((None, 1, D_MODEL), lambda b, i: (b, 0, 0)),
            pl.BlockSpec((None, 1, D_MODEL), lambda b, i: (b, 0, 0)),
            pl.BlockSpec((D_MODEL, ROUTE_W), lambda b, i: (0, 0)),
            pl.BlockSpec((ROUTE_W, 1), lambda b, i: (0, 0)),
        ],
        out_specs=pl.BlockSpec((None, tm, ROUTE_W), lambda b, i: (b, i, 0)),
        out_shape=jax.ShapeDtypeStruct((bsz, t, ROUTE_W), f32),
        compiler_params=pltpu.CompilerParams(
            dimension_semantics=("parallel", "parallel"), vmem_limit_bytes=VMEM_LIMIT),
        name="route",
    )(x, g, sc, sh, rw, rb)


def _moe_kernel(meta_ref, x_ref, g_ref, sc_ref, sh_ref, g2_ref, rt_ref, fg_ref, wg_ref, wu_ref, wd_ref, o_ref,
                xs_scr, ys_scr, cs_scr, pos_scr, *, final_norm, tm, nt):
    sb = MOE_SB
    n_rows = tm + 2 * sb
    grp = pl.program_id(2)
    base = (pl.program_id(0) * nt + pl.program_id(1)) * (2 * N_GROUPS)

    @pl.when(grp == 0)
    def _():
        h = _norm_mod(x_ref[...], g_ref[...], sc_ref[...], sh_ref[...]).astype(bf16)
        rt = rt_ref[...]
        lane = lax.broadcasted_iota(jnp.int32, rt.shape, 1)
        gid = jnp.sum(jnp.where(lane == ROUTE_GID, rt, 0.0), axis=-1, keepdims=True)
        g8 = lax.broadcasted_iota(jnp.int32, (tm, 8), 1)
        onehot = gid == g8.astype(f32)
        ii = lax.broadcasted_iota(jnp.int32, (tm, tm), 0)
        jj = lax.broadcasted_iota(jnp.int32, (tm, tm), 1)
        rank = _dot((jj <= ii).astype(bf16), onehot.astype(bf16))
        start = jnp.zeros((1, 8), f32)
        g8r = lax.broadcasted_iota(jnp.int32, (1, 8), 1)
        for gi in range(N_GROUPS):
            start = jnp.where(g8r == gi, meta_ref[base + gi].astype(f32), start)
        pos = jnp.sum(jnp.where(onehot, (start + rank) - 1.0, 0.0), axis=-1, keepdims=True)
        posmat = jnp.broadcast_to(pos, (tm, 8))
        pos_scr[...] = posmat
        e_i = lax.broadcasted_iota(jnp.int32, (8, 8), 0)
        e_j = lax.broadcasted_iota(jnp.int32, (8, 8), 1)
        pos_row = _mm_exact_lhs((e_i == e_j).astype(bf16), posmat, _NT)[0:1, :]
        hi = rt.astype(bf16)
        r1 = rt - hi.astype(f32)
        mid = r1.astype(bf16)
        lo = (r1 - mid.astype(f32)).astype(bf16)
        rt3 = jnp.concatenate([hi, mid, lo], axis=1)
        for c in range(n_rows // sb):
            rid = (lax.broadcasted_iota(jnp.int32, (sb, 1), 0) + c * sb).astype(f32)
            perm = (rid == pos_row).astype(bf16)
            xs_scr[c * sb:(c + 1) * sb, :] = _dot(perm, h).astype(bf16)
            cc = _dot(perm, rt3)
            cs_scr[c * sb:(c + 1) * sb, :] = cc[:, :ROUTE_W] + (cc[:, ROUTE_W:2 * ROUTE_W] + cc[:, 2 * ROUTE_W:])
        ys_scr[...] = jnp.zeros_like(ys_scr)

    seg_start = meta_ref[base + grp]
    seg_blocks = meta_ref[base + N_GROUPS + grp]
    lane_sb = lax.broadcasted_iota(jnp.int32, (sb, ROUTE_W), 1)

    def block(kb, carry):
        rows = pl.ds(pl.multiple_of(seg_start + kb * sb, MOE_ALIGN), sb)
        xs = xs_scr[rows, :]
        cs = cs_scr[rows, :]
        acc = None
        for k in range(EXPERTS_PER_GROUP):
            gate = _dot(xs, wg_ref[k])
            hid = (gate * _sigmoid(gate)) * _dot(xs, wu_ref[k])
            ce = jnp.sum(jnp.where(lane_sb == grp * EXPERTS_PER_GROUP + k, cs, 0.0), axis=-1, keepdims=True)
            part = _dot((ce * hid).astype(bf16), wd_ref[k])
            acc = part if acc is None else acc + part
        ys_scr[rows, :] = acc.astype(bf16)
        return carry

    lax.fori_loop(0, seg_blocks, block, 0)

    @pl.when(grp == N_GROUPS - 1)
    def _():
        cid = lax.broadcasted_iota(jnp.int32, (1, n_rows), 1).astype(f32)
        chunks = [slice(c * sb, (c + 1) * sb) for c in range(tm // sb)]
        ys = [_dot((pos_scr[rows, 0:1] == cid).astype(bf16), ys_scr[...]) for rows in chunks]
        ys = [x_ref[rows, :] + g2_ref[...] * ys[c] for c, rows in enumerate(chunks)]
        if final_norm:
            ms = [jnp.mean(y * y, axis=-1, keepdims=True) for y in ys]
            ys = [y * lax.rsqrt(ms[c] + NORM_EPS) * fg_ref[...] for c, y in enumerate(ys)]
        for c, rows in enumerate(chunks):
            o_ref[rows, :] = ys[c]


def _moe(x, g, sc, sh, g2, rt, final_g, final_norm, w_gate, w_up, w_down):
    bsz, t, _ = x.shape
    tm = min(t, 1024)
    nt = t // tm
    sb = MOE_SB
    n_rows = tm + 2 * sb
    gid = rt[..., ROUTE_GID].astype(jnp.int32).reshape(bsz, nt, tm)
    cnt = jnp.sum(gid[..., None] == jnp.arange(N_GROUPS), axis=2).astype(jnp.int32)
    nblk = (cnt + (sb - 1)) // sb
    seg = ((cnt + (MOE_ALIGN - 1)) // MOE_ALIGN) * MOE_ALIGN
    start = jnp.cumsum(seg, axis=-1) - seg
    meta = jnp.concatenate([start, nblk], axis=-1).reshape(-1).astype(jnp.int32)
    wspec = lambda shape: pl.BlockSpec((EXPERTS_PER_GROUP,) + shape, lambda b, i, e, m: (e, 0, 0))
    grid_spec = pltpu.PrefetchScalarGridSpec(
        num_scalar_prefetch=1,
        grid=(bsz, nt, N_GROUPS),
        in_specs=[
            pl.BlockSpec((None, tm, D_MODEL), lambda b, i, e, m: (b, i, 0)),
            pl.BlockSpec((1, D_MODEL), lambda b, i, e, m: (0, 0)),
            pl.BlockSpec((None, 1, D_MODEL), lambda b, i, e, m: (b, 0, 0)),
            pl.BlockSpec((None, 1, D_MODEL), lambda b, i, e, m: (b, 0, 0)),
            pl.BlockSpec((None, 1, D_MODEL), lambda b, i, e, m: (b, 0, 0)),
            pl.BlockSpec((None, tm, ROUTE_W), lambda b, i, e, m: (b, i, 0)),
            pl.BlockSpec((1, D_MODEL), lambda b, i, e, m: (0, 0)),
            wspec((D_MODEL, D_EXPERT)), wspec((D_MODEL, D_EXPERT)), wspec((D_EXPERT, D_MODEL)),
        ],
        out_specs=pl.BlockSpec((None, tm, D_MODEL), lambda b, i, e, m: (b, i, 0)),
        scratch_shapes=[pltpu.VMEM((n_rows, D_MODEL), bf16), pltpu.VMEM((n_rows, D_MODEL), bf16),
                        pltpu.VMEM((n_rows, ROUTE_W), f32), pltpu.VMEM((tm, 8), f32)],
    )
    return pl.pallas_call(
        functools.partial(_moe_kernel, final_norm=final_norm, tm=tm, nt=nt),
        grid_spec=grid_spec,
        out_shape=jax.ShapeDtypeStruct((bsz, t, D_MODEL), f32),
        compiler_params=pltpu.CompilerParams(
            dimension_semantics=("parallel", "parallel", "arbitrary"), vmem_limit_bytes=MOE_VMEM_LIMIT),
        name="moe",
    )(meta, x, g, sc, sh, g2, rt, final_g, w_gate, w_up, w_down)


def _cast_kernel(a_ref, b_ref, c_ref, ao_ref, bo_ref, co_ref):
    ao_ref[...] = a_ref[...].astype(bf16)
    bo_ref[...] = b_ref[...].astype(bf16)
    co_ref[...] = c_ref[...].astype(bf16)


def _cast_experts(w_gate, w_up, w_down, l):
    per = 2
    shapes = [(D_MODEL, D_EXPERT), (D_MODEL, D_EXPERT), (D_EXPERT, D_MODEL)]
    return pl.pallas_call(
        _cast_kernel,
        grid=(N_EXPERTS // per,),
        in_specs=[pl.BlockSpec((None, per) + s, lambda i: (l, i, 0, 0)) for s in shapes],
        out_specs=[pl.BlockSpec((per,) + s, lambda i: (i, 0, 0)) for s in shapes],
        out_shape=[jax.ShapeDtypeStruct((N_EXPERTS,) + s, bf16) for s in shapes],
        compiler_params=pltpu.CompilerParams(
            dimension_semantics=("arbitrary",), vmem_limit_bytes=VMEM_LIMIT),
        name="cast_experts",
    )(w_gate, w_up, w_down)


def _pack_w_in(w_in):
    ml, rw, gt = w_in[:, :ML_COLS], w_in[:, ML_COLS:ML_COLS + RW_COLS], w_in[:, ML_COLS + RW_COLS:]
    qkvo, mlg = ml[:, :2 * ML_QK + 2 * ML_V], ml[:, 2 * ML_QK + 2 * ML_V:]
    z = lambda n: jnp.zeros((D_MODEL, n), w_in.dtype)
    return jnp.concatenate(
        [gt, rw, mlg, z(P_RW_PAD - RW_COLS - 4 * ML_HEADS), qkvo], axis=1).astype(bf16)


def _mixer(pfull, is_ctx, lp, ml_state, rw_state):
    conv_args = (pfull, lp["taps"], lp["conv_b"], lp["gate_b"])
    h_f, c_f, qk_conv = _mlstm(*conv_args, ml_state[0], rev=False, d=0, grid_conv=not is_ctx)
    h_b, c_b = _mlstm(*conv_args, ml_state[1], rev=True, d=1, grid_conv=not is_ctx, qk_conv=qk_conv)
    hs, ml_fin = [h_f, h_b], [c_f, c_b]
    yf, bvf, g, yb, bvb, sff, sfb = _rwkv(pfull, rw_state[0], rw_state[1], lp["rw_mu"], lp["rw_w_up"],
                                          lp["rw_w0"], lp["rw_a_up"], lp["rw_a0"], lp["rw_k_k"], lp["rw_k_a"],
                                          lp["rw_r_k"], lp["ones_bd"], lp["gup"], lp["nbr"])
    return (hs[0], hs[1], yf, yb, bvf, bvb, g), ml_fin, [sff, sfb]


def kernel(x, c, ctx, c_ctx, w_ada, b_ada, norm1_g, norm2_g, w_in, ml_conv_k, ml_conv_b, ml_gate_b, ml_norm_g, rw_mu, rw_w_up, rw_w0, rw_a_up, rw_a0, rw_g_up, rw_k_k, rw_k_a, rw_r_k, rw_ln_w, rw_ln_b, merge_pa, merge_pb, w_out, router_w, router_b, exp_w_gate, exp_w_up, exp_w_down, final_g):
    bsz = x.shape[0]
    s_rows = jnp.zeros((8, D_MODEL), f32).at[:bsz].set(c).at[bsz].set(c_ctx)
    mod = _ada(s_rows, w_ada, b_ada)
    head_id = jnp.arange(RW_W) // RW_N
    ones_bd = (head_id[:, None] == head_id[None, :]).astype(bf16)
    tok_id = jnp.arange(RW_BLOCK)
    nbr = (0.5 * (jnp.abs(tok_id[:, None] - tok_id[None, :]) == 1)).astype(bf16)
    row = lambda v: v.reshape(1, -1)

    x_lat, x_ctx = x, ctx
    for l in range(DEPTH):
        last = l == DEPTH - 1
        mod_lat = mod[l, :bsz].reshape(bsz, 1, N_MOD, D_MODEL)
        mod_ctx = jnp.broadcast_to(mod[l, bsz].reshape(1, 1, N_MOD, D_MODEL), (bsz, 1, N_MOD, D_MODEL))
        lp = dict(taps=ml_conv_k[l].reshape(9, 2 * ML_QK), conv_b=row(ml_conv_b[l]), gate_b=row(ml_gate_b[l]),
                  rw_mu=jnp.pad(row(rw_mu[l]), ((0, 0), (0, P_RW_PAD - RW_COLS))),
                  rw_w_up=rw_w_up[l].astype(bf16), rw_w0=rw_w0[l], rw_a_up=rw_a_up[l].astype(bf16),
                  rw_a0=rw_a0[l], rw_k_k=row(rw_k_k[l]), rw_k_a=row(rw_k_a[l]), rw_r_k=row(rw_r_k[l]),
                  ones_bd=ones_bd, gup=rw_g_up[l].astype(bf16), nbr=nbr)
        w_in_bf = _pack_w_in(w_in[l])
        pa_bf, pb_bf, wo_bf = merge_pa[l].astype(bf16), merge_pb[l].astype(bf16), w_out[l].astype(bf16)
        experts = _cast_experts(exp_w_gate, exp_w_up, exp_w_down, l)
        g1n, g2n = row(norm1_g[l]), row(norm2_g[l])
        readout = (row(ml_norm_g[l]), row(rw_ln_w[l]), row(rw_ln_b[l]), ones_bd, pa_bf, pb_bf, wo_bf)

        def m(modv, i):
            return modv[:, :, i]

        p_ctx = _proj(x_ctx, g1n, m(mod_ctx, 1), m(mod_ctx, 0), w_in_bf)
        ml0 = [jnp.zeros((bsz, ML_HEADS, ML_DQK, 2 * ML_DV), f32)] * 2
        rw0 = [jnp.zeros((bsz, RW_HEADS, RW_N, RW_N), f32)] * 2
        mix_c, ml_st, rw_st = _mixer(p_ctx, True, lp, ml0, rw0)

        p_lat = _proj(x_lat, g1n, m(mod_lat, 1), m(mod_lat, 0), w_in_bf)
        mix_l, _, _ = _mixer(p_lat, False, lp, ml_st, rw_st)
        x_lat = _merge(x_lat, *mix_l, p_lat, m(mod_lat, 2), *readout)
        comb = _route(x_lat, g2n, m(mod_lat, 4), m(mod_lat, 3), router_w, router_b)
        x_lat = _moe(x_lat, g2n, m(mod_lat, 4), m(mod_lat, 3), m(mod_lat, 5), comb, row(final_g), last, *experts)
        if not last:
            x_ctx = _merge(x_ctx, *mix_c, p_ctx, m(mod_ctx, 2), *readout)
            comb_c = _route(x_ctx, g2n, m(mod_ctx, 4), m(mod_ctx, 3), router_w, router_b)
            x_ctx = _moe(x_ctx, g2n, m(mod_ctx, 4), m(mod_ctx, 3), m(mod_ctx, 5), comb_c, row(final_g), False,
                         *experts)
    return x_lat
```
